```python
import math
import jax, jax.numpy as jnp
from jax import lax
import numpy as np

D_MODEL = 1024
BATCH = 8
SEQ = 2048
DEPTH = 2

N_META = 16
BLOCK = 128
MLA_HEADS = 4
Q_LORA = 256
KV_LORA = 256
QK_NOPE = 128
QK_ROPE = 64
QK_HEAD = QK_NOPE + QK_ROPE
V_HEAD = 128
MLA_WIDTH = MLA_HEADS * V_HEAD
RET_HEADS = 4
RET_HEAD = 128
RET_WIDTH = RET_HEADS * RET_HEAD
MIX_WIDTH = MLA_WIDTH + RET_WIDTH
IN_SIZES = (Q_LORA, KV_LORA, QK_ROPE, RET_WIDTH, RET_WIDTH, RET_WIDTH, RET_WIDTH)
N_IN = sum(IN_SIZES)
D_FF = -(-(8 * D_MODEL) // (3 * 256)) * 256
ROPE_BASE = 10000.0
EPS = 1e-6
NEG_INF = -1e30

kernel_name = "hybrid_mla_retention_swiglu"


def rms_norm(x, g):
    xf = x.astype(jnp.float32)
    y = xf * lax.rsqrt(jnp.mean(xf * xf, axis=-1, keepdims=True) + EPS)
    return (y * g.astype(jnp.float32)).astype(x.dtype)


def rope_tables(n_pos, dim):
    inv = ROPE_BASE ** (-jnp.arange(0, dim, 2, dtype=jnp.float32) / dim)
    ang = jnp.arange(n_pos, dtype=jnp.float32)[:, None] * inv[None, :]
    return jnp.cos(ang), jnp.sin(ang)


def apply_rope(x, cos, sin):
    x1, x2 = jnp.split(x, 2, axis=-1)
    c = cos[:, None, :].astype(x.dtype)
    s = sin[:, None, :].astype(x.dtype)
    return jnp.concatenate([x1 * c - x2 * s, x2 * c + x1 * s], axis=-1)


def mla_mixer(c_q, c_kv, k_pe, q_a_g, w_q_b, kv_a_g, w_kv_b, q_g, k_g, out_g, cos, sin):
    B, L, _ = c_q.shape
    q = (rms_norm(c_q, q_a_g) @ w_q_b).reshape(B, L, MLA_HEADS, QK_HEAD)
    kv = (rms_norm(c_kv, kv_a_g) @ w_kv_b).reshape(B, L, MLA_HEADS, QK_NOPE + V_HEAD)
    k_nope, v = kv[..., :QK_NOPE], kv[..., QK_NOPE:]
    k_pe_h = jnp.broadcast_to(k_pe[:, :, None, :], (B, L, MLA_HEADS, QK_ROPE))
    k = jnp.concatenate([k_nope, k_pe_h], axis=-1)
    q = rms_norm(q, q_g)
    k = rms_norm(k, k_g)
    q = jnp.concatenate([q[..., :QK_NOPE], apply_rope(q[..., QK_NOPE:], cos, sin)], axis=-1)
    k = jnp.concatenate([k[..., :QK_NOPE], apply_rope(k[..., QK_NOPE:], cos, sin)], axis=-1)
    scale = QK_HEAD ** -0.5
    bounds = [0] + [N_META + BLOCK * j for j in range((L - N_META) // BLOCK + 1)]
    outs = []
    for start, end in zip(bounds[:-1], bounds[1:]):
        qb, kb, vb = q[:, start:end], k[:, :end], v[:, :end]
        s = jnp.einsum('bqhd,bkhd->bhqk', qb, kb).astype(jnp.float32) * scale
        causal = jnp.arange(start, end)[:, None] >= jnp.arange(end)[None, :]
        s = jnp.where(causal[None, None], s, NEG_INF)
        p = jax.nn.softmax(s, axis=-1).astype(vb.dtype)
        outs.append(jnp.einsum('bhqk,bkhd->bqhd', p, vb))
    o = jnp.concatenate(outs, axis=1)
    o = rms_norm(o, out_g.reshape(MLA_HEADS, V_HEAD))
    return o.reshape(B, L, MLA_WIDTH)


def retention_mixer(rq, rk, rv, rg, norm_g, norm_b, cos, sin):
    B, L, _ = rq.shape
    dt = rq.dtype
    q = apply_rope(rq.reshape(B, L, RET_HEADS, RET_HEAD), cos, sin).astype(jnp.float32)
    k = (apply_rope(rk.reshape(B, L, RET_HEADS, RET_HEAD), cos, sin) * RET_HEAD ** -0.5).astype(jnp.float32)
    v = rv.reshape(B, L, RET_HEADS, RET_HEAD).astype(jnp.float32)
    pad = BLOCK - N_META
    padw = ((0, 0), (pad, 0), (0, 0), (0, 0))
    q, k, v = (jnp.pad(t, padw) for t in (q, k, v))
    Lp = L + pad
    n_chunks = Lp // BLOCK

    def to_chunks(t):
        return t.reshape(B, n_chunks, BLOCK, RET_HEADS, RET_HEAD).transpose(1, 0, 3, 2, 4)

    qc, kc, vc = to_chunks(q), to_chunks(k), to_chunks(v)
    gamma = 1.0 - 2.0 ** (-5.0 - jnp.arange(RET_HEADS, dtype=jnp.float32))
    log_g = jnp.log(gamma)
    idx = jnp.arange(BLOCK, dtype=jnp.float32)
    diff = idx[:, None] - idx[None, :]
    decay = jnp.where(diff >= 0, jnp.exp(jnp.maximum(diff, 0.0)[None] * log_g[:, None, None]), 0.0)
    xi = jnp.exp((idx + 1.0)[None, :] * log_g[:, None])
    zeta = jnp.exp((BLOCK - 1.0 - idx)[None, :] * log_g[:, None])
    chunk_decay = jnp.exp(BLOCK * log_g)

    def step(state, inp):
        qb, kb, vb = inp
        s = jnp.einsum('bhnd,bhmd->bhnm', qb, kb) * decay[None]
        inner = jnp.einsum('bhnm,bhmd->bhnd', s, vb)
        cross = jnp.einsum('bhnd,bhde->bhne', qb, state) * xi[None, :, :, None]
        new_state = state * chunk_decay[None, :, None, None] + jnp.einsum(
            'bhmd,bhme->bhde', kb * zeta[None, :, :, None], vb)
        return new_state, inner + cross

    state0 = jnp.zeros((B, RET_HEADS, RET_HEAD, RET_HEAD), jnp.float32)
    _, o = lax.scan(step, state0, (qc, kc, vc))
    o = o.transpose(1, 0, 3, 2, 4).reshape(B, Lp, RET_HEADS, RET_HEAD)[:, pad:]
    mu = jnp.mean(o, axis=-1, keepdims=True)
    var = jnp.mean(jnp.square(o - mu), axis=-1, keepdims=True)
    o = ((o - mu) * lax.rsqrt(var + EPS)).reshape(B, L, RET_WIDTH)
    o = o * norm_g.astype(jnp.float32) + norm_b.astype(jnp.float32)
    return (jax.nn.silu(rg.astype(jnp.float32)) * o).astype(dt)


def setup_inputs(seed: int = 0) -> dict:
    key = jax.random.key(seed)
    ks = jax.random.split(key, 20)

    def w(k, shape, fan_in):
        return jax.random.normal(k, shape, jnp.float32) * fan_in ** -0.5

    def gain(k, shape):
        return 1.0 + 0.02 * jax.random.normal(k, shape, jnp.float32)

    return {
        "x": jax.random.normal(ks[0], (BATCH, SEQ, D_MODEL), jnp.float32),
        "meta_tokens": jax.random.normal(ks[1], (N_META, D_MODEL), jnp.float32),
        "attn_norm_g": gain(ks[2], (DEPTH, D_MODEL)),
        "w_in": w(ks[3], (DEPTH, D_MODEL, N_IN), D_MODEL),
        "q_a_norm_g": gain(ks[4], (DEPTH, Q_LORA)),
        "w_q_b": w(ks[5], (DEPTH, Q_LORA, MLA_HEADS * QK_HEAD), Q_LORA),
        "kv_a_norm_g": gain(ks[6], (DEPTH, KV_LORA)),
        "w_kv_b": w(ks[7], (DEPTH, KV_LORA, MLA_HEADS * (QK_NOPE + V_HEAD)), KV_LORA),
        "q_norm_g": gain(ks[8], (DEPTH, QK_HEAD)),
        "k_norm_g": gain(ks[9], (DEPTH, QK_HEAD)),
        "mla_out_norm_g": gain(ks[10], (DEPTH, MLA_WIDTH)),
        "ret_norm_g": gain(ks[11], (DEPTH, RET_WIDTH)),
        "ret_norm_b": 0.02 * jax.random.normal(ks[12], (DEPTH, RET_WIDTH), jnp.float32),
        "w_out": w(ks[13], (DEPTH, MIX_WIDTH, D_MODEL), MIX_WIDTH),
        "ffn_norm_g": gain(ks[14], (DEPTH, D_MODEL)),
        "w_gate_up": w(ks[15], (DEPTH, D_MODEL, 2 * D_FF), D_MODEL),
        "w_down": w(ks[16], (DEPTH, D_FF, D_MODEL), D_FF),
    }


def reference(x, meta_tokens, attn_norm_g, w_in, q_a_norm_g, w_q_b, kv_a_norm_g, w_kv_b,
              q_norm_g, k_norm_g, mla_out_norm_g, ret_norm_g, ret_norm_b, w_out,
              ffn_norm_g, w_gate_up, w_down):
    B = x.shape[0]
    meta = jnp.broadcast_to(meta_tokens[None].astype(x.dtype), (B, N_META, D_MODEL))
    h_res = jnp.concatenate([meta, x], axis=1)
    L = h_res.shape[1]
    cos_m, sin_m = rope_tables(L, QK_ROPE)
    cos_r, sin_r = rope_tables(L, RET_HEAD)
    split_idx = [int(v) for v in np.cumsum(IN_SIZES)[:-1]]
    for l in range(DEPTH):
        h = rms_norm(h_res, attn_norm_g[l])
        z = h @ w_in[l]
        c_q, c_kv, k_pe, rq, rk, rv, rg = jnp.split(z, split_idx, axis=-1)
        y_mla = mla_mixer(c_q, c_kv, k_pe, q_a_norm_g[l], w_q_b[l], kv_a_norm_g[l], w_kv_b[l],
                          q_norm_g[l], k_norm_g[l], mla_out_norm_g[l], cos_m, sin_m)
        y_ret = retention_mixer(rq, rk, rv, rg, ret_norm_g[l], ret_norm_b[l], cos_r, sin_r)
        y = jnp.concatenate([y_mla, y_ret], axis=-1) @ w_out[l]
        h_res = h_res + y
        hf = rms_norm(h_res, ffn_norm_g[l])
        gate, up = jnp.split(hf @ w_gate_up[l], 2, axis=-1)
        h_res = h_res + (jax.nn.silu(gate) * up) @ w_down[l]
    return h_res[:, N_META:]
```

```python
import functools

import jax
import jax.numpy as jnp
from jax import lax
from jax.experimental import pallas as pl
from jax.experimental.pallas import tpu as pltpu

D_MODEL = 1024
SEQ = 2048
N_META = 16
BLOCK = 128
MLA_HEADS = 4
Q_LORA = 256
KV_LORA = 256
QK_NOPE = 128
QK_ROPE = 64
QK_HEAD = QK_NOPE + QK_ROPE
V_HEAD = 128
MLA_WIDTH = MLA_HEADS * V_HEAD
RET_HEADS = 4
RET_HEAD = 128
RET_WIDTH = RET_HEADS * RET_HEAD
D_FF = 2816
ROPE_BASE = 10000.0
EPS = 1e-6
NEG_INF = -1e30

LANES = 128
PAD_ROWS = BLOCK - N_META
LP = SEQ + BLOCK
QK_PAD = 2 * LANES

C_CQ, C_CKV, C_KPE = 0, Q_LORA, Q_LORA + KV_LORA
C_RQ = C_KPE + LANES
C_RK = C_RQ + RET_WIDTH
C_RV = C_RK + RET_WIDTH
C_RG = C_RV + RET_WIDTH
N_IN_PAD = C_RG + RET_WIDTH

PRE_TM = 544
POST_TM_LAST = 512
ATT_TQ = 256
ATT_TK = 256
FF_TILE = 256
VMEM_LIMIT = 56 * 1024 * 1024

f32 = jnp.float32
bf16 = jnp.bfloat16


def _dot(a, b):
    return jnp.dot(a, b, preferred_element_type=f32)


def _dot_nt(a, b):
    return lax.dot_general(a, b, (((1,), (1,)), ((), ())), preferred_element_type=f32)


def _dot_tn(a, b):
    return lax.dot_general(a, b, (((0,), (0,)), ((), ())), preferred_element_type=f32)


def _rms(x, inv_n):
    return lax.rsqrt(jnp.sum(x * x, axis=-1, keepdims=True) * inv_n + EPS)


def _sigmoid(x):
    return 1.0 / (1.0 + jnp.exp(-x))


def _pre_kernel(h_ref, g_ref, win_ref, qag_ref, wqb_ref, kvag_ref, wkvb_ref, qg_ref, kgn_ref,
                kgp_ref, cm_ref, s1m_ref, s2m_ref, cr_ref, sr_ref, crk_ref, srk_ref,
                q_out, k_out, v_out, rq_out, rk_out, rv_out, rg_out):
    x = h_ref[...]
    hb = (x * _rms(x, 1.0 / D_MODEL) * g_ref[...]).astype(bf16)

    def proj(lo, hi):
        return _dot(hb, win_ref[:, lo:hi])

    cm, s1m, s2m = cm_ref[...], s1m_ref[...], s2m_ref[...]

    def rope_mla(v):
        return v * cm + pltpu.roll(v, 96, 1) * s1m + pltpu.roll(v, 32, 1) * s2m

    cq = proj(C_CQ, C_CQ + Q_LORA)
    cqn = (cq * _rms(cq, 1.0 / Q_LORA) * qag_ref[...]).astype(bf16)
    q = _dot(cqn, wqb_ref[...])
    qg = qg_ref[...]
    for h in range(MLA_HEADS):
        qh = q[:, h * QK_PAD:(h + 1) * QK_PAD]
        qn = qh * _rms(qh, 1.0 / QK_HEAD) * qg
        q_out[:, h * QK_PAD:h * QK_PAD + LANES] = qn[:, :LANES].astype(bf16)
        q_out[:, h * QK_PAD + LANES:(h + 1) * QK_PAD] = rope_mla(qn[:, LANES:]).astype(bf16)

    ckv = proj(C_CKV, C_CKV + KV_LORA)
    ckvn = (ckv * _rms(ckv, 1.0 / KV_LORA) * kvag_ref[...]).astype(bf16)
    kv = _dot(ckvn, wkvb_ref[...])
    kpe = proj(C_KPE, C_KPE + LANES)
    ss_pe = jnp.sum(kpe * kpe, axis=-1, keepdims=True)
    kpe_rot = rope_mla(kpe * kgp_ref[...])
    kgn = kgn_ref[...]
    for h in range(MLA_HEADS):
        kn = kv[:, h * QK_PAD:h * QK_PAD + LANES]
        r = lax.rsqrt((jnp.sum(kn * kn, axis=-1, keepdims=True) + ss_pe) * (1.0 / QK_HEAD) + EPS)
        k_out[:, h * QK_PAD:h * QK_PAD + LANES] = (kn * r * kgn).astype(bf16)
        k_out[:, h * QK_PAD + LANES:(h + 1) * QK_PAD] = (kpe_rot * r).astype(bf16)
        v_out[:, h * V_HEAD:(h + 1) * V_HEAD] = kv[:, h * QK_PAD + LANES:(h + 1) * QK_PAD].astype(bf16)

    cr, sr, crk, srk = cr_ref[...], sr_ref[...], crk_ref[...], srk_ref[...]
    rq = proj(C_RQ, C_RQ + RET_WIDTH)
    rk = proj(C_RK, C_RK + RET_WIDTH)
    for h in range(RET_HEADS):
        sl = slice(h * RET_HEAD, (h + 1) * RET_HEAD)
        xq = rq[:, sl]
        rq_out[:, sl] = (xq * cr + pltpu.roll(xq, RET_HEAD // 2, 1) * sr).astype(bf16)
        xk = rk[:, sl]
        rk_out[:, sl] = (xk * crk + pltpu.roll(xk, RET_HEAD // 2, 1) * srk).astype(bf16)
    rv_out[...] = proj(C_RV, C_RV + RET_WIDTH).astype(bf16)
    rg_out[...] = proj(C_RG, C_RG + RET_WIDTH).astype(bf16)


def _const_spec(shape):
    nd = len(shape)
    return pl.BlockSpec(shape, lambda *_: (0,) * nd, pipeline_mode=pl.Buffered(1))


def _pre_call(h, p, tabs):
    B = h.shape[0]
    tm = PRE_TM
    grid = (B, LP // tm)
    row = lambda w: pl.BlockSpec((None, tm, w), lambda b, j: (b, j, 0))
    tab = pl.BlockSpec((tm, LANES), lambda b, j: (j, 0))
    in_specs = [
        row(D_MODEL),
        _const_spec((1, D_MODEL)),
        _const_spec((D_MODEL, N_IN_PAD)),
        _const_spec((1, Q_LORA)),
        _const_spec((Q_LORA, MLA_HEADS * QK_PAD)),
        _const_spec((1, KV_LORA)),
        _const_spec((KV_LORA, MLA_HEADS * QK_PAD)),
        _const_spec((1, QK_PAD)),
        _const_spec((1, LANES)),
        _const_spec((1, LANES)),
    ] + [tab] * 7
    widths = (MLA_HEADS * QK_PAD, MLA_HEADS * QK_PAD, MLA_WIDTH, RET_WIDTH, RET_WIDTH, RET_WIDTH, RET_WIDTH)
    out_shape = [jax.ShapeDtypeStruct((B, LP, w), bf16) for w in widths]
    out_specs = [row(w) for w in widths]
    return pl.pallas_call(
        _pre_kernel, grid=grid, in_specs=in_specs, out_specs=out_specs, out_shape=out_shape,
        name="pre",
        compiler_params=pltpu.CompilerParams(
            dimension_semantics=("parallel", "parallel"), vmem_limit_bytes=VMEM_LIMIT),
    )(h, p["attn_g"], p["w_in"], p["qa_g"], p["w_qb"], p["kva_g"], p["w_kvb"], p["q_g"],
      p["k_gn"], p["k_gp"], *tabs)


def _attn_kernel(q_ref, k_ref, v_ref, og_ref, o_ref):
    og = og_ref[...]
    k_meta = k_ref[SEQ:LP, :]
    v_meta = v_ref[SEQ:LP, :]
    meta_valid = lax.broadcasted_iota(jnp.int32, (1, BLOCK), 1) >= PAD_ROWS

    def online(q, kb, vb, carry, mask):
        m, l, acc = carry
        s = _dot_nt(q, kb)
        if mask is not None:
            s = jnp.where(mask, s, NEG_INF)
        m_new = jnp.maximum(m, jnp.max(s, axis=-1, keepdims=True))
        alpha = jnp.exp(m - m_new)
        p = jnp.exp(s - m_new)
        l = alpha * l + jnp.sum(p, axis=-1, keepdims=True)
        acc = alpha * acc + _dot(p.astype(bf16), vb)
        return m_new, l, acc

    def finish(carry):
        _, l, acc = carry
        o = acc * (1.0 / l)
        return (o * _rms(o, 1.0 / V_HEAD) * og).astype(bf16)

    def init(rows):
        return (jnp.full((rows, 1), NEG_INF, f32), jnp.zeros((rows, 1), f32),
                jnp.zeros((rows, V_HEAD), f32))

    tri = (lax.broadcasted_iota(jnp.int32, (ATT_TQ, ATT_TQ), 0)
           >= lax.broadcasted_iota(jnp.int32, (ATT_TQ, ATT_TQ), 1))

    for i in range(SEQ // ATT_TQ):
        q = q_ref[i * ATT_TQ:(i + 1) * ATT_TQ, :]
        carry = online(q, k_meta, v_meta, init(ATT_TQ), meta_valid)

        def body(j, c, q=q):
            start = pl.multiple_of(j * ATT_TK, ATT_TK)
            return online(q, k_ref[pl.ds(start, ATT_TK), :], v_ref[pl.ds(start, ATT_TK), :], c, None)

        carry = lax.fori_loop(0, (i * ATT_TQ) // ATT_TK, body, carry)
        carry = online(q, k_ref[i * ATT_TQ:(i + 1) * ATT_TQ, :], v_ref[i * ATT_TQ:(i + 1) * ATT_TQ, :],
                       carry, tri)
        o_ref[i * ATT_TQ:(i + 1) * ATT_TQ, :] = finish(carry)

    q = q_ref[SEQ:LP, :]
    mmask = (lax.broadcasted_iota(jnp.int32, (BLOCK, BLOCK), 0)
             >= lax.broadcasted_iota(jnp.int32, (BLOCK, BLOCK), 1)) & meta_valid
    o_ref[SEQ:LP, :] = finish(online(q, k_meta, v_meta, init(BLOCK), mmask))


def _attn_call(q, k, v, out_g):
    B = q.shape[0]
    qk_spec = pl.BlockSpec((None, LP, QK_PAD), lambda b, h: (b, 0, h))
    v_spec = pl.BlockSpec((None, LP, V_HEAD), lambda b, h: (b, 0, h))
    return pl.pallas_call(
        _attn_kernel, grid=(B, MLA_HEADS),
        in_specs=[qk_spec, qk_spec, v_spec, pl.BlockSpec((1, V_HEAD), lambda b, h: (0, h))],
        out_specs=v_spec,
        out_shape=jax.ShapeDtypeStruct((B, LP, MLA_WIDTH), bf16),
        name="attn",
        compiler_params=pltpu.CompilerParams(
            dimension_semantics=("parallel", "parallel"), vmem_limit_bytes=VMEM_LIMIT),
    )(q, k, v, out_g)


def _ret_kernel(q_ref, k_ref, v_ref, g_ref, lg_ref, ng_ref, nb_ref, o_ref):
    lg = lg_ref[0:1, :]
    ri = lax.broadcasted_iota(jnp.int32, (BLOCK, BLOCK), 0)
    ci = lax.broadcasted_iota(jnp.int32, (BLOCK, BLOCK), 1)
    diff = (ri - ci).astype(f32)
    decay = jnp.where(diff >= 0, jnp.exp(jnp.maximum(diff, 0.0) * lg), 0.0)
    idx = ri.astype(f32)
    xi = jnp.exp((idx + 1.0) * lg)
    zeta = jnp.exp((BLOCK - 1.0 - idx) * lg)
    chunk_decay = jnp.exp(float(BLOCK) * lg)
    ng, nb = ng_ref[...], nb_ref[...]

    def chunk(start, state):
        rows = pl.ds(start, BLOCK)
        qb, kb, vb = q_ref[rows, :], k_ref[rows, :], v_ref[rows, :]
        s = _dot_nt(qb, kb) * decay
        inner = _dot(s.astype(bf16), vb)
        cross = _dot(qb, state.astype(bf16)) * xi
        kz = (kb.astype(f32) * zeta).astype(bf16)
        new_state = state * chunk_decay + _dot_tn(kz, vb)
        o = inner + cross
        mu = jnp.mean(o, axis=-1, keepdims=True)
        d = o - mu
        var = jnp.mean(d * d, axis=-1, keepdims=True)
        on = d * lax.rsqrt(var + EPS) * ng + nb
        g = g_ref[rows, :].astype(f32)
        o_ref[rows, :] = (g * _sigmoid(g) * on).astype(bf16)
        return new_state

    state = chunk(SEQ, jnp.zeros((RET_HEAD, RET_HEAD), f32))

    def body(c, st):
        return chunk(pl.multiple_of(c * BLOCK, BLOCK), st)

    lax.fori_loop(0, SEQ // BLOCK, body, state)


def _ret_call(rq, rk, rv, rg, log_g, norm_g, norm_b):
    B = rq.shape[0]
    spec = pl.BlockSpec((None, LP, RET_HEAD), lambda b, h: (b, 0, h))
    vec = pl.BlockSpec((1, RET_HEAD), lambda b, h: (0, h))
    return pl.pallas_call(
        _ret_kernel, grid=(B, RET_HEADS),
        in_specs=[spec, spec, spec, spec,
                  pl.BlockSpec((None, 8, LANES), lambda b, h: (h, 0, 0)), vec, vec],
        out_specs=spec,
        out_shape=jax.ShapeDtypeStruct((B, LP, RET_WIDTH), bf16),
        name="ret",
        compiler_params=pltpu.CompilerParams(
            dimension_semantics=("parallel", "parallel"), vmem_limit_bytes=VMEM_LIMIT),
    )(rq, rk, rv, rg, log_g, norm_g, norm_b)


def _post_kernel(h_ref, ym_ref, yr_ref, wo_ref, fg_ref, wgu_ref, wd_ref, o_ref, act_ref):
    h1 = (h_ref[...] + _dot(ym_ref[...], wo_ref[0:MLA_WIDTH, :])
          + _dot(yr_ref[...], wo_ref[MLA_WIDTH:MLA_WIDTH + RET_WIDTH, :]))
    hf = (h1 * _rms(h1, 1.0 / D_MODEL) * fg_ref[...]).astype(bf16)
    for c in range(D_FF // FF_TILE):
        lo = c * FF_TILE
        gate = _dot(hf, wgu_ref[:, lo:lo + FF_TILE])
        up = _dot(hf, wgu_ref[:, D_FF + lo:D_FF + lo + FF_TILE])
        act_ref[:, lo:lo + FF_TILE] = (gate * _sigmoid(gate) * up).astype(bf16)
    o_ref[...] = h1 + _dot(act_ref[...], wd_ref[...])


def _post_call(h, ym, yr, p, last):
    B = h.shape[0]
    tm = POST_TM_LAST if last else PRE_TM
    rows_out = SEQ if last else LP
    row = lambda w: pl.BlockSpec((None, tm, w), lambda b, j: (b, j, 0))
    return pl.pallas_call(
        _post_kernel, grid=(B, rows_out // tm),
        in_specs=[row(D_MODEL), row(MLA_WIDTH), row(RET_WIDTH),
                  _const_spec((MLA_WIDTH + RET_WIDTH, D_MODEL)),
                  _const_spec((1, D_MODEL)),
                  _const_spec((D_MODEL, 2 * D_FF)),
                  _const_spec((D_FF, D_MODEL))],
        out_specs=row(D_MODEL),
        out_shape=jax.ShapeDtypeStruct((B, rows_out, D_MODEL), f32),
        scratch_shapes=[pltpu.VMEM((tm, D_FF), bf16)],
        name="post_last" if last else "post",
        compiler_params=pltpu.CompilerParams(
            dimension_semantics=("parallel", "parallel"), vmem_limit_bytes=VMEM_LIMIT),
    )(h, ym, yr, p["w_out"], p["ffn_g"], p["w_gu"], p["w_down"])


def _positions():
    r = jnp.arange(LP)
    return jnp.where(r < SEQ, r + N_META, jnp.maximum(r - (SEQ + PAD_ROWS), 0)).astype(f32)


def _rope_tables():
    pos = _positions()
    valid = ((jnp.arange(LP) < SEQ) | (jnp.arange(LP) >= SEQ + PAD_ROWS)).astype(f32)[:, None]

    def cs(dim):
        inv = ROPE_BASE ** (-jnp.arange(0, dim, 2, dtype=f32) / dim)
        ang = pos[:, None] * inv[None, :]
        return jnp.cos(ang), jnp.sin(ang)

    cm, sm = cs(QK_ROPE)
    z32 = jnp.zeros_like(cm)
    cm_t = jnp.concatenate([cm, cm, z32, z32], axis=1)
    s1_t = jnp.concatenate([-sm, z32, z32, z32], axis=1)
    s2_t = jnp.concatenate([z32, sm, z32, z32], axis=1)
    cr, sr = cs(RET_HEAD)
    cr_t = jnp.concatenate([cr, cr], axis=1)
    sr_t = jnp.concatenate([-sr, sr], axis=1)
    kscale = valid * (RET_HEAD ** -0.5)
    return cm_t, s1_t, s2_t, cr_t, sr_t, cr_t * kscale, sr_t * kscale


def _layer_params(l, attn_norm_g, w_in, q_a_norm_g, w_q_b, kv_a_norm_g, w_kv_b, q_norm_g, k_norm_g,
                  mla_out_norm_g, ret_norm_g, ret_norm_b, w_out, ffn_norm_g, w_gate_up, w_down):
    wi = w_in[l]
    w_in_p = jnp.concatenate(
        [wi[:, :C_KPE + QK_ROPE], jnp.zeros((D_MODEL, LANES - QK_ROPE), f32), wi[:, C_KPE + QK_ROPE:]],
        axis=1).astype(bf16)
    wq = w_q_b[l].reshape(Q_LORA, MLA_HEADS, QK_HEAD)
    wq = jnp.pad(wq, ((0, 0), (0, 0), (0, QK_PAD - QK_HEAD))).reshape(Q_LORA, MLA_HEADS * QK_PAD)
    scale = QK_HEAD ** -0.5
    return {
        "attn_g": attn_norm_g[l][None, :],
        "w_in": w_in_p,
        "qa_g": q_a_norm_g[l][None, :],
        "w_qb": wq.astype(bf16),
        "kva_g": kv_a_norm_g[l][None, :],
        "w_kvb": w_kv_b[l].astype(bf16),
        "q_g": jnp.pad(q_norm_g[l] * scale, (0, QK_PAD - QK_HEAD))[None, :],
        "k_gn": k_norm_g[l][None, :QK_NOPE],
        "k_gp": jnp.pad(k_norm_g[l][QK_NOPE:], (0, LANES - QK_ROPE))[None, :],
        "out_g": mla_out_norm_g[l][None, :],
        "ret_g": ret_norm_g[l][None, :],
        "ret_b": ret_norm_b[l][None, :],
        "w_out": w_out[l].astype(bf16),
        "ffn_g": ffn_norm_g[l][None, :],
        "w_gu": w_gate_up[l].astype(bf16),
        "w_down": w_down[l].astype(bf16),
    }


def kernel(x, meta_tokens, attn_norm_g, w_in, q_a_norm_g, w_q_b, kv_a_norm_g, w_kv_b, q_norm_g,
           k_norm_g, mla_out_norm_g, ret_norm_g, ret_norm_b, w_out, ffn_norm_g, w_gate_up, w_down):
    B = x.shape[0]
    depth = w_in.shape[0]
    meta = jnp.broadcast_to(meta_tokens[None].astype(x.dtype), (B, N_META, D_MODEL))
    h = jnp.concatenate([x, jnp.zeros((B, PAD_ROWS, D_MODEL), x.dtype), meta], axis=1)
    tabs = _rope_tables()
    gamma = 1.0 - 2.0 ** (-5.0 - jnp.arange(RET_HEADS, dtype=f32))
    log_g = jnp.broadcast_to(jnp.log(gamma)[:, None, None], (RET_HEADS, 8, LANES))
    for l in range(depth):
        p = _layer_params(l, attn_norm_g, w_in, q_a_norm_g, w_q_b, kv_a_norm_g, w_kv_b, q_norm_g,
                          k_norm_g, mla_out_norm_g, ret_norm_g, ret_norm_b, w_out, ffn_norm_g,
                          w_gate_up, w_down)
        q, k, v, rq, rk, rv, rg = _pre_call(h, p, tabs)
        y_mla = _attn_call(q, k, v, p["out_g"])
        y_ret = _ret_call(rq, rk, rv, rg, log_g, p["ret_g"], p["ret_b"])
        h = _post_call(h, y_mla, y_ret, p, last=(l == depth - 1))
    return h
```

```python
import functools

import jax
import jax.numpy as jnp
from jax import lax
from jax.experimental import pallas as pl
from jax.experimental.pallas import tpu as pltpu

D_MODEL = 1024
SEQ = 2048
N_META = 16
BLOCK = 128
MLA_HEADS = 4
Q_LORA = 256
KV_LORA = 256
QK_NOPE = 128
QK_ROPE = 64
QK_HEAD = QK_NOPE + QK_ROPE
V_HEAD = 128
MLA_WIDTH = MLA_HEADS * V_HEAD
RET_HEADS = 4
RET_HEAD = 128
RET_WIDTH = RET_HEADS * RET_HEAD
D_FF = 2816
ROPE_BASE = 10000.0
EPS = 1e-6
NEG_INF = -1e30

LANES = 128
PAD_ROWS = BLOCK - N_META
LP = SEQ + BLOCK
QK_PAD = 2 * LANES

C_CQ, C_CKV, C_KPE = 0, Q_LORA, Q_LORA + KV_LORA
C_RQ = C_KPE + LANES
C_RK = C_RQ + RET_WIDTH
C_RV = C_RK + RET_WIDTH
C_RG = C_RV + RET_WIDTH
N_IN_PAD = C_RG + RET_WIDTH

PRE_TM = 544
POST_TM_LAST = 512
ATT_TQ = 512
FF_TILE = 256
LOG2_E = 1.4426950408889634
VMEM_LIMIT = 56 * 1024 * 1024

f32 = jnp.float32
bf16 = jnp.bfloat16


def _dot(a, b):
    return jnp.dot(a, b, preferred_element_type=f32)


def _dot_nt(a, b):
    return lax.dot_general(a, b, (((1,), (1,)), ((), ())), preferred_element_type=f32)


def _dot_tn(a, b):
    return lax.dot_general(a, b, (((0,), (0,)), ((), ())), preferred_element_type=f32)


def _rms(x, inv_n):
    return lax.rsqrt(jnp.sum(x * x, axis=-1, keepdims=True) * inv_n + EPS)


def _sigmoid(x):
    return 1.0 / (1.0 + jnp.exp(-x))


def _pre_kernel(h_ref, g_ref, win_ref, qag_ref, wqb_ref, kvag_ref, wkvb_ref, qg_ref, kgn_ref,
                kgp_ref, cm_ref, s1m_ref, s2m_ref, cr_ref, sr_ref, crk_ref, srk_ref,
                q_out, k_out, v_out, rq_out, rk_out, rv_out, rg_out):
    x = h_ref[...]
    hb = (x * _rms(x, 1.0 / D_MODEL) * g_ref[...]).astype(bf16)

    def proj(lo, hi):
        return _dot(hb, win_ref[:, lo:hi])

    cm, s1m, s2m = cm_ref[...], s1m_ref[...], s2m_ref[...]

    def rope_mla(v):
        return v * cm + pltpu.roll(v, 96, 1) * s1m + pltpu.roll(v, 32, 1) * s2m

    cq = proj(C_CQ, C_CQ + Q_LORA)
    cqn = (cq * _rms(cq, 1.0 / Q_LORA) * qag_ref[...]).astype(bf16)
    q = _dot(cqn, wqb_ref[...])
    qg = qg_ref[...]
    for h in range(MLA_HEADS):
        qh = q[:, h * QK_PAD:(h + 1) * QK_PAD]
        qn = qh * _rms(qh, 1.0 / QK_HEAD) * qg
        q_out[:, h * QK_PAD:h * QK_PAD + LANES] = qn[:, :LANES].astype(bf16)
        q_out[:, h * QK_PAD + LANES:(h + 1) * QK_PAD] = rope_mla(qn[:, LANES:]).astype(bf16)

    ckv = proj(C_CKV, C_CKV + KV_LORA)
    ckvn = (ckv * _rms(ckv, 1.0 / KV_LORA) * kvag_ref[...]).astype(bf16)
    kv = _dot(ckvn, wkvb_ref[...])
    kpe = proj(C_KPE, C_KPE + LANES)
    ss_pe = jnp.sum(kpe * kpe, axis=-1, keepdims=True)
    kpe_rot = rope_mla(kpe * kgp_ref[...])
    kgn = kgn_ref[...]
    for h in range(MLA_HEADS):
        kn = kv[:, h * QK_PAD:h * QK_PAD + LANES]
        r = lax.rsqrt((jnp.sum(kn * kn, axis=-1, keepdims=True) + ss_pe) * (1.0 / QK_HEAD) + EPS)
        k_out[:, h * QK_PAD:h * QK_PAD + LANES] = (kn * r * kgn).astype(bf16)
        k_out[:, h * QK_PAD + LANES:(h + 1) * QK_PAD] = (kpe_rot * r).astype(bf16)
        v_out[:, h * V_HEAD:(h + 1) * V_HEAD] = kv[:, h * QK_PAD + LANES:(h + 1) * QK_PAD].astype(bf16)

    cr, sr, crk, srk = cr_ref[...], sr_ref[...], crk_ref[...], srk_ref[...]
    rq = proj(C_RQ, C_RQ + RET_WIDTH)
    rk = proj(C_RK, C_RK + RET_WIDTH)
    for h in range(RET_HEADS):
        sl = slice(h * RET_HEAD, (h + 1) * RET_HEAD)
        xq = rq[:, sl]
        rq_out[:, sl] = (xq * cr + pltpu.roll(xq, RET_HEAD // 2, 1) * sr).astype(bf16)
        xk = rk[:, sl]
        rk_out[:, sl] = (xk * crk + pltpu.roll(xk, RET_HEAD // 2, 1) * srk).astype(bf16)
    rv_out[...] = proj(C_RV, C_RV + RET_WIDTH).astype(bf16)
    rg_out[...] = proj(C_RG, C_RG + RET_WIDTH).astype(bf16)


def _const_spec(shape):
    nd = len(shape)
    return pl.BlockSpec(shape, lambda *_: (0,) * nd, pipeline_mode=pl.Buffered(1))


def _pre_call(h, p, tabs):
    B = h.shape[0]
    tm = PRE_TM
    grid = (B, LP // tm)
    row = lambda w: pl.BlockSpec((None, tm, w), lambda b, j: (b, j, 0))
    tab = pl.BlockSpec((tm, LANES), lambda b, j: (j, 0))
    in_specs = [
        row(D_MODEL),
        _const_spec((1, D_MODEL)),
        _const_spec((D_MODEL, N_IN_PAD)),
        _const_spec((1, Q_LORA)),
        _const_spec((Q_LORA, MLA_HEADS * QK_PAD)),
        _const_spec((1, KV_LORA)),
        _const_spec((KV_LORA, MLA_HEADS * QK_PAD)),
        _const_spec((1, QK_PAD)),
        _const_spec((1, LANES)),
        _const_spec((1, LANES)),
    ] + [tab] * 7
    widths = (MLA_HEADS * QK_PAD, MLA_HEADS * QK_PAD, MLA_WIDTH, RET_WIDTH, RET_WIDTH, RET_WIDTH, RET_WIDTH)
    out_shape = [jax.ShapeDtypeStruct((B, LP, w), bf16) for w in widths]
    out_specs = [row(w) for w in widths]
    return pl.pallas_call(
        _pre_kernel, grid=grid, in_specs=in_specs, out_specs=out_specs, out_shape=out_shape,
        name="pre",
        compiler_params=pltpu.CompilerParams(
            dimension_semantics=("parallel", "parallel"), vmem_limit_bytes=VMEM_LIMIT),
    )(h, p["attn_g"], p["w_in"], p["qa_g"], p["w_qb"], p["kva_g"], p["w_kvb"], p["q_g"],
      p["k_gn"], p["k_gp"], *tabs)


def _attn_kernel(q_ref, k_ref, v_ref, og_ref, o_ref):
    og = og_ref[...]
    k_meta = k_ref[SEQ:LP, :]
    v_meta = v_ref[SEQ:LP, :]
    meta_valid = lax.broadcasted_iota(jnp.int32, (1, BLOCK), 1) >= PAD_ROWS

    def rowmax(s):
        return jnp.max(s, axis=-1, keepdims=True)

    def rowsum(p):
        return jnp.sum(p, axis=-1, keepdims=True)

    def finish(acc, l):
        o = acc * (1.0 / l)
        return (o * _rms(o, 1.0 / V_HEAD) * og).astype(bf16)

    tri = (lax.broadcasted_iota(jnp.int32, (ATT_TQ, ATT_TQ), 0)
           >= lax.broadcasted_iota(jnp.int32, (ATT_TQ, ATT_TQ), 1))

    for i in range(SEQ // ATT_TQ):
        lo, hi = i * ATT_TQ, (i + 1) * ATT_TQ
        q = q_ref[lo:hi, :]
        s_m = jnp.where(meta_valid, _dot_nt(q, k_meta), NEG_INF)
        s_d = jnp.where(tri, _dot_nt(q, k_ref[lo:hi, :]), NEG_INF)
        m = jnp.maximum(rowmax(s_m), rowmax(s_d))
        if i > 0:
            s_p = _dot_nt(q, k_ref[0:lo, :])
            m = jnp.maximum(m, rowmax(s_p))
        p_m = jnp.exp2(s_m - m)
        p_d = jnp.exp2(s_d - m)
        l = rowsum(p_m) + rowsum(p_d)
        acc = _dot(p_m.astype(bf16), v_meta) + _dot(p_d.astype(bf16), v_ref[lo:hi, :])
        if i > 0:
            p_p = jnp.exp2(s_p - m)
            l = l + rowsum(p_p)
            acc = acc + _dot(p_p.astype(bf16), v_ref[0:lo, :])
        o_ref[lo:hi, :] = finish(acc, l)

    q = q_ref[SEQ:LP, :]
    mmask = (lax.broadcasted_iota(jnp.int32, (BLOCK, BLOCK), 0)
             >= lax.broadcasted_iota(jnp.int32, (BLOCK, BLOCK), 1)) & meta_valid
    s = jnp.where(mmask, _dot_nt(q, k_meta), NEG_INF)
    p = jnp.exp2(s - rowmax(s))
    o_ref[SEQ:LP, :] = finish(_dot(p.astype(bf16), v_meta), rowsum(p))


def _attn_call(q, k, v, out_g):
    B = q.shape[0]
    qk_spec = pl.BlockSpec((None, LP, QK_PAD), lambda b, h: (b, 0, h))
    v_spec = pl.BlockSpec((None, LP, V_HEAD), lambda b, h: (b, 0, h))
    return pl.pallas_call(
        _attn_kernel, grid=(B, MLA_HEADS),
        in_specs=[qk_spec, qk_spec, v_spec, pl.BlockSpec((1, V_HEAD), lambda b, h: (0, h))],
        out_specs=v_spec,
        out_shape=jax.ShapeDtypeStruct((B, LP, MLA_WIDTH), bf16),
        name="attn",
        compiler_params=pltpu.CompilerParams(
            dimension_semantics=("parallel", "parallel"), vmem_limit_bytes=VMEM_LIMIT),
    )(q, k, v, out_g)


def _ret_kernel(q_ref, k_ref, v_ref, g_ref, lg_ref, ng_ref, nb_ref, o_ref):
    lg = lg_ref[0:1, :]
    ri = lax.broadcasted_iota(jnp.int32, (BLOCK, BLOCK), 0)
    ci = lax.broadcasted_iota(jnp.int32, (BLOCK, BLOCK), 1)
    diff = (ri - ci).astype(f32)
    decay = jnp.where(diff >= 0, jnp.exp(jnp.maximum(diff, 0.0) * lg), 0.0)
    idx = ri.astype(f32)
    xi = jnp.exp((idx + 1.0) * lg)
    zeta = jnp.exp((BLOCK - 1.0 - idx) * lg)
    chunk_decay = jnp.exp(float(BLOCK) * lg)
    ng, nb = ng_ref[...], nb_ref[...]

    def chunk(start, state):
        rows = pl.ds(start, BLOCK)
        qb, kb, vb = q_ref[rows, :], k_ref[rows, :], v_ref[rows, :]
        s = _dot_nt(qb, kb) * decay
        inner = _dot(s.astype(bf16), vb)
        cross = _dot(qb, state.astype(bf16)) * xi
        kz = (kb.astype(f32) * zeta).astype(bf16)
        new_state = state * chunk_decay + _dot_tn(kz, vb)
        o = inner + cross
        mu = jnp.mean(o, axis=-1, keepdims=True)
        d = o - mu
        var = jnp.mean(d * d, axis=-1, keepdims=True)
        on = d * lax.rsqrt(var + EPS) * ng + nb
        g = g_ref[rows, :].astype(f32)
        o_ref[rows, :] = (g * _sigmoid(g) * on).astype(bf16)
        return new_state

    state = chunk(SEQ, jnp.zeros((RET_HEAD, RET_HEAD), f32))

    def body(c, st):
        return chunk(pl.multiple_of(c * BLOCK, BLOCK), st)

    lax.fori_loop(0, SEQ // BLOCK, body, state)


def _ret_call(rq, rk, rv, rg, log_g, norm_g, norm_b):
    B = rq.shape[0]
    spec = pl.BlockSpec((None, LP, RET_HEAD), lambda b, h: (b, 0, h))
    vec = pl.BlockSpec((1, RET_HEAD), lambda b, h: (0, h))
    return pl.pallas_call(
        _ret_kernel, grid=(B, RET_HEADS),
        in_specs=[spec, spec, spec, spec,
                  pl.BlockSpec((None, 8, LANES), lambda b, h: (h, 0, 0)), vec, vec],
        out_specs=spec,
        out_shape=jax.ShapeDtypeStruct((B, LP, RET_WIDTH), bf16),
        name="ret",
        compiler_params=pltpu.CompilerParams(
            dimension_semantics=("parallel", "parallel"), vmem_limit_bytes=VMEM_LIMIT),
    )(rq, rk, rv, rg, log_g, norm_g, norm_b)


def _post_kernel(h_ref, ym_ref, yr_ref, wo_ref, fg_ref, wgu_ref, wd_ref, o_ref, act_ref):
    h1 = (h_ref[...] + _dot(ym_ref[...], wo_ref[0:MLA_WIDTH, :])
          + _dot(yr_ref[...], wo_ref[MLA_WIDTH:MLA_WIDTH + RET_WIDTH, :]))
    hf = (h1 * _rms(h1, 1.0 / D_MODEL) * fg_ref[...]).astype(bf16)
    for c in range(D_FF // FF_TILE):
        lo = c * FF_TILE
        gate = _dot(hf, wgu_ref[:, lo:lo + FF_TILE])
        up = _dot(hf, wgu_ref[:, D_FF + lo:D_FF + lo + FF_TILE])
        act_ref[:, lo:lo + FF_TILE] = (gate * _sigmoid(gate) * up).astype(bf16)
    o_ref[...] = h1 + _dot(act_ref[...], wd_ref[...])


def _post_call(h, ym, yr, p, last):
    B = h.shape[0]
    tm = POST_TM_LAST if last else PRE_TM
    rows_out = SEQ if last else LP
    row = lambda w: pl.BlockSpec((None, tm, w), lambda b, j: (b, j, 0))
    return pl.pallas_call(
        _post_kernel, grid=(B, rows_out // tm),
        in_specs=[row(D_MODEL), row(MLA_WIDTH), row(RET_WIDTH),
                  _const_spec((MLA_WIDTH + RET_WIDTH, D_MODEL)),
                  _const_spec((1, D_MODEL)),
                  _const_spec((D_MODEL, 2 * D_FF)),
                  _const_spec((D_FF, D_MODEL))],
        out_specs=row(D_MODEL),
        out_shape=jax.ShapeDtypeStruct((B, rows_out, D_MODEL), f32),
        scratch_shapes=[pltpu.VMEM((tm, D_FF), bf16)],
        name="post_last" if last else "post",
        compiler_params=pltpu.CompilerParams(
            dimension_semantics=("parallel", "parallel"), vmem_limit_bytes=VMEM_LIMIT),
    )(h, ym, yr, p["w_out"], p["ffn_g"], p["w_gu"], p["w_down"])


def _positions():
    r = jnp.arange(LP)
    return jnp.where(r < SEQ, r + N_META, jnp.maximum(r - (SEQ + PAD_ROWS), 0)).astype(f32)


def _rope_tables():
    pos = _positions()
    valid = ((jnp.arange(LP) < SEQ) | (jnp.arange(LP) >= SEQ + PAD_ROWS)).astype(f32)[:, None]

    def cs(dim):
        inv = ROPE_BASE ** (-jnp.arange(0, dim, 2, dtype=f32) / dim)
        ang = pos[:, None] * inv[None, :]
        return jnp.cos(ang), jnp.sin(ang)

    cm, sm = cs(QK_ROPE)
    z32 = jnp.zeros_like(cm)
    cm_t = jnp.concatenate([cm, cm, z32, z32], axis=1)
    s1_t = jnp.concatenate([-sm, z32, z32, z32], axis=1)
    s2_t = jnp.concatenate([z32, sm, z32, z32], axis=1)
    cr, sr = cs(RET_HEAD)
    cr_t = jnp.concatenate([cr, cr], axis=1)
    sr_t = jnp.concatenate([-sr, sr], axis=1)
    kscale = valid * (RET_HEAD ** -0.5)
    return cm_t, s1_t, s2_t, cr_t, sr_t, cr_t * kscale, sr_t * kscale


def _layer_params(l, attn_norm_g, w_in, q_a_norm_g, w_q_b, kv_a_norm_g, w_kv_b, q_norm_g, k_norm_g,
                  mla_out_norm_g, ret_norm_g, ret_norm_b, w_out, ffn_norm_g, w_gate_up, w_down):
    wi = w_in[l]
    w_in_p = jnp.concatenate(
        [wi[:, :C_KPE + QK_ROPE], jnp.zeros((D_MODEL, LANES - QK_ROPE), f32), wi[:, C_KPE + QK_ROPE:]],
        axis=1).astype(bf16)
    wq = w_q_b[l].reshape(Q_LORA, MLA_HEADS, QK_HEAD)
    wq = jnp.pad(wq, ((0, 0), (0, 0), (0, QK_PAD - QK_HEAD))).reshape(Q_LORA, MLA_HEADS * QK_PAD)
    scale = QK_HEAD ** -0.5 * LOG2_E
    return {
        "attn_g": attn_norm_g[l][None, :],
        "w_in": w_in_p,
        "qa_g": q_a_norm_g[l][None, :],
        "w_qb": wq.astype(bf16),
        "kva_g": kv_a_norm_g[l][None, :],
        "w_kvb": w_kv_b[l].astype(bf16),
        "q_g": jnp.pad(q_norm_g[l] * scale, (0, QK_PAD - QK_HEAD))[None, :],
        "k_gn": k_norm_g[l][None, :QK_NOPE],
        "k_gp": jnp.pad(k_norm_g[l][QK_NOPE:], (0, LANES - QK_ROPE))[None, :],
        "out_g": mla_out_norm_g[l][None, :],
        "ret_g": ret_norm_g[l][None, :],
        "ret_b": ret_norm_b[l][None, :],
        "w_out": w_out[l].astype(bf16),
        "ffn_g": ffn_norm_g[l][None, :],
        "w_gu": w_gate_up[l].astype(bf16),
        "w_down": w_down[l].astype(bf16),
    }


def kernel(x, meta_tokens, attn_norm_g, w_in, q_a_norm_g, w_q_b, kv_a_norm_g, w_kv_b, q_norm_g,
           k_norm_g, mla_out_norm_g, ret_norm_g, ret_norm_b, w_out, ffn_norm_g, w_gate_up, w_down):
    B = x.shape[0]
    depth = w_in.shape[0]
    meta = jnp.broadcast_to(meta_tokens[None].astype(x.dtype), (B, N_META, D_MODEL))
    h = jnp.concatenate([x, jnp.zeros((B, PAD_ROWS, D_MODEL), x.dtype), meta], axis=1)
    tabs = _rope_tables()
    gamma = 1.0 - 2.0 ** (-5.0 - jnp.arange(RET_HEADS, dtype=f32))
    log_g = jnp.broadcast_to(jnp.log(gamma)[:, None, None], (RET_HEADS, 8, LANES))
    for l in range(depth):
        p = _layer_params(l, attn_norm_g, w_in, q_a_norm_g, w_q_b, kv_a_norm_g, w_kv_b, q_norm_g,
                          k_norm_g, mla_out_norm_g, ret_norm_g, ret_norm_b, w_out, ffn_norm_g,
                          w_gate_up, w_down)
        q, k, v, rq, rk, rv, rg = _pre_call(h, p, tabs)
        y_mla = _attn_call(q, k, v, p["out_g"])
        y_ret = _ret_call(rq, rk, rv, rg, log_g, p["ret_g"], p["ret_b"])
        h = _post_call(h, y_mla, y_ret, p, last=(l == depth - 1))
    return h
```

```python
import jax
import jax.numpy as jnp
from jax import lax
from jax.experimental import pallas as pl
from jax.experimental.pallas import tpu as pltpu

D_MODEL = 1024
SEQ = 2048
N_META = 16
BLOCK = 128
MLA_HEADS = 4
Q_LORA = 256
KV_LORA = 256
QK_NOPE = 128
QK_ROPE = 64
QK_HEAD = QK_NOPE + QK_ROPE
V_HEAD = 128
MLA_WIDTH = MLA_HEADS * V_HEAD
RET_HEADS = 4
RET_HEAD = 128
RET_WIDTH = RET_HEADS * RET_HEAD
D_FF = 2816
ROPE_BASE = 10000.0
EPS = 1e-6
NEG_INF = -1e30

LANES = 128
HALF = LANES // 2
PAD_ROWS = BLOCK - N_META
LP = SEQ + BLOCK
QK_PAD = 2 * LANES
KV_PAD = 3 * LANES
RET_PAIR = 2 * RET_HEAD

C_CQ = 0
C_CKV = C_CQ + Q_LORA
C_KPE = C_CKV + KV_LORA
C_RQ = C_KPE + 2 * LANES
C_RK = C_RQ + RET_WIDTH
C_RV = C_RK + RET_WIDTH
C_RG = C_RV + RET_WIDTH
N_IN_PAD = C_RG + RET_WIDTH

PRE_TM = 544
POST_TM_LAST = 512
ATT_TQ = 512
RET_CHUNK = 256
FF_TILE = 256
LOG2_E = 1.4426950408889634
VMEM_LIMIT = 56 * 1024 * 1024

f32 = jnp.float32
bf16 = jnp.bfloat16


def _dot(a, b):
    return jnp.dot(a, b, preferred_element_type=f32)


def _dot_nt(a, b):
    return lax.dot_general(a, b, (((1,), (1,)), ((), ())), preferred_element_type=f32)


def _dot_tn(a, b):
    return lax.dot_general(a, b, (((0,), (0,)), ((), ())), preferred_element_type=f32)


def _rms(x, inv_n):
    return lax.rsqrt(jnp.sum(x * x, axis=-1, keepdims=True) * inv_n + EPS)


def _sigmoid(x):
    return 1.0 / (1.0 + jnp.exp(-x))


def _rot(a, b, c, s):
    return a * c - b * s, b * c + a * s


def _pre_kernel(h_ref, g_ref, win_ref, qag_ref, wqb_ref, kvag_ref, wkvb_ref, qg_ref, kg_ref,
                ca_ref, sa_ref, c2_ref, s2_ref, ck_ref, sk_ref,
                q_out, k_out, v_out, rq_out, rk_out, rv_out, rg_out, z_ref, q_ref, kv_ref):
    x = h_ref[...]
    hb = (x * _rms(x, 1.0 / D_MODEL) * g_ref[...]).astype(bf16)
    z_ref[...] = _dot(hb, win_ref[...])

    cq = z_ref[:, C_CQ:C_CQ + Q_LORA]
    cqn = (cq * _rms(cq, 1.0 / Q_LORA) * qag_ref[...]).astype(bf16)
    q_ref[...] = _dot(cqn, wqb_ref[...])
    ckv = z_ref[:, C_CKV:C_CKV + KV_LORA]
    ckvn = (ckv * _rms(ckv, 1.0 / KV_LORA) * kvag_ref[...]).astype(bf16)
    kv_ref[...] = _dot(ckvn, wkvb_ref[...])

    ca, sa = ca_ref[...], sa_ref[...]

    def norm_rope(a, b, ga, gb):
        r = lax.rsqrt(jnp.sum(a * a + b * b, axis=-1, keepdims=True) * (1.0 / QK_HEAD) + EPS)
        ao, bo = _rot(a * ga, b * gb, ca, sa)
        return (ao * r).astype(bf16), (bo * r).astype(bf16)

    qga, qgb = qg_ref[:, :LANES], qg_ref[:, LANES:]
    kga, kgb = kg_ref[:, :LANES], kg_ref[:, LANES:]
    kpa = z_ref[:, C_KPE:C_KPE + LANES]
    kpb = z_ref[:, C_KPE + LANES:C_KPE + 2 * LANES]
    for h in range(MLA_HEADS):
        lo = h * QK_PAD
        q_out[:, lo:lo + LANES], q_out[:, lo + LANES:lo + QK_PAD] = norm_rope(
            q_ref[:, lo:lo + LANES], q_ref[:, lo + LANES:lo + QK_PAD], qga, qgb)
        kl = h * KV_PAD
        k_out[:, lo:lo + LANES], k_out[:, lo + LANES:lo + QK_PAD] = norm_rope(
            kv_ref[:, kl:kl + LANES] + kpa, kv_ref[:, kl + LANES:kl + 2 * LANES] + kpb, kga, kgb)
        v_out[:, h * V_HEAD:(h + 1) * V_HEAD] = kv_ref[:, kl + 2 * LANES:kl + KV_PAD].astype(bf16)

    c2, s2, ck, sk = c2_ref[...], s2_ref[...], ck_ref[...], sk_ref[...]
    first = lax.broadcasted_iota(jnp.int32, (1, LANES), 1) < HALF
    for p in range(RET_HEADS // 2):
        lo = p * RET_PAIR
        k1, k2 = _rot(z_ref[:, C_RK + lo:C_RK + lo + LANES],
                      z_ref[:, C_RK + lo + LANES:C_RK + lo + RET_PAIR], ck, sk)
        rk_out[:, lo:lo + LANES] = k1.astype(bf16)
        rk_out[:, lo + LANES:lo + RET_PAIR] = k2.astype(bf16)
        q1, q2 = _rot(z_ref[:, C_RQ + lo:C_RQ + lo + LANES],
                      z_ref[:, C_RQ + lo + LANES:C_RQ + lo + RET_PAIR], c2, s2)
        for e in range(2):
            keep = first if e == 0 else jnp.logical_not(first)
            qo = (2 * p + e) * RET_PAIR
            rq_out[:, qo:qo + LANES] = jnp.where(keep, q1, 0.0).astype(bf16)
            rq_out[:, qo + LANES:qo + RET_PAIR] = jnp.where(keep, q2, 0.0).astype(bf16)
    rv_out[...] = z_ref[:, C_RV:C_RV + RET_WIDTH].astype(bf16)
    rg_out[...] = z_ref[:, C_RG:C_RG + RET_WIDTH].astype(bf16)


def _const_spec(shape):
    nd = len(shape)
    return pl.BlockSpec(shape, lambda *_: (0,) * nd, pipeline_mode=pl.Buffered(1))


def _pre_call(h, p, tabs):
    B = h.shape[0]
    tm = PRE_TM
    grid = (B, LP // tm)
    row = lambda w: pl.BlockSpec((None, tm, w), lambda b, j: (b, j, 0))
    tab = pl.BlockSpec((tm, LANES), lambda b, j: (j, 0))
    in_specs = [
        row(D_MODEL),
        _const_spec((1, D_MODEL)),
        _const_spec((D_MODEL, N_IN_PAD)),
        _const_spec((1, Q_LORA)),
        _const_spec((Q_LORA, MLA_HEADS * QK_PAD)),
        _const_spec((1, KV_LORA)),
        _const_spec((KV_LORA, MLA_HEADS * KV_PAD)),
        _const_spec((1, QK_PAD)),
        _const_spec((1, QK_PAD)),
    ] + [tab] * 6
    widths = (MLA_HEADS * QK_PAD, MLA_HEADS * QK_PAD, MLA_WIDTH, RET_HEADS * RET_PAIR, RET_WIDTH,
              RET_WIDTH, RET_WIDTH)
    out_shape = [jax.ShapeDtypeStruct((B, LP, w), bf16) for w in widths]
    out_specs = [row(w) for w in widths]
    return pl.pallas_call(
        _pre_kernel, grid=grid, in_specs=in_specs, out_specs=out_specs, out_shape=out_shape,
        scratch_shapes=[pltpu.VMEM((tm, N_IN_PAD), f32), pltpu.VMEM((tm, MLA_HEADS * QK_PAD), f32),
                        pltpu.VMEM((tm, MLA_HEADS * KV_PAD), f32)],
        name="pre",
        compiler_params=pltpu.CompilerParams(
            dimension_semantics=("parallel", "parallel"), vmem_limit_bytes=VMEM_LIMIT),
    )(h, p["attn_g"], p["w_in"], p["qa_g"], p["w_qb"], p["kva_g"], p["w_kvb"], p["q_g"], p["k_g"],
      *tabs)


def _attn_kernel(q_ref, k_ref, v_ref, og_ref, o_ref):
    og = og_ref[...]
    k_meta = k_ref[SEQ:LP, :]
    v_meta = v_ref[SEQ:LP, :]
    meta_valid = lax.broadcasted_iota(jnp.int32, (1, BLOCK), 1) >= PAD_ROWS

    def rowmax(s):
        return jnp.max(s, axis=-1, keepdims=True)

    def rowsum(p):
        return jnp.sum(p, axis=-1, keepdims=True)

    def finish(acc, l):
        o = acc * (1.0 / l)
        return (o * _rms(o, 1.0 / V_HEAD) * og).astype(bf16)

    tri = (lax.broadcasted_iota(jnp.int32, (ATT_TQ, ATT_TQ), 0)
           >= lax.broadcasted_iota(jnp.int32, (ATT_TQ, ATT_TQ), 1))

    for i in range(SEQ // ATT_TQ):
        lo, hi = i * ATT_TQ, (i + 1) * ATT_TQ
        q = q_ref[lo:hi, :]
        s_m = jnp.where(meta_valid, _dot_nt(q, k_meta), NEG_INF)
        s_d = jnp.where(tri, _dot_nt(q, k_ref[lo:hi, :]), NEG_INF)
        m = jnp.maximum(rowmax(s_m), rowmax(s_d))
        if i > 0:
            s_p = _dot_nt(q, k_ref[0:lo, :])
            m = jnp.maximum(m, rowmax(s_p))
        p_m = jnp.exp2(s_m - m)
        p_d = jnp.exp2(s_d - m)
        l = rowsum(p_m) + rowsum(p_d)
        acc = _dot(p_m.astype(bf16), v_meta) + _dot(p_d.astype(bf16), v_ref[lo:hi, :])
        if i > 0:
            p_p = jnp.exp2(s_p - m)
            l = l + rowsum(p_p)
            acc = acc + _dot(p_p.astype(bf16), v_ref[0:lo, :])
        o_ref[lo:hi, :] = finish(acc, l)

    q = q_ref[SEQ:LP, :]
    mmask = (lax.broadcasted_iota(jnp.int32, (BLOCK, BLOCK), 0)
             >= lax.broadcasted_iota(jnp.int32, (BLOCK, BLOCK), 1)) & meta_valid
    s = jnp.where(mmask, _dot_nt(q, k_meta), NEG_INF)
    p = jnp.exp2(s - rowmax(s))
    o_ref[SEQ:LP, :] = finish(_dot(p.astype(bf16), v_meta), rowsum(p))


def _attn_call(q, k, v, out_g):
    B = q.shape[0]
    qk_spec = pl.BlockSpec((None, LP, QK_PAD), lambda b, h: (b, 0, h))
    v_spec = pl.BlockSpec((None, LP, V_HEAD), lambda b, h: (b, 0, h))
    return pl.pallas_call(
        _attn_kernel, grid=(B, MLA_HEADS),
        in_specs=[qk_spec, qk_spec, v_spec, pl.BlockSpec((1, V_HEAD), lambda b, h: (0, h))],
        out_specs=v_spec,
        out_shape=jax.ShapeDtypeStruct((B, LP, MLA_WIDTH), bf16),
        name="attn",
        compiler_params=pltpu.CompilerParams(
            dimension_semantics=("parallel", "parallel"), vmem_limit_bytes=VMEM_LIMIT),
    )(q, k, v, out_g)


def _ret_kernel(q_ref, k_ref, v_ref, g_ref, lg_ref, ng_ref, nb_ref, o_ref):
    C = RET_CHUNK
    lg = lg_ref[0:1, :]
    lg2 = jnp.concatenate([lg, lg], axis=1)
    ri = lax.broadcasted_iota(jnp.int32, (C, C), 0)
    ci = lax.broadcasted_iota(jnp.int32, (C, C), 1)
    diff = (ri - ci).astype(f32)
    decay = jnp.where(diff >= 0, jnp.exp(jnp.maximum(diff, 0.0) * lg2), 0.0)
    idx = lax.broadcasted_iota(jnp.int32, (C, RET_HEAD), 0).astype(f32)
    xi = jnp.exp((idx + 1.0) * lg)
    ng, nb = ng_ref[...], nb_ref[...]

    def key_state(rows, n):
        zeta = jnp.exp((n - 1.0 - idx[:n]) * lg)
        kz = (k_ref[rows, :].astype(f32) * jnp.concatenate([zeta, zeta], axis=1)).astype(bf16)
        return _dot_tn(v_ref[rows, :], kz)

    def emit(rows, n, state_t):
        qb = q_ref[rows, :]
        s = _dot_nt(qb, k_ref[rows, :]) * decay[:n, :n]
        o = _dot(s.astype(bf16), v_ref[rows, :])
        if state_t is not None:
            o = o + _dot_nt(qb, state_t) * xi[:n]
        mu = jnp.mean(o, axis=-1, keepdims=True)
        d = o - mu
        var = jnp.mean(d * d, axis=-1, keepdims=True)
        on = d * lax.rsqrt(var + EPS) * ng + nb
        g = g_ref[rows, :].astype(f32)
        o_ref[rows, :] = (g * _sigmoid(g) * on).astype(bf16)

    meta_rows = slice(SEQ, LP)
    emit(meta_rows, BLOCK, None)
    state_t = key_state(meta_rows, BLOCK)
    chunk_decay = jnp.exp(float(C) * lg2)
    for c in range(SEQ // C):
        rows = slice(c * C, (c + 1) * C)
        emit(rows, C, state_t.astype(bf16))
        if c + 1 < SEQ // C:
            state_t = state_t * chunk_decay + key_state(rows, C)


def _ret_call(rq, rk, rv, rg, log_g, norm_g, norm_b):
    B = rq.shape[0]
    q_spec = pl.BlockSpec((None, LP, RET_PAIR), lambda b, h: (b, 0, h))
    k_spec = pl.BlockSpec((None, LP, RET_PAIR), lambda b, h: (b, 0, h // 2))
    spec = pl.BlockSpec((None, LP, RET_HEAD), lambda b, h: (b, 0, h))
    vec = pl.BlockSpec((1, RET_HEAD), lambda b, h: (0, h))
    return pl.pallas_call(
        _ret_kernel, grid=(B, RET_HEADS),
        in_specs=[q_spec, k_spec, spec, spec,
                  pl.BlockSpec((None, 8, LANES), lambda b, h: (h, 0, 0)), vec, vec],
        out_specs=spec,
        out_shape=jax.ShapeDtypeStruct((B, LP, RET_WIDTH), bf16),
        name="ret",
        compiler_params=pltpu.CompilerParams(
            dimension_semantics=("parallel", "parallel"), vmem_limit_bytes=VMEM_LIMIT),
    )(rq, rk, rv, rg, log_g, norm_g, norm_b)


def _post_kernel(h_ref, ym_ref, yr_ref, wo_ref, fg_ref, wgu_ref, wd_ref, o_ref, act_ref):
    h1 = (h_ref[...] + _dot(ym_ref[...], wo_ref[0:MLA_WIDTH, :])
          + _dot(yr_ref[...], wo_ref[MLA_WIDTH:MLA_WIDTH + RET_WIDTH, :]))
    hf = (h1 * _rms(h1, 1.0 / D_MODEL) * fg_ref[...]).astype(bf16)
    for c in range(D_FF // FF_TILE):
        lo = c * FF_TILE
        gate = _dot(hf, wgu_ref[:, lo:lo + FF_TILE])
        up = _dot(hf, wgu_ref[:, D_FF + lo:D_FF + lo + FF_TILE])
        act_ref[:, lo:lo + FF_TILE] = (gate * _sigmoid(gate) * up).astype(bf16)
    o_ref[...] = h1 + _dot(act_ref[...], wd_ref[...])


def _post_call(h, ym, yr, p, last):
    B = h.shape[0]
    tm = POST_TM_LAST if last else PRE_TM
    rows_out = SEQ if last else LP
    row = lambda w: pl.BlockSpec((None, tm, w), lambda b, j: (b, j, 0))
    return pl.pallas_call(
        _post_kernel, grid=(B, rows_out // tm),
        in_specs=[row(D_MODEL), row(MLA_WIDTH), row(RET_WIDTH),
                  _const_spec((MLA_WIDTH + RET_WIDTH, D_MODEL)),
                  _const_spec((1, D_MODEL)),
                  _const_spec((D_MODEL, 2 * D_FF)),
                  _const_spec((D_FF, D_MODEL))],
        out_specs=row(D_MODEL),
        out_shape=jax.ShapeDtypeStruct((B, rows_out, D_MODEL), f32),
        scratch_shapes=[pltpu.VMEM((tm, D_FF), bf16)],
        name="post_last" if last else "post",
        compiler_params=pltpu.CompilerParams(
            dimension_semantics=("parallel", "parallel"), vmem_limit_bytes=VMEM_LIMIT),
    )(h, ym, yr, p["w_out"], p["ffn_g"], p["w_gu"], p["w_down"])


def _positions():
    r = jnp.arange(LP)
    return jnp.where(r < SEQ, r + N_META, jnp.maximum(r - (SEQ + PAD_ROWS), 0)).astype(f32)


def _rope_tables():
    pos = _positions()
    valid = ((jnp.arange(LP) < SEQ) | (jnp.arange(LP) >= SEQ + PAD_ROWS)).astype(f32)[:, None]

    def cs(dim):
        inv = ROPE_BASE ** (-jnp.arange(0, dim, 2, dtype=f32) / dim)
        ang = pos[:, None] * inv[None, :]
        return jnp.cos(ang), jnp.sin(ang)

    cm, sm = cs(QK_ROPE)
    z32 = jnp.zeros_like(cm)
    ca = jnp.concatenate([jnp.ones((LP, HALF), f32), cm, z32], axis=1)
    sa = jnp.concatenate([jnp.zeros((LP, HALF), f32), sm, z32], axis=1)
    cr, sr = cs(RET_HEAD)
    c2 = jnp.concatenate([cr, cr], axis=1)
    s2 = jnp.concatenate([sr, sr], axis=1)
    kscale = valid * (RET_HEAD ** -0.5)
    return ca, sa, c2, s2, c2 * kscale, s2 * kscale


def _mla_head_layout(a):
    q = QK_ROPE // 2
    z = jnp.zeros(a.shape[:-1] + (q,), a.dtype)
    return jnp.concatenate([a[..., :HALF], a[..., QK_NOPE:QK_NOPE + q], z,
                            a[..., HALF:QK_NOPE], a[..., QK_NOPE + q:], z], axis=-1)


def _kpe_layout(a):
    q = QK_ROPE // 2
    z64 = jnp.zeros(a.shape[:-1] + (HALF,), a.dtype)
    z32 = jnp.zeros(a.shape[:-1] + (q,), a.dtype)
    return jnp.concatenate([z64, a[..., :q], z32, z64, a[..., q:], z32], axis=-1)


def _pair_layout(a):
    lead = a.shape[:-1]
    a = a.reshape(lead + (RET_HEADS // 2, 2, 2, HALF))
    a = jnp.swapaxes(a, -3, -2)
    return a.reshape(lead + (RET_WIDTH,))


def _layer_params(l, attn_norm_g, w_in, q_a_norm_g, w_q_b, kv_a_norm_g, w_kv_b, q_norm_g, k_norm_g,
                  mla_out_norm_g, ret_norm_g, ret_norm_b, w_out, ffn_norm_g, w_gate_up, w_down):
    wi = w_in[l]
    o_kpe = Q_LORA + KV_LORA
    o_rq = o_kpe + QK_ROPE
    o_rk = o_rq + RET_WIDTH
    o_rv = o_rk + RET_WIDTH
    w_in_p = jnp.concatenate(
        [wi[:, :o_kpe], _kpe_layout(wi[:, o_kpe:o_rq]), _pair_layout(wi[:, o_rq:o_rk]),
         _pair_layout(wi[:, o_rk:o_rv]), wi[:, o_rv:]], axis=1).astype(bf16)
    wq = _mla_head_layout(w_q_b[l].reshape(Q_LORA, MLA_HEADS, QK_HEAD))
    wkv = w_kv_b[l].reshape(KV_LORA, MLA_HEADS, QK_NOPE + V_HEAD)
    z64 = jnp.zeros((KV_LORA, MLA_HEADS, HALF), f32)
    wkv = jnp.concatenate([wkv[..., :HALF], z64, wkv[..., HALF:QK_NOPE], z64, wkv[..., QK_NOPE:]],
                          axis=-1)
    scale = QK_HEAD ** -0.5 * LOG2_E
    return {
        "attn_g": attn_norm_g[l][None, :],
        "w_in": w_in_p,
        "qa_g": q_a_norm_g[l][None, :],
        "w_qb": wq.reshape(Q_LORA, MLA_HEADS * QK_PAD).astype(bf16),
        "kva_g": kv_a_norm_g[l][None, :],
        "w_kvb": wkv.reshape(KV_LORA, MLA_HEADS * KV_PAD).astype(bf16),
        "q_g": _mla_head_layout(q_norm_g[l] * scale)[None, :],
        "k_g": _mla_head_layout(k_norm_g[l])[None, :],
        "out_g": mla_out_norm_g[l][None, :],
        "ret_g": ret_norm_g[l][None, :],
        "ret_b": ret_norm_b[l][None, :],
        "w_out": w_out[l].astype(bf16),
        "ffn_g": ffn_norm_g[l][None, :],
        "w_gu": w_gate_up[l].astype(bf16),
        "w_down": w_down[l].astype(bf16),
    }


def kernel(x, meta_tokens, attn_norm_g, w_in, q_a_norm_g, w_q_b, kv_a_norm_g, w_kv_b, q_norm_g,
           k_norm_g, mla_out_norm_g, ret_norm_g, ret_norm_b, w_out, ffn_norm_g, w_gate_up, w_down):
    B = x.shape[0]
    depth = w_in.shape[0]
    meta = jnp.broadcast_to(meta_tokens[None].astype(x.dtype), (B, N_META, D_MODEL))
    h = jnp.concatenate([x, jnp.zeros((B, PAD_ROWS, D_MODEL), x.dtype), meta], axis=1)
    tabs = _rope_tables()
    gamma = 1.0 - 2.0 ** (-5.0 - jnp.arange(RET_HEADS, dtype=f32))
    log_g = jnp.broadcast_to(jnp.log(gamma)[:, None, None], (RET_HEADS, 8, LANES))
    for l in range(depth):
        p = _layer_params(l, attn_norm_g, w_in, q_a_norm_g, w_q_b, kv_a_norm_g, w_kv_b, q_norm_g,
                          k_norm_g, mla_out_norm_g, ret_norm_g, ret_norm_b, w_out, ffn_norm_g,
                          w_gate_up, w_down)
        q, k, v, rq, rk, rv, rg = _pre_call(h, p, tabs)
        y_mla = _attn_call(q, k, v, p["out_g"])
        y_ret = _ret_call(rq, rk, rv, rg, log_g, p["ret_g"], p["ret_b"])
        h = _post_call(h, y_mla, y_ret, p, last=(l == depth - 1))
    return h
```

```python
import numpy as np

import jax
import jax.numpy as jnp
from jax import lax
from jax.experimental import pallas as pl
from jax.experimental.pallas import tpu as pltpu

D_MODEL = 1024
SEQ = 2048
N_META = 16
BLOCK = 128
MLA_HEADS = 4
Q_LORA = 256
KV_LORA = 256
QK_NOPE = 128
QK_ROPE = 64
QK_HEAD = QK_NOPE + QK_ROPE
V_HEAD = 128
MLA_WIDTH = MLA_HEADS * V_HEAD
RET_HEADS = 4
RET_HEAD = 128
RET_WIDTH = RET_HEADS * RET_HEAD
D_FF = 2816
ROPE_BASE = 10000.0
EPS = 1e-6
NEG_INF = -1e30

LANES = 128
HALF = LANES // 2
PAD_ROWS = BLOCK - N_META
LP = SEQ + BLOCK
QK_PAD = 2 * LANES
KV_PAD = 3 * LANES
RET_PAIR = 2 * RET_HEAD

C_CQ = 0
C_CKV = C_CQ + Q_LORA
C_KPE = C_CKV + KV_LORA
C_RQ = C_KPE + 2 * LANES
C_RK = C_RQ + RET_WIDTH
C_RV = C_RK + RET_WIDTH
C_RG = C_RV + RET_WIDTH
N_IN_PAD = C_RG + RET_WIDTH

PRE_TM = 544
PRE_SUB = 272
POST_TM_LAST = 512
ATT_TQ = 512
ATT_ORDER = (0, 1, 2, 3)
RET_CHUNK = 256
FF_TILE = 256
LOG2_E = 1.4426950408889634
VMEM_LIMIT = 56 * 1024 * 1024

f32 = jnp.float32
bf16 = jnp.bfloat16


def _dot(a, b):
    return jnp.dot(a, b, preferred_element_type=f32)


def _dot_nt(a, b):
    return lax.dot_general(a, b, (((1,), (1,)), ((), ())), preferred_element_type=f32)


def _dot_tn(a, b):
    return lax.dot_general(a, b, (((0,), (0,)), ((), ())), preferred_element_type=f32)


def _rms(x, inv_n):
    return lax.rsqrt(jnp.sum(x * x, axis=-1, keepdims=True) * inv_n + EPS)


def _sigmoid(x):
    return 1.0 / (1.0 + jnp.exp(-x))


def _rot(a, b, c, s):
    return a * c - b * s, b * c + a * s


def _pre_kernel(h_ref, g_ref, win_ref, qag_ref, wqb_ref, kvag_ref, wkvb_ref, qg_ref, kg_ref,
                ca_ref, sa_ref, c2_ref, s2_ref, ck_ref, sk_ref,
                q_out, k_out, v_out, rq_out, rk_out, rv_out, rg_out, z_ref, q_ref, kv_ref):
    qga, qgb = qg_ref[:, :LANES], qg_ref[:, LANES:]
    kga, kgb = kg_ref[:, :LANES], kg_ref[:, LANES:]
    first = lax.broadcasted_iota(jnp.int32, (1, LANES), 1) < HALF

    def matmuls(rows):
        x = h_ref[rows, :]
        hb = (x * _rms(x, 1.0 / D_MODEL) * g_ref[...]).astype(bf16)
        z_ref[rows, :] = _dot(hb, win_ref[...])
        cq = z_ref[rows, C_CQ:C_CQ + Q_LORA]
        cqn = (cq * _rms(cq, 1.0 / Q_LORA) * qag_ref[...]).astype(bf16)
        q_ref[rows, :] = _dot(cqn, wqb_ref[...])
        ckv = z_ref[rows, C_CKV:C_CKV + KV_LORA]
        ckvn = (ckv * _rms(ckv, 1.0 / KV_LORA) * kvag_ref[...]).astype(bf16)
        kv_ref[rows, :] = _dot(ckvn, wkvb_ref[...])

    def rowwise(rows):
        ca, sa = ca_ref[rows, :], sa_ref[rows, :]

        def norm_rope(a, b, ga, gb):
            r = lax.rsqrt(jnp.sum(a * a + b * b, axis=-1, keepdims=True) * (1.0 / QK_HEAD) + EPS)
            ao, bo = _rot(a * ga, b * gb, ca, sa)
            return (ao * r).astype(bf16), (bo * r).astype(bf16)

        kpa = z_ref[rows, C_KPE:C_KPE + LANES]
        kpb = z_ref[rows, C_KPE + LANES:C_KPE + 2 * LANES]
        for h in range(MLA_HEADS):
            lo = h * QK_PAD
            q_out[rows, lo:lo + LANES], q_out[rows, lo + LANES:lo + QK_PAD] = norm_rope(
                q_ref[rows, lo:lo + LANES], q_ref[rows, lo + LANES:lo + QK_PAD], qga, qgb)
            kl = h * KV_PAD
            k_out[rows, lo:lo + LANES], k_out[rows, lo + LANES:lo + QK_PAD] = norm_rope(
                kv_ref[rows, kl:kl + LANES] + kpa, kv_ref[rows, kl + LANES:kl + 2 * LANES] + kpb,
                kga, kgb)
            v_out[rows, h * V_HEAD:(h + 1) * V_HEAD] = (
                kv_ref[rows, kl + 2 * LANES:kl + KV_PAD].astype(bf16))

        c2, s2, ck, sk = c2_ref[rows, :], s2_ref[rows, :], ck_ref[rows, :], sk_ref[rows, :]
        for p in range(RET_HEADS // 2):
            lo = p * RET_PAIR
            k1, k2 = _rot(z_ref[rows, C_RK + lo:C_RK + lo + LANES],
                          z_ref[rows, C_RK + lo + LANES:C_RK + lo + RET_PAIR], ck, sk)
            rk_out[rows, lo:lo + LANES] = k1.astype(bf16)
            rk_out[rows, lo + LANES:lo + RET_PAIR] = k2.astype(bf16)
            q1, q2 = _rot(z_ref[rows, C_RQ + lo:C_RQ + lo + LANES],
                          z_ref[rows, C_RQ + lo + LANES:C_RQ + lo + RET_PAIR], c2, s2)
            for e in range(2):
                keep = first if e == 0 else jnp.logical_not(first)
                qo = (2 * p + e) * RET_PAIR
                rq_out[rows, qo:qo + LANES] = jnp.where(keep, q1, 0.0).astype(bf16)
                rq_out[rows, qo + LANES:qo + RET_PAIR] = jnp.where(keep, q2, 0.0).astype(bf16)
        rv_out[rows, :] = z_ref[rows, C_RV:C_RV + RET_WIDTH].astype(bf16)
        rg_out[rows, :] = z_ref[rows, C_RG:C_RG + RET_WIDTH].astype(bf16)

    n_sub = h_ref.shape[0] // PRE_SUB
    subs = [slice(r * PRE_SUB, (r + 1) * PRE_SUB) for r in range(n_sub)]
    matmuls(subs[0])
    for r in range(n_sub):
        if r + 1 < n_sub:
            matmuls(subs[r + 1])
        rowwise(subs[r])


def _layer_spec(l, shape):
    nd = len(shape)
    return pl.BlockSpec((None,) + shape, lambda *_: (l,) + (0,) * nd, pipeline_mode=pl.Buffered(1))


def _pre_call(h, p, l, tabs):
    B = h.shape[0]
    tm = PRE_TM
    grid = (B, LP // tm)
    row = lambda w: pl.BlockSpec((None, tm, w), lambda b, j: (b, j, 0))
    tab = pl.BlockSpec((tm, LANES), lambda b, j: (j, 0))
    in_specs = [
        row(D_MODEL),
        _layer_spec(l, (1, D_MODEL)),
        _layer_spec(l, (D_MODEL, N_IN_PAD)),
        _layer_spec(l, (1, Q_LORA)),
        _layer_spec(l, (Q_LORA, MLA_HEADS * QK_PAD)),
        _layer_spec(l, (1, KV_LORA)),
        _layer_spec(l, (KV_LORA, MLA_HEADS * KV_PAD)),
        _layer_spec(l, (1, QK_PAD)),
        _layer_spec(l, (1, QK_PAD)),
    ] + [tab] * 6
    widths = (MLA_HEADS * QK_PAD, MLA_HEADS * QK_PAD, MLA_WIDTH, RET_HEADS * RET_PAIR, RET_WIDTH,
              RET_WIDTH, RET_WIDTH)
    out_shape = [jax.ShapeDtypeStruct((B, LP, w), bf16) for w in widths]
    out_specs = [row(w) for w in widths]
    return pl.pallas_call(
        _pre_kernel, grid=grid, in_specs=in_specs, out_specs=out_specs, out_shape=out_shape,
        scratch_shapes=[pltpu.VMEM((tm, N_IN_PAD), f32), pltpu.VMEM((tm, MLA_HEADS * QK_PAD), f32),
                        pltpu.VMEM((tm, MLA_HEADS * KV_PAD), f32)],
        name="pre",
        compiler_params=pltpu.CompilerParams(
            dimension_semantics=("parallel", "parallel"), vmem_limit_bytes=VMEM_LIMIT),
    )(h, p["attn_g"], p["w_in"], p["qa_g"], p["w_qb"], p["kva_g"], p["w_kvb"], p["q_g"], p["k_g"],
      *tabs)


def _attn_kernel(q_ref, k_ref, v_ref, og_ref, o_ref):
    og = og_ref[...]
    k_meta = k_ref[SEQ:LP, :]
    v_meta = v_ref[SEQ:LP, :]
    meta_valid = lax.broadcasted_iota(jnp.int32, (1, BLOCK), 1) >= PAD_ROWS

    def rowmax(s):
        return jnp.max(s, axis=-1, keepdims=True)

    def rowsum(p):
        return jnp.sum(p, axis=-1, keepdims=True)

    def finish(acc, l):
        o = acc * (1.0 / l)
        return (o * _rms(o, 1.0 / V_HEAD) * og).astype(bf16)

    tri = (lax.broadcasted_iota(jnp.int32, (ATT_TQ, ATT_TQ), 0)
           >= lax.broadcasted_iota(jnp.int32, (ATT_TQ, ATT_TQ), 1))

    def scores(i):
        lo, hi = i * ATT_TQ, (i + 1) * ATT_TQ
        q = q_ref[lo:hi, :]
        s_m = jnp.where(meta_valid, _dot_nt(q, k_meta), NEG_INF)
        s_x = _dot_nt(q, k_ref[0:hi, :])
        parts = [s_m] + ([s_x[:, :lo]] if lo else []) + [jnp.where(tri, s_x[:, lo:], NEG_INF)]
        return jnp.concatenate(parts, axis=1)

    order = ATT_ORDER
    s = scores(order[0])
    for n, i in enumerate(order):
        s_next = scores(order[n + 1]) if n + 1 < len(order) else None
        lo, hi = i * ATT_TQ, (i + 1) * ATT_TQ
        p = jnp.exp2(s - rowmax(s))
        acc = (_dot(p[:, :BLOCK].astype(bf16), v_meta)
               + _dot(p[:, BLOCK:].astype(bf16), v_ref[0:hi, :]))
        o_ref[lo:hi, :] = finish(acc, rowsum(p))
        s = s_next

    q = q_ref[SEQ:LP, :]
    mmask = (lax.broadcasted_iota(jnp.int32, (BLOCK, BLOCK), 0)
             >= lax.broadcasted_iota(jnp.int32, (BLOCK, BLOCK), 1)) & meta_valid
    s = jnp.where(mmask, _dot_nt(q, k_meta), NEG_INF)
    p = jnp.exp2(s - rowmax(s))
    o_ref[SEQ:LP, :] = finish(_dot(p.astype(bf16), v_meta), rowsum(p))


def _attn_call(q, k, v, out_g, l):
    B = q.shape[0]
    qk_spec = pl.BlockSpec((None, LP, QK_PAD), lambda b, h: (b, 0, h))
    v_spec = pl.BlockSpec((None, LP, V_HEAD), lambda b, h: (b, 0, h))
    return pl.pallas_call(
        _attn_kernel, grid=(B, MLA_HEADS),
        in_specs=[qk_spec, qk_spec, v_spec,
                  pl.BlockSpec((None, 1, V_HEAD), lambda b, h: (l, 0, h))],
        out_specs=v_spec,
        out_shape=jax.ShapeDtypeStruct((B, LP, MLA_WIDTH), bf16),
        name="attn",
        compiler_params=pltpu.CompilerParams(
            dimension_semantics=("parallel", "parallel"), vmem_limit_bytes=VMEM_LIMIT),
    )(q, k, v, out_g)


def _ret_kernel(q_ref, k_ref, v_ref, g_ref, lg_ref, ng_ref, nb_ref, o_ref):
    C = RET_CHUNK
    lg = lg_ref[0:1, :]
    lg2 = jnp.concatenate([lg, lg], axis=1)
    ri = lax.broadcasted_iota(jnp.int32, (C, C), 0)
    ci = lax.broadcasted_iota(jnp.int32, (C, C), 1)
    diff = (ri - ci).astype(f32)
    decay = jnp.where(diff >= 0, jnp.exp(jnp.maximum(diff, 0.0) * lg2), 0.0)
    idx = lax.broadcasted_iota(jnp.int32, (C, RET_HEAD), 0).astype(f32)
    xi = jnp.exp((idx + 1.0) * lg)
    ng, nb = ng_ref[...], nb_ref[...]

    def key_state(rows, n):
        zeta = jnp.exp((n - 1.0 - idx[:n]) * lg)
        kz = (k_ref[rows, :].astype(f32) * jnp.concatenate([zeta, zeta], axis=1)).astype(bf16)
        return _dot_tn(v_ref[rows, :], kz)

    def emit(rows, n, state_t):
        qb = q_ref[rows, :]
        s = _dot_nt(qb, k_ref[rows, :]) * decay[:n, :n]
        o = _dot(s.astype(bf16), v_ref[rows, :])
        if state_t is not None:
            o = o + _dot_nt(qb, state_t) * xi[:n]
        mu = jnp.mean(o, axis=-1, keepdims=True)
        d = o - mu
        var = jnp.mean(d * d, axis=-1, keepdims=True)
        on = d * lax.rsqrt(var + EPS) * ng + nb
        g = g_ref[rows, :].astype(f32)
        o_ref[rows, :] = (g * _sigmoid(g) * on).astype(bf16)

    meta_rows = slice(SEQ, LP)
    emit(meta_rows, BLOCK, None)
    state_t = key_state(meta_rows, BLOCK)
    chunk_decay = jnp.exp(float(C) * lg2)
    for c in range(SEQ // C):
        rows = slice(c * C, (c + 1) * C)
        emit(rows, C, state_t.astype(bf16))
        if c + 1 < SEQ // C:
            state_t = state_t * chunk_decay + key_state(rows, C)


def _ret_call(rq, rk, rv, rg, log_g, norm_g, norm_b, l):
    B = rq.shape[0]
    q_spec = pl.BlockSpec((None, LP, RET_PAIR), lambda b, h: (b, 0, h))
    k_spec = pl.BlockSpec((None, LP, RET_PAIR), lambda b, h: (b, 0, h // 2))
    spec = pl.BlockSpec((None, LP, RET_HEAD), lambda b, h: (b, 0, h))
    vec = pl.BlockSpec((None, 1, RET_HEAD), lambda b, h: (l, 0, h))
    return pl.pallas_call(
        _ret_kernel, grid=(B, RET_HEADS),
        in_specs=[q_spec, k_spec, spec, spec,
                  pl.BlockSpec((None, 8, LANES), lambda b, h: (h, 0, 0)), vec, vec],
        out_specs=spec,
        out_shape=jax.ShapeDtypeStruct((B, LP, RET_WIDTH), bf16),
        name="ret",
        compiler_params=pltpu.CompilerParams(
            dimension_semantics=("parallel", "parallel"), vmem_limit_bytes=VMEM_LIMIT),
    )(rq, rk, rv, rg, log_g, norm_g, norm_b)


def _post_kernel(h_ref, ym_ref, yr_ref, wo_ref, fg_ref, wgu_ref, wd_ref, o_ref, act_ref):
    h1 = (h_ref[...] + _dot(ym_ref[...], wo_ref[0:MLA_WIDTH, :])
          + _dot(yr_ref[...], wo_ref[MLA_WIDTH:MLA_WIDTH + RET_WIDTH, :]))
    hf = (h1 * _rms(h1, 1.0 / D_MODEL) * fg_ref[...]).astype(bf16)
    for c in range(D_FF // FF_TILE):
        lo = c * FF_TILE
        gate = _dot(hf, wgu_ref[:, lo:lo + FF_TILE])
        up = _dot(hf, wgu_ref[:, D_FF + lo:D_FF + lo + FF_TILE])
        act_ref[:, lo:lo + FF_TILE] = (gate * _sigmoid(gate) * up).astype(bf16)
    o_ref[...] = h1 + _dot(act_ref[...], wd_ref[...])


def _post_call(h, ym, yr, p, l, last):
    B = h.shape[0]
    tm = POST_TM_LAST if last else PRE_TM
    rows_out = SEQ if last else LP
    row = lambda w: pl.BlockSpec((None, tm, w), lambda b, j: (b, j, 0))
    return pl.pallas_call(
        _post_kernel, grid=(B, rows_out // tm),
        in_specs=[row(D_MODEL), row(MLA_WIDTH), row(RET_WIDTH),
                  _layer_spec(l, (MLA_WIDTH + RET_WIDTH, D_MODEL)),
                  _layer_spec(l, (1, D_MODEL)),
                  _layer_spec(l, (D_MODEL, 2 * D_FF)),
                  _layer_spec(l, (D_FF, D_MODEL))],
        out_specs=row(D_MODEL),
        out_shape=jax.ShapeDtypeStruct((B, rows_out, D_MODEL), f32),
        scratch_shapes=[pltpu.VMEM((tm, D_FF), bf16)],
        name="post_last" if last else "post",
        compiler_params=pltpu.CompilerParams(
            dimension_semantics=("parallel", "parallel"), vmem_limit_bytes=VMEM_LIMIT),
    )(h, ym, yr, p["w_out"], p["ffn_g"], p["w_gu"], p["w_down"])


def _rope_tables():
    r = np.arange(LP)
    pos = np.where(r < SEQ, r + N_META, np.maximum(r - (SEQ + PAD_ROWS), 0)).astype(np.float32)
    valid = ((r < SEQ) | (r >= SEQ + PAD_ROWS)).astype(np.float32)[:, None]

    def cs(dim):
        inv = np.float32(ROPE_BASE) ** (-np.arange(0, dim, 2, dtype=np.float32) / np.float32(dim))
        ang = pos[:, None] * inv[None, :].astype(np.float32)
        return np.cos(ang).astype(np.float32), np.sin(ang).astype(np.float32)

    cm, sm = cs(QK_ROPE)
    z32 = np.zeros_like(cm)
    ca = np.concatenate([np.ones((LP, HALF), np.float32), cm, z32], axis=1)
    sa = np.concatenate([np.zeros((LP, HALF), np.float32), sm, z32], axis=1)
    cr, sr = cs(RET_HEAD)
    c2 = np.concatenate([cr, cr], axis=1)
    s2 = np.concatenate([sr, sr], axis=1)
    kscale = valid * np.float32(RET_HEAD ** -0.5)
    return tuple(jnp.asarray(t) for t in (ca, sa, c2, s2, c2 * kscale, s2 * kscale))


def _mla_head_layout(a):
    q = QK_ROPE // 2
    z = jnp.zeros(a.shape[:-1] + (q,), a.dtype)
    return jnp.concatenate([a[..., :HALF], a[..., QK_NOPE:QK_NOPE + q], z,
                            a[..., HALF:QK_NOPE], a[..., QK_NOPE + q:], z], axis=-1)


def _kpe_layout(a):
    q = QK_ROPE // 2
    z64 = jnp.zeros(a.shape[:-1] + (HALF,), a.dtype)
    z32 = jnp.zeros(a.shape[:-1] + (q,), a.dtype)
    return jnp.concatenate([z64, a[..., :q], z32, z64, a[..., q:], z32], axis=-1)


def _pair_layout(a):
    lead = a.shape[:-1]
    a = a.reshape(lead + (RET_HEADS // 2, 2, 2, HALF))
    a = jnp.swapaxes(a, -3, -2)
    return a.reshape(lead + (RET_WIDTH,))


def _prep_params(attn_norm_g, w_in, q_a_norm_g, w_q_b, kv_a_norm_g, w_kv_b, q_norm_g, k_norm_g,
                 mla_out_norm_g, ret_norm_g, ret_norm_b, w_out, ffn_norm_g, w_gate_up, w_down):
    depth = w_in.shape[0]
    o_kpe = Q_LORA + KV_LORA
    o_rq = o_kpe + QK_ROPE
    o_rk = o_rq + RET_WIDTH
    o_rv = o_rk + RET_WIDTH
    wi = w_in.astype(bf16)
    w_in_p = jnp.concatenate(
        [wi[..., :o_kpe], _kpe_layout(wi[..., o_kpe:o_rq]), _pair_layout(wi[..., o_rq:o_rk]),
         _pair_layout(wi[..., o_rk:o_rv]), wi[..., o_rv:]], axis=-1)
    wq = _mla_head_layout(w_q_b.astype(bf16).reshape(depth, Q_LORA, MLA_HEADS, QK_HEAD))
    wkv = w_kv_b.astype(bf16).reshape(depth, KV_LORA, MLA_HEADS, QK_NOPE + V_HEAD)
    z64 = jnp.zeros((depth, KV_LORA, MLA_HEADS, HALF), bf16)
    wkv = jnp.concatenate([wkv[..., :HALF], z64, wkv[..., HALF:QK_NOPE], z64, wkv[..., QK_NOPE:]],
                          axis=-1)
    scale = QK_HEAD ** -0.5 * LOG2_E
    vec = lambda a: a[:, None, :]
    return {
        "attn_g": vec(attn_norm_g),
        "w_in": w_in_p,
        "qa_g": vec(q_a_norm_g),
        "w_qb": wq.reshape(depth, Q_LORA, MLA_HEADS * QK_PAD),
        "kva_g": vec(kv_a_norm_g),
        "w_kvb": wkv.reshape(depth, KV_LORA, MLA_HEADS * KV_PAD),
        "q_g": vec(_mla_head_layout(q_norm_g * scale)),
        "k_g": vec(_mla_head_layout(k_norm_g)),
        "out_g": vec(mla_out_norm_g),
        "ret_g": vec(ret_norm_g),
        "ret_b": vec(ret_norm_b),
        "w_out": w_out.astype(bf16),
        "ffn_g": vec(ffn_norm_g),
        "w_gu": w_gate_up.astype(bf16),
        "w_down": w_down.astype(bf16),
    }


def kernel(x, meta_tokens, attn_norm_g, w_in, q_a_norm_g, w_q_b, kv_a_norm_g, w_kv_b, q_norm_g,
           k_norm_g, mla_out_norm_g, ret_norm_g, ret_norm_b, w_out, ffn_norm_g, w_gate_up, w_down):
    B = x.shape[0]
    depth = w_in.shape[0]
    meta = jnp.broadcast_to(meta_tokens[None].astype(x.dtype), (B, N_META, D_MODEL))
    h = jnp.concatenate([x, jnp.zeros((B, PAD_ROWS, D_MODEL), x.dtype), meta], axis=1)
    tabs = _rope_tables()
    gamma = np.float32(1.0) - np.float32(2.0) ** (np.float32(-5.0) - np.arange(RET_HEADS, dtype=np.float32))
    log_g = jnp.asarray(np.broadcast_to(np.log(gamma)[:, None, None], (RET_HEADS, 8, LANES)))
    p = _prep_params(attn_norm_g, w_in, q_a_norm_g, w_q_b, kv_a_norm_g, w_kv_b, q_norm_g, k_norm_g,
                     mla_out_norm_g, ret_norm_g, ret_norm_b, w_out, ffn_norm_g, w_gate_up, w_down)
    for l in range(depth):
        q, k, v, rq, rk, rv, rg = _pre_call(h, p, l, tabs)
        y_mla = _attn_call(q, k, v, p["out_g"], l)
        y_ret = _ret_call(rq, rk, rv, rg, log_g, p["ret_g"], p["ret_b"], l)
        h = _post_call(h, y_mla, y_ret, p, l, last=(l == depth - 1))
    return h
```

```python
import functools

import numpy as np

import jax
import jax.numpy as jnp
from jax import lax
from jax.experimental import pallas as pl
from jax.experimental.pallas import tpu as pltpu

D_MODEL = 1024
SEQ = 2048
N_META = 16
BLOCK = 128
MLA_HEADS = 4
Q_LORA = 256
KV_LORA = 256
QK_NOPE = 128
QK_ROPE = 64
QK_HEAD = QK_NOPE + QK_ROPE
V_HEAD = 128
MLA_WIDTH = MLA_HEADS * V_HEAD
RET_HEADS = 4
RET_HEAD = 128
RET_WIDTH = RET_HEADS * RET_HEAD
D_FF = 2816
ROPE_BASE = 10000.0
EPS = 1e-6
NEG_INF = -1e30

LANES = 128
HALF = LANES // 2
PAD_ROWS = BLOCK - N_META
QK_PAD = 2 * LANES
KV_PAD = 3 * LANES
RET_PAIR = 2 * RET_HEAD

C_CQ = 0
C_CKV = C_CQ + Q_LORA
C_KPE = C_CKV + KV_LORA
C_RQ = C_KPE + 2 * LANES
C_RK = C_RQ + RET_WIDTH
C_RV = C_RK + RET_WIDTH
C_RG = C_RV + RET_WIDTH
N_IN_PAD = C_RG + RET_WIDTH

ROW_TM = 512
PRE_SUB = 256
ATT_TQ = 512
ATT_ORDER = (3, 2, 1, 0)
RET_CHUNK = 256
FF_TILE = 256
LOG2_E = 1.4426950408889634
VMEM_LIMIT = 56 * 1024 * 1024

f32 = jnp.float32
bf16 = jnp.bfloat16


def _dot(a, b):
    return jnp.dot(a, b, preferred_element_type=f32)


def _dot_nt(a, b):
    return lax.dot_general(a, b, (((1,), (1,)), ((), ())), preferred_element_type=f32)


def _dot_tn(a, b):
    return lax.dot_general(a, b, (((0,), (0,)), ((), ())), preferred_element_type=f32)


def _rms(x, inv_n):
    return lax.rsqrt(jnp.sum(x * x, axis=-1, keepdims=True) * inv_n + EPS)


def _sigmoid(x):
    return 1.0 / (1.0 + jnp.exp(-x))


def _rot(a, b, c, s):
    return a * c - b * s, b * c + a * s


def _params(n_grid_axes):
    return pltpu.CompilerParams(dimension_semantics=("parallel",) * n_grid_axes,
                                vmem_limit_bytes=VMEM_LIMIT)


def _layer_spec(l, shape):
    nd = len(shape)
    return pl.BlockSpec((None,) + shape, lambda *_: (l,) + (0,) * nd, pipeline_mode=pl.Buffered(1))


def _pre_kernel(h_ref, g_ref, whead_ref, wqk_ref, wvg_ref, qag_ref, wqb_ref, kvag_ref, wkvb_ref,
                qg_ref, kg_ref, ca_ref, sa_ref, c2_ref, s2_ref, ck_ref, sk_ref,
                q_out, k_out, v_out, rq_out, rk_out, rv_out, rg_out, z_ref, q_ref, kv_ref, *, sub):
    qga, qgb = qg_ref[:, :LANES], qg_ref[:, LANES:]
    kga, kgb = kg_ref[:, :LANES], kg_ref[:, LANES:]
    first = lax.broadcasted_iota(jnp.int32, (1, LANES), 1) < HALF

    def matmuls(rows):
        x = h_ref[rows, :]
        hb = (x * _rms(x, 1.0 / D_MODEL) * g_ref[...]).astype(bf16)
        z_ref[rows, :C_RQ] = _dot(hb, whead_ref[...])
        z_ref[rows, C_RQ:C_RV] = _dot(hb, wqk_ref[...])
        z_ref[rows, C_RV:] = _dot(hb, wvg_ref[...])
        cq = z_ref[rows, C_CQ:C_CQ + Q_LORA]
        cqn = (cq * _rms(cq, 1.0 / Q_LORA) * qag_ref[...]).astype(bf16)
        q_ref[rows, :] = _dot(cqn, wqb_ref[...])
        ckv = z_ref[rows, C_CKV:C_CKV + KV_LORA]
        ckvn = (ckv * _rms(ckv, 1.0 / KV_LORA) * kvag_ref[...]).astype(bf16)
        kv_ref[rows, :] = _dot(ckvn, wkvb_ref[...])

    def rowwise(rows):
        ca, sa = ca_ref[rows, :], sa_ref[rows, :]

        def norm_rope(a, b, ga, gb):
            r = lax.rsqrt(jnp.sum(a * a + b * b, axis=-1, keepdims=True) * (1.0 / QK_HEAD) + EPS)
            ao, bo = _rot(a * ga, b * gb, ca, sa)
            return (ao * r).astype(bf16), (bo * r).astype(bf16)

        kpa = z_ref[rows, C_KPE:C_KPE + LANES]
        kpb = z_ref[rows, C_KPE + LANES:C_KPE + 2 * LANES]
        for h in range(MLA_HEADS):
            lo = h * QK_PAD
            q_out[rows, lo:lo + LANES], q_out[rows, lo + LANES:lo + QK_PAD] = norm_rope(
                q_ref[rows, lo:lo + LANES], q_ref[rows, lo + LANES:lo + QK_PAD], qga, qgb)
            kl = h * KV_PAD
            k_out[rows, lo:lo + LANES], k_out[rows, lo + LANES:lo + QK_PAD] = norm_rope(
                kv_ref[rows, kl:kl + LANES] + kpa, kv_ref[rows, kl + LANES:kl + 2 * LANES] + kpb,
                kga, kgb)
            v_out[rows, h * V_HEAD:(h + 1) * V_HEAD] = (
                kv_ref[rows, kl + 2 * LANES:kl + KV_PAD].astype(bf16))

        c2, s2, ck, sk = c2_ref[rows, :], s2_ref[rows, :], ck_ref[rows, :], sk_ref[rows, :]
        for p in range(RET_HEADS // 2):
            lo = p * RET_PAIR
            k1, k2 = _rot(z_ref[rows, C_RK + lo:C_RK + lo + LANES],
                          z_ref[rows, C_RK + lo + LANES:C_RK + lo + RET_PAIR], ck, sk)
            rk_out[rows, lo:lo + LANES] = k1.astype(bf16)
            rk_out[rows, lo + LANES:lo + RET_PAIR] = k2.astype(bf16)
            q1, q2 = _rot(z_ref[rows, C_RQ + lo:C_RQ + lo + LANES],
                          z_ref[rows, C_RQ + lo + LANES:C_RQ + lo + RET_PAIR], c2, s2)
            for e in range(2):
                keep = first if e == 0 else jnp.logical_not(first)
                qo = (2 * p + e) * RET_PAIR
                rq_out[rows, qo:qo + LANES] = jnp.where(keep, q1, 0.0).astype(bf16)
                rq_out[rows, qo + LANES:qo + RET_PAIR] = jnp.where(keep, q2, 0.0).astype(bf16)
        rv_out[rows, :] = z_ref[rows, C_RV:C_RV + RET_WIDTH].astype(bf16)
        rg_out[rows, :] = z_ref[rows, C_RG:C_RG + RET_WIDTH].astype(bf16)

    n_sub = h_ref.shape[0] // sub
    subs = [slice(r * sub, (r + 1) * sub) for r in range(n_sub)]
    matmuls(subs[0])
    for r in range(n_sub):
        if r + 1 < n_sub:
            matmuls(subs[r + 1])
        rowwise(subs[r])


def _pre_call(h, p, l, tabs, tm, name):
    B, R, _ = h.shape
    row = lambda w: pl.BlockSpec((None, tm, w), lambda b, j: (b, j, 0))
    tab = pl.BlockSpec((tm, LANES), lambda b, j: (j, 0))
    in_specs = [
        row(D_MODEL),
        _layer_spec(l, (1, D_MODEL)),
        _layer_spec(l, (D_MODEL, C_RQ)),
        _layer_spec(l, (D_MODEL, C_RV - C_RQ)),
        _layer_spec(l, (D_MODEL, N_IN_PAD - C_RV)),
        _layer_spec(l, (1, Q_LORA)),
        _layer_spec(l, (Q_LORA, MLA_HEADS * QK_PAD)),
        _layer_spec(l, (1, KV_LORA)),
        _layer_spec(l, (KV_LORA, MLA_HEADS * KV_PAD)),
        _layer_spec(l, (1, QK_PAD)),
        _layer_spec(l, (1, QK_PAD)),
    ] + [tab] * 6
    widths = (MLA_HEADS * QK_PAD, MLA_HEADS * QK_PAD, MLA_WIDTH, RET_HEADS * RET_PAIR, RET_WIDTH,
              RET_WIDTH, RET_WIDTH)
    return pl.pallas_call(
        functools.partial(_pre_kernel, sub=min(PRE_SUB, tm)), grid=(B, R // tm),
        in_specs=in_specs, out_specs=[row(w) for w in widths],
        out_shape=[jax.ShapeDtypeStruct((B, R, w), bf16) for w in widths],
        scratch_shapes=[pltpu.VMEM((tm, N_IN_PAD), f32), pltpu.VMEM((tm, MLA_HEADS * QK_PAD), f32),
                        pltpu.VMEM((tm, MLA_HEADS * KV_PAD), f32)],
        name=name, compiler_params=_params(2),
    )(h, p["attn_g"], p["w_head"], p["w_qk"], p["w_vg"], p["qa_g"], p["w_qb"], p["kva_g"], p["w_kvb"],
      p["q_g"], p["k_g"], *tabs)


def _rowmax(s):
    return jnp.max(s, axis=-1, keepdims=True)


def _rowsum(p):
    return jnp.sum(p, axis=-1, keepdims=True)


def _attn_finish(acc, l, og):
    o = acc * (1.0 / l)
    return (o * _rms(o, 1.0 / V_HEAD) * og).astype(bf16)


def _meta_valid():
    return lax.broadcasted_iota(jnp.int32, (1, BLOCK), 1) >= PAD_ROWS


def _attn_kernel(q_ref, k_ref, v_ref, km_ref, vm_ref, og_ref, o_ref):
    og = og_ref[...]
    k_meta, v_meta = km_ref[...], vm_ref[...]
    meta_valid = _meta_valid()
    tri = (lax.broadcasted_iota(jnp.int32, (ATT_TQ, ATT_TQ), 0)
           >= lax.broadcasted_iota(jnp.int32, (ATT_TQ, ATT_TQ), 1))

    def scores(i):
        lo, hi = i * ATT_TQ, (i + 1) * ATT_TQ
        q = q_ref[lo:hi, :]
        s_m = jnp.where(meta_valid, _dot_nt(q, k_meta), NEG_INF)
        s_x = _dot_nt(q, k_ref[0:hi, :])
        parts = [s_m] + ([s_x[:, :lo]] if lo else []) + [jnp.where(tri, s_x[:, lo:], NEG_INF)]
        return jnp.concatenate(parts, axis=1)

    order = ATT_ORDER
    s = scores(order[0])
    for n, i in enumerate(order):
        s_next = scores(order[n + 1]) if n + 1 < len(order) else None
        lo, hi = i * ATT_TQ, (i + 1) * ATT_TQ
        p = jnp.exp2(s - _rowmax(s))
        acc = (_dot(p[:, :BLOCK].astype(bf16), v_meta)
               + _dot(p[:, BLOCK:].astype(bf16), v_ref[0:hi, :]))
        o_ref[lo:hi, :] = _attn_finish(acc, _rowsum(p), og)
        s = s_next


def _attn_meta_kernel(q_ref, k_ref, v_ref, og_ref, o_ref):
    mask = (lax.broadcasted_iota(jnp.int32, (BLOCK, BLOCK), 0)
            >= lax.broadcasted_iota(jnp.int32, (BLOCK, BLOCK), 1)) & _meta_valid()
    s = jnp.where(mask, _dot_nt(q_ref[...], k_ref[...]), NEG_INF)
    p = jnp.exp2(s - _rowmax(s))
    o_ref[...] = _attn_finish(_dot(p.astype(bf16), v_ref[...]), _rowsum(p), og_ref[...])


def _attn_call(q, k, v, km, vm, out_g, l):
    B = q.shape[0]
    qk_spec = pl.BlockSpec((None, SEQ, QK_PAD), lambda b, h: (b, 0, h))
    v_spec = pl.BlockSpec((None, SEQ, V_HEAD), lambda b, h: (b, 0, h))
    return pl.pallas_call(
        _attn_kernel, grid=(B, MLA_HEADS),
        in_specs=[qk_spec, qk_spec, v_spec,
                  pl.BlockSpec((None, BLOCK, QK_PAD), lambda b, h: (0, 0, h)),
                  pl.BlockSpec((None, BLOCK, V_HEAD), lambda b, h: (0, 0, h)),
                  pl.BlockSpec((None, 1, V_HEAD), lambda b, h: (l, 0, h))],
        out_specs=v_spec,
        out_shape=jax.ShapeDtypeStruct((B, SEQ, MLA_WIDTH), bf16),
        name="attn", compiler_params=_params(2),
    )(q, k, v, km, vm, out_g)


def _attn_meta_call(qm, km, vm, out_g, l):
    qk_spec = pl.BlockSpec((None, BLOCK, QK_PAD), lambda h: (0, 0, h))
    v_spec = pl.BlockSpec((None, BLOCK, V_HEAD), lambda h: (0, 0, h))
    return pl.pallas_call(
        _attn_meta_kernel, grid=(MLA_HEADS,),
        in_specs=[qk_spec, qk_spec, v_spec, pl.BlockSpec((None, 1, V_HEAD), lambda h: (l, 0, h))],
        out_specs=v_spec,
        out_shape=jax.ShapeDtypeStruct((1, BLOCK, MLA_WIDTH), bf16),
        name="attn_meta", compiler_params=_params(1),
    )(qm, km, vm, out_g)


def _ret_tables(lg_ref, C):
    lg = lg_ref[0:1, :]
    lgc = jnp.concatenate([lg] * (C // LANES), axis=1)
    ri = lax.broadcasted_iota(jnp.int32, (C, C), 0)
    ci = lax.broadcasted_iota(jnp.int32, (C, C), 1)
    diff = (ri - ci).astype(f32)
    decay = jnp.where(diff >= 0, jnp.exp(jnp.maximum(diff, 0.0) * lgc), 0.0)
    idx = lax.broadcasted_iota(jnp.int32, (C, RET_HEAD), 0).astype(f32)
    xi = jnp.exp((idx + 1.0) * lg)
    return lg, decay, idx, xi


def _ret_key_state(k, v, idx, lg):
    n = k.shape[0]
    zeta = jnp.exp((n - 1.0 - idx[:n]) * lg)
    vz = (v.astype(f32) * zeta).astype(bf16)
    return _dot_tn(vz, k)


def _ret_mix(q, k, v, decay, state_t, xi):
    n = q.shape[0]
    o = _dot((_dot_nt(q, k) * decay[:n, :n]).astype(bf16), v)
    if state_t is not None:
        o = o + _dot_nt(q, state_t.astype(bf16)) * xi[:n]
    return o


def _ret_emit(o, g, ng, nb):
    mu = jnp.mean(o, axis=-1, keepdims=True)
    d = o - mu
    var = jnp.mean(d * d, axis=-1, keepdims=True)
    on = d * lax.rsqrt(var + EPS) * ng + nb
    g = g.astype(f32)
    return (g * _sigmoid(g) * on).astype(bf16)


def _ret_kernel(q_ref, k_ref, v_ref, g_ref, km_ref, vm_ref, lg_ref, ng_ref, nb_ref, o_ref):
    C = RET_CHUNK
    lg, decay, idx, xi = _ret_tables(lg_ref, C)
    ng, nb = ng_ref[...], nb_ref[...]
    chunk_decay = jnp.exp(float(C) * jnp.concatenate([lg, lg], axis=1))
    n_chunks = SEQ // C
    chunks = [slice(c * C, (c + 1) * C) for c in range(n_chunks)]
    state_t = _ret_key_state(km_ref[...], vm_ref[...], idx, lg)
    o_prev = None
    for c, rows in enumerate(chunks):
        o = _ret_mix(q_ref[rows, :], k_ref[rows, :], v_ref[rows, :], decay, state_t, xi)
        if c + 1 < n_chunks:
            state_t = state_t * chunk_decay + _ret_key_state(k_ref[rows, :], v_ref[rows, :], idx, lg)
        if o_prev is not None:
            o_ref[chunks[c - 1], :] = _ret_emit(o_prev, g_ref[chunks[c - 1], :], ng, nb)
        o_prev = o
    o_ref[chunks[-1], :] = _ret_emit(o_prev, g_ref[chunks[-1], :], ng, nb)


def _ret_meta_kernel(q_ref, k_ref, v_ref, g_ref, lg_ref, ng_ref, nb_ref, o_ref):
    _, decay, _, _ = _ret_tables(lg_ref, BLOCK)
    o = _ret_mix(q_ref[...], k_ref[...], v_ref[...], decay, None, None)
    o_ref[...] = _ret_emit(o, g_ref[...], ng_ref[...], nb_ref[...])


def _ret_call(rq, rk, rv, rg, rkm, rvm, log_g, norm_g, norm_b, l):
    B = rq.shape[0]
    spec = pl.BlockSpec((None, SEQ, RET_HEAD), lambda b, h: (b, 0, h))
    vec = pl.BlockSpec((None, 1, RET_HEAD), lambda b, h: (l, 0, h))
    return pl.pallas_call(
        _ret_kernel, grid=(B, RET_HEADS),
        in_specs=[pl.BlockSpec((None, SEQ, RET_PAIR), lambda b, h: (b, 0, h)),
                  pl.BlockSpec((None, SEQ, RET_PAIR), lambda b, h: (b, 0, h // 2)),
                  spec, spec,
                  pl.BlockSpec((None, BLOCK, RET_PAIR), lambda b, h: (0, 0, h // 2)),
                  pl.BlockSpec((None, BLOCK, RET_HEAD), lambda b, h: (0, 0, h)),
                  pl.BlockSpec((None, 8, LANES), lambda b, h: (h, 0, 0)), vec, vec],
        out_specs=spec,
        out_shape=jax.ShapeDtypeStruct((B, SEQ, RET_WIDTH), bf16),
        name="ret", compiler_params=_params(2),
    )(rq, rk, rv, rg, rkm, rvm, log_g, norm_g, norm_b)


def _ret_meta_call(rqm, rkm, rvm, rgm, log_g, norm_g, norm_b, l):
    spec = pl.BlockSpec((None, BLOCK, RET_HEAD), lambda h: (0, 0, h))
    vec = pl.BlockSpec((None, 1, RET_HEAD), lambda h: (l, 0, h))
    return pl.pallas_call(
        _ret_meta_kernel, grid=(RET_HEADS,),
        in_specs=[pl.BlockSpec((None, BLOCK, RET_PAIR), lambda h: (0, 0, h)),
                  pl.BlockSpec((None, BLOCK, RET_PAIR), lambda h: (0, 0, h // 2)),
                  spec, spec, pl.BlockSpec((None, 8, LANES), lambda h: (h, 0, 0)), vec, vec],
        out_specs=spec,
        out_shape=jax.ShapeDtypeStruct((1, BLOCK, RET_WIDTH), bf16),
        name="ret_meta", compiler_params=_params(1),
    )(rqm, rkm, rvm, rgm, log_g, norm_g, norm_b)


def _post_kernel(h_ref, ym_ref, yr_ref, wo_ref, fg_ref, wgu_ref, wd_ref, o_ref, act_ref):
    h1 = (h_ref[...] + _dot(ym_ref[...], wo_ref[0:MLA_WIDTH, :])
          + _dot(yr_ref[...], wo_ref[MLA_WIDTH:MLA_WIDTH + RET_WIDTH, :]))
    hf = (h1 * _rms(h1, 1.0 / D_MODEL) * fg_ref[...]).astype(bf16)
    for c in range(D_FF // FF_TILE):
        lo = c * FF_TILE
        gate = _dot(hf, wgu_ref[:, lo:lo + FF_TILE])
        up = _dot(hf, wgu_ref[:, D_FF + lo:D_FF + lo + FF_TILE])
        act_ref[:, lo:lo + FF_TILE] = (gate * _sigmoid(gate) * up).astype(bf16)
    o_ref[...] = h1 + _dot(act_ref[...], wd_ref[...])


def _post_call(h, ym, yr, p, l, tm, name):
    B, R, _ = h.shape
    row = lambda w: pl.BlockSpec((None, tm, w), lambda b, j: (b, j, 0))
    return pl.pallas_call(
        _post_kernel, grid=(B, R // tm),
        in_specs=[row(D_MODEL), row(MLA_WIDTH), row(RET_WIDTH),
                  _layer_spec(l, (MLA_WIDTH + RET_WIDTH, D_MODEL)),
                  _layer_spec(l, (1, D_MODEL)),
                  _layer_spec(l, (D_MODEL, 2 * D_FF)),
                  _layer_spec(l, (D_FF, D_MODEL))],
        out_specs=row(D_MODEL),
        out_shape=jax.ShapeDtypeStruct((B, R, D_MODEL), f32),
        scratch_shapes=[pltpu.VMEM((tm, D_FF), bf16)],
        name=name, compiler_params=_params(2),
    )(h, ym, yr, p["w_out"], p["ffn_g"], p["w_gu"], p["w_down"])


def _rope_tables(pos, valid):
    pos = pos.astype(np.float32)
    n = pos.shape[0]

    def cs(dim):
        inv = np.float32(ROPE_BASE) ** (-np.arange(0, dim, 2, dtype=np.float32) / np.float32(dim))
        ang = pos[:, None] * inv[None, :].astype(np.float32)
        return np.cos(ang).astype(np.float32), np.sin(ang).astype(np.float32)

    cm, sm = cs(QK_ROPE)
    z32 = np.zeros_like(cm)
    ca = np.concatenate([np.ones((n, HALF), np.float32), cm, z32], axis=1)
    sa = np.concatenate([np.zeros((n, HALF), np.float32), sm, z32], axis=1)
    cr, sr = cs(RET_HEAD)
    c2 = np.concatenate([cr, cr], axis=1)
    s2 = np.concatenate([sr, sr], axis=1)
    kscale = valid.astype(np.float32)[:, None] * np.float32(RET_HEAD ** -0.5)
    return tuple(jnp.asarray(t) for t in (ca, sa, c2, s2, c2 * kscale, s2 * kscale))


def _mla_head_layout(a):
    q = QK_ROPE // 2
    z = jnp.zeros(a.shape[:-1] + (q,), a.dtype)
    return jnp.concatenate([a[..., :HALF], a[..., QK_NOPE:QK_NOPE + q], z,
                            a[..., HALF:QK_NOPE], a[..., QK_NOPE + q:], z], axis=-1)


def _kpe_layout(a):
    q = QK_ROPE // 2
    z64 = jnp.zeros(a.shape[:-1] + (HALF,), a.dtype)
    z32 = jnp.zeros(a.shape[:-1] + (q,), a.dtype)
    return jnp.concatenate([z64, a[..., :q], z32, z64, a[..., q:], z32], axis=-1)


def _pair_layout(a):
    lead = a.shape[:-1]
    a = a.reshape(lead + (RET_HEADS // 2, 2, 2, HALF))
    a = jnp.swapaxes(a, -3, -2)
    return a.reshape(lead + (RET_WIDTH,))


def _prep_params(attn_norm_g, w_in, q_a_norm_g, w_q_b, kv_a_norm_g, w_kv_b, q_norm_g, k_norm_g,
                 mla_out_norm_g, ret_norm_g, ret_norm_b, w_out, ffn_norm_g, w_gate_up, w_down):
    depth = w_in.shape[0]
    o_kpe = Q_LORA + KV_LORA
    o_rq = o_kpe + QK_ROPE
    o_rk = o_rq + RET_WIDTH
    o_rv = o_rk + RET_WIDTH
    w_head = jnp.concatenate([w_in[..., :o_kpe], _kpe_layout(w_in[..., o_kpe:o_rq])],
                             axis=-1).astype(bf16)
    w_qk = jnp.concatenate([_pair_layout(w_in[..., o_rq:o_rk]), _pair_layout(w_in[..., o_rk:o_rv])],
                           axis=-1).astype(bf16)
    w_vg = w_in[..., o_rv:].astype(bf16)
    wq = _mla_head_layout(w_q_b.astype(bf16).reshape(depth, Q_LORA, MLA_HEADS, QK_HEAD))
    wkv = w_kv_b.astype(bf16).reshape(depth, KV_LORA, MLA_HEADS, QK_NOPE + V_HEAD)
    z64 = jnp.zeros((depth, KV_LORA, MLA_HEADS, HALF), bf16)
    wkv = jnp.concatenate([wkv[..., :HALF], z64, wkv[..., HALF:QK_NOPE], z64, wkv[..., QK_NOPE:]],
                          axis=-1)
    scale = QK_HEAD ** -0.5 * LOG2_E
    vec = lambda a: a[:, None, :]
    return {
        "attn_g": vec(attn_norm_g),
        "w_head": w_head,
        "w_qk": w_qk,
        "w_vg": w_vg,
        "qa_g": vec(q_a_norm_g),
        "w_qb": wq.reshape(depth, Q_LORA, MLA_HEADS * QK_PAD),
        "kva_g": vec(kv_a_norm_g),
        "w_kvb": wkv.reshape(depth, KV_LORA, MLA_HEADS * KV_PAD),
        "q_g": vec(_mla_head_layout(q_norm_g * scale)),
        "k_g": vec(_mla_head_layout(k_norm_g)),
        "out_g": vec(mla_out_norm_g),
        "ret_g": vec(ret_norm_g),
        "ret_b": vec(ret_norm_b),
        "w_out": w_out.astype(bf16),
        "ffn_g": vec(ffn_norm_g),
        "w_gu": w_gate_up.astype(bf16),
        "w_down": w_down.astype(bf16),
    }


def kernel(x, meta_tokens, attn_norm_g, w_in, q_a_norm_g, w_q_b, kv_a_norm_g, w_kv_b, q_norm_g,
           k_norm_g, mla_out_norm_g, ret_norm_g, ret_norm_b, w_out, ffn_norm_g, w_gate_up, w_down):
    depth = w_in.shape[0]
    hx = x
    hm = jnp.concatenate([jnp.zeros((PAD_ROWS, D_MODEL), x.dtype), meta_tokens.astype(x.dtype)])[None]
    r = np.arange(BLOCK)
    tabs_x = _rope_tables(np.arange(SEQ) + N_META, np.ones(SEQ))
    tabs_m = _rope_tables(np.maximum(r - PAD_ROWS, 0), r >= PAD_ROWS)
    gamma = np.float32(1.0) - np.float32(2.0) ** (np.float32(-5.0) - np.arange(RET_HEADS, dtype=np.float32))
    log_g = jnp.asarray(np.broadcast_to(np.log(gamma)[:, None, None], (RET_HEADS, 8, LANES)))
    p = _prep_params(attn_norm_g, w_in, q_a_norm_g, w_q_b, kv_a_norm_g, w_kv_b, q_norm_g, k_norm_g,
                     mla_out_norm_g, ret_norm_g, ret_norm_b, w_out, ffn_norm_g, w_gate_up, w_down)
    for l in range(depth):
        q, k, v, rq, rk, rv, rg = _pre_call(hx, p, l, tabs_x, ROW_TM, "pre")
        qm, km, vm, rqm, rkm, rvm, rgm = _pre_call(hm, p, l, tabs_m, BLOCK, "pre_meta")
        y_mla = _attn_call(q, k, v, km, vm, p["out_g"], l)
        y_ret = _ret_call(rq, rk, rv, rg, rkm, rvm, log_g, p["ret_g"], p["ret_b"], l)
        hx = _post_call(hx, y_mla, y_ret, p, l, ROW_TM, "post")
        if l + 1 < depth:
            ym_mla = _attn_meta_call(qm, km, vm, p["out_g"], l)
            ym_ret = _ret_meta_call(rqm, rkm, rvm, rgm, log_g, p["ret_g"], p["ret_b"], l)
            hm = _post_call(hm, ym_mla, ym_ret, p, l, BLOCK, "post_meta")
    return hx
```

```python
import functools

import numpy as np

import jax
import jax.numpy as jnp
from jax import lax
from jax.experimental import pallas as pl
from jax.experimental.pallas import tpu as pltpu

D_MODEL = 1024
SEQ = 2048
N_META = 16
BLOCK = 128
MLA_HEADS = 4
Q_LORA = 256
KV_LORA = 256
QK_NOPE = 128
QK_ROPE = 64
QK_HEAD = QK_NOPE + QK_ROPE
V_HEAD = 128
MLA_WIDTH = MLA_HEADS * V_HEAD
RET_HEADS = 4
RET_HEAD = 128
RET_WIDTH = RET_HEADS * RET_HEAD
D_FF = 2816
ROPE_BASE = 10000.0
EPS = 1e-6
NEG_INF = -1e30

LANES = 128
HALF = LANES // 2
PAD_ROWS = BLOCK - N_META
QK_PAD = 2 * LANES
KV_PAD = 3 * LANES
RET_PAIR = 2 * RET_HEAD

C_CQ = 0
C_CKV = C_CQ + Q_LORA
C_KPE = C_CKV + KV_LORA
C_RQ = C_KPE + 2 * LANES
C_RK = C_RQ + RET_WIDTH
C_RV = C_RK + RET_WIDTH
C_RG = C_RV + RET_WIDTH
N_IN_PAD = C_RG + RET_WIDTH

ROW_TM = 512
PRE_SUB = 256
ATT_TQ = 512
ATT_ORDER = (3, 2, 1, 0)
ATT_HPS = 4
RET_CHUNK = 256
FF_TILE = 256
LOG2_E = 1.4426950408889634
VMEM_LIMIT = 56 * 1024 * 1024

f32 = jnp.float32
bf16 = jnp.bfloat16


def _dot(a, b):
    return jnp.dot(a, b, preferred_element_type=f32)


def _dot_nt(a, b):
    return lax.dot_general(a, b, (((1,), (1,)), ((), ())), preferred_element_type=f32)


def _dot_tn(a, b):
    return lax.dot_general(a, b, (((0,), (0,)), ((), ())), preferred_element_type=f32)


def _rms(x, inv_n):
    return lax.rsqrt(jnp.sum(x * x, axis=-1, keepdims=True) * inv_n + EPS)


def _sigmoid(x):
    return 1.0 / (1.0 + jnp.exp(-x))


def _rot(a, b, c, s):
    return a * c - b * s, b * c + a * s


def _params(n_grid_axes):
    return pltpu.CompilerParams(dimension_semantics=("parallel",) * n_grid_axes,
                                vmem_limit_bytes=VMEM_LIMIT)


def _layer_spec(l, shape):
    nd = len(shape)
    return pl.BlockSpec((None,) + shape, lambda *_: (l,) + (0,) * nd, pipeline_mode=pl.Buffered(1))


def _pre_kernel(h_ref, g_ref, whead_ref, wqk_ref, wvg_ref, qag_ref, wqb_ref, kvag_ref, wkvb_ref,
                qg_ref, kg_ref, ca_ref, sa_ref, c2_ref, s2_ref, ck_ref, sk_ref,
                q_out, k_out, v_out, rq_out, rk_out, rv_out, rg_out, z_ref, q_ref, kv_ref, *, sub):
    qga, qgb = qg_ref[:, :LANES], qg_ref[:, LANES:]
    kga, kgb = kg_ref[:, :LANES], kg_ref[:, LANES:]
    first = lax.broadcasted_iota(jnp.int32, (1, LANES), 1) < HALF

    def matmuls(rows):
        x = h_ref[rows, :]
        hb = (x * _rms(x, 1.0 / D_MODEL) * g_ref[...]).astype(bf16)
        z_ref[rows, :C_RQ] = _dot(hb, whead_ref[...])
        z_ref[rows, C_RQ:C_RV] = _dot(hb, wqk_ref[...])
        z_ref[rows, C_RV:] = _dot(hb, wvg_ref[...])
        cq = z_ref[rows, C_CQ:C_CQ + Q_LORA]
        cqn = (cq * _rms(cq, 1.0 / Q_LORA) * qag_ref[...]).astype(bf16)
        q_ref[rows, :] = _dot(cqn, wqb_ref[...])
        ckv = z_ref[rows, C_CKV:C_CKV + KV_LORA]
        ckvn = (ckv * _rms(ckv, 1.0 / KV_LORA) * kvag_ref[...]).astype(bf16)
        kv_ref[rows, :] = _dot(ckvn, wkvb_ref[...])

    def rowwise(rows):
        ca, sa = ca_ref[rows, :], sa_ref[rows, :]

        def norm_rope(a, b, ga, gb):
            r = lax.rsqrt(jnp.sum(a * a + b * b, axis=-1, keepdims=True) * (1.0 / QK_HEAD) + EPS)
            ao, bo = _rot(a * ga, b * gb, ca, sa)
            return (ao * r).astype(bf16), (bo * r).astype(bf16)

        kpa = z_ref[rows, C_KPE:C_KPE + LANES]
        kpb = z_ref[rows, C_KPE + LANES:C_KPE + 2 * LANES]
        for h in range(MLA_HEADS):
            lo = h * QK_PAD
            q_out[rows, lo:lo + LANES], q_out[rows, lo + LANES:lo + QK_PAD] = norm_rope(
                q_ref[rows, lo:lo + LANES], q_ref[rows, lo + LANES:lo + QK_PAD], qga, qgb)
            kl = h * KV_PAD
            k_out[rows, lo:lo + LANES], k_out[rows, lo + LANES:lo + QK_PAD] = norm_rope(
                kv_ref[rows, kl:kl + LANES] + kpa, kv_ref[rows, kl + LANES:kl + 2 * LANES] + kpb,
                kga, kgb)
            v_out[rows, h * V_HEAD:(h + 1) * V_HEAD] = (
                kv_ref[rows, kl + 2 * LANES:kl + KV_PAD].astype(bf16))

        c2, s2, ck, sk = c2_ref[rows, :], s2_ref[rows, :], ck_ref[rows, :], sk_ref[rows, :]
        for p in range(RET_HEADS // 2):
            lo = p * RET_PAIR
            k1, k2 = _rot(z_ref[rows, C_RK + lo:C_RK + lo + LANES],
                          z_ref[rows, C_RK + lo + LANES:C_RK + lo + RET_PAIR], ck, sk)
            rk_out[rows, lo:lo + LANES] = k1.astype(bf16)
            rk_out[rows, lo + LANES:lo + RET_PAIR] = k2.astype(bf16)
            q1, q2 = _rot(z_ref[rows, C_RQ + lo:C_RQ + lo + LANES],
                          z_ref[rows, C_RQ + lo + LANES:C_RQ + lo + RET_PAIR], c2, s2)
            for e in range(2):
                keep = first if e == 0 else jnp.logical_not(first)
                qo = (2 * p + e) * RET_PAIR
                rq_out[rows, qo:qo + LANES] = jnp.where(keep, q1, 0.0).astype(bf16)
                rq_out[rows, qo + LANES:qo + RET_PAIR] = jnp.where(keep, q2, 0.0).astype(bf16)
        rv_out[rows, :] = z_ref[rows, C_RV:C_RV + RET_WIDTH].astype(bf16)
        rg_out[rows, :] = z_ref[rows, C_RG:C_RG + RET_WIDTH].astype(bf16)

    n_sub = h_ref.shape[0] // sub
    subs = [slice(r * sub, (r + 1) * sub) for r in range(n_sub)]
    matmuls(subs[0])
    for r in range(n_sub):
        if r + 1 < n_sub:
            matmuls(subs[r + 1])
        rowwise(subs[r])


def _pre_call(h, p, l, tabs, tm, name):
    B, R, _ = h.shape
    row = lambda w: pl.BlockSpec((None, tm, w), lambda b, j: (b, j, 0))
    tab = pl.BlockSpec((tm, LANES), lambda b, j: (j, 0))
    in_specs = [
        row(D_MODEL),
        _layer_spec(l, (1, D_MODEL)),
        _layer_spec(l, (D_MODEL, C_RQ)),
        _layer_spec(l, (D_MODEL, C_RV - C_RQ)),
        _layer_spec(l, (D_MODEL, N_IN_PAD - C_RV)),
        _layer_spec(l, (1, Q_LORA)),
        _layer_spec(l, (Q_LORA, MLA_HEADS * QK_PAD)),
        _layer_spec(l, (1, KV_LORA)),
        _layer_spec(l, (KV_LORA, MLA_HEADS * KV_PAD)),
        _layer_spec(l, (1, QK_PAD)),
        _layer_spec(l, (1, QK_PAD)),
    ] + [tab] * 6
    widths = (MLA_HEADS * QK_PAD, MLA_HEADS * QK_PAD, MLA_WIDTH, RET_HEADS * RET_PAIR, RET_WIDTH,
              RET_WIDTH, RET_WIDTH)
    return pl.pallas_call(
        functools.partial(_pre_kernel, sub=min(PRE_SUB, tm)), grid=(B, R // tm),
        in_specs=in_specs, out_specs=[row(w) for w in widths],
        out_shape=[jax.ShapeDtypeStruct((B, R, w), bf16) for w in widths],
        scratch_shapes=[pltpu.VMEM((tm, N_IN_PAD), f32), pltpu.VMEM((tm, MLA_HEADS * QK_PAD), f32),
                        pltpu.VMEM((tm, MLA_HEADS * KV_PAD), f32)],
        name=name, compiler_params=_params(2),
    )(h, p["attn_g"], p["w_head"], p["w_qk"], p["w_vg"], p["qa_g"], p["w_qb"], p["kva_g"], p["w_kvb"],
      p["q_g"], p["k_g"], *tabs)


def _rowmax(s):
    return jnp.max(s, axis=-1, keepdims=True)


def _rowsum(p):
    return jnp.sum(p, axis=-1, keepdims=True)


def _attn_finish(acc, l, og):
    o = acc * (1.0 / l)
    return (o * _rms(o, 1.0 / V_HEAD) * og).astype(bf16)


def _meta_valid():
    return lax.broadcasted_iota(jnp.int32, (1, BLOCK), 1) >= PAD_ROWS


def _attn_kernel(q_ref, k_ref, v_ref, km_ref, vm_ref, og_ref, o_ref):
    meta_valid = _meta_valid()
    tri = (lax.broadcasted_iota(jnp.int32, (ATT_TQ, ATT_TQ), 0)
           >= lax.broadcasted_iota(jnp.int32, (ATT_TQ, ATT_TQ), 1))

    def scores(item):
        hh, i = item
        qk = slice(hh * QK_PAD, (hh + 1) * QK_PAD)
        lo, hi = i * ATT_TQ, (i + 1) * ATT_TQ
        q = q_ref[lo:hi, qk]
        s_m = jnp.where(meta_valid, _dot_nt(q, km_ref[:, qk]), NEG_INF)
        s_x = _dot_nt(q, k_ref[0:hi, qk])
        parts = [s_m] + ([s_x[:, :lo]] if lo else []) + [jnp.where(tri, s_x[:, lo:], NEG_INF)]
        return jnp.concatenate(parts, axis=1)

    items = [(hh, i) for i in ATT_ORDER for hh in range(ATT_HPS)]
    s = scores(items[0])
    for n, (hh, i) in enumerate(items):
        s_next = scores(items[n + 1]) if n + 1 < len(items) else None
        vv = slice(hh * V_HEAD, (hh + 1) * V_HEAD)
        lo, hi = i * ATT_TQ, (i + 1) * ATT_TQ
        p = jnp.exp2(s - _rowmax(s))
        acc = (_dot(p[:, :BLOCK].astype(bf16), vm_ref[:, vv])
               + _dot(p[:, BLOCK:].astype(bf16), v_ref[0:hi, vv]))
        o_ref[lo:hi, vv] = _attn_finish(acc, _rowsum(p), og_ref[:, vv])
        s = s_next


def _attn_meta_kernel(q_ref, k_ref, v_ref, og_ref, o_ref):
    mask = (lax.broadcasted_iota(jnp.int32, (BLOCK, BLOCK), 0)
            >= lax.broadcasted_iota(jnp.int32, (BLOCK, BLOCK), 1)) & _meta_valid()
    s = jnp.where(mask, _dot_nt(q_ref[...], k_ref[...]), NEG_INF)
    p = jnp.exp2(s - _rowmax(s))
    o_ref[...] = _attn_finish(_dot(p.astype(bf16), v_ref[...]), _rowsum(p), og_ref[...])


def _attn_call(q, k, v, km, vm, out_g, l):
    B = q.shape[0]
    n = ATT_HPS
    qk_spec = pl.BlockSpec((None, SEQ, n * QK_PAD), lambda b, h: (b, 0, h))
    v_spec = pl.BlockSpec((None, SEQ, n * V_HEAD), lambda b, h: (b, 0, h))
    return pl.pallas_call(
        _attn_kernel, grid=(B, MLA_HEADS // n),
        in_specs=[qk_spec, qk_spec, v_spec,
                  pl.BlockSpec((None, BLOCK, n * QK_PAD), lambda b, h: (0, 0, h)),
                  pl.BlockSpec((None, BLOCK, n * V_HEAD), lambda b, h: (0, 0, h)),
                  pl.BlockSpec((None, 1, n * V_HEAD), lambda b, h: (l, 0, h))],
        out_specs=v_spec,
        out_shape=jax.ShapeDtypeStruct((B, SEQ, MLA_WIDTH), bf16),
        name="attn", compiler_params=_params(2),
    )(q, k, v, km, vm, out_g)


def _attn_meta_call(qm, km, vm, out_g, l):
    qk_spec = pl.BlockSpec((None, BLOCK, QK_PAD), lambda h: (0, 0, h))
    v_spec = pl.BlockSpec((None, BLOCK, V_HEAD), lambda h: (0, 0, h))
    return pl.pallas_call(
        _attn_meta_kernel, grid=(MLA_HEADS,),
        in_specs=[qk_spec, qk_spec, v_spec, pl.BlockSpec((None, 1, V_HEAD), lambda h: (l, 0, h))],
        out_specs=v_spec,
        out_shape=jax.ShapeDtypeStruct((1, BLOCK, MLA_WIDTH), bf16),
        name="attn_meta", compiler_params=_params(1),
    )(qm, km, vm, out_g)


def _ret_tables(lg_ref, C):
    lg = lg_ref[0:1, :]
    lgc = jnp.concatenate([lg] * (C // LANES), axis=1)
    ri = lax.broadcasted_iota(jnp.int32, (C, C), 0)
    ci = lax.broadcasted_iota(jnp.int32, (C, C), 1)
    diff = (ri - ci).astype(f32)
    decay = jnp.where(diff >= 0, jnp.exp(jnp.maximum(diff, 0.0) * lgc), 0.0)
    idx = lax.broadcasted_iota(jnp.int32, (C, RET_HEAD), 0).astype(f32)
    xi = jnp.exp((idx + 1.0) * lg)
    return lg, decay, idx, xi


def _ret_key_state(k, v, idx, lg):
    n = k.shape[0]
    zeta = jnp.exp((n - 1.0 - idx[:n]) * lg)
    vz = (v.astype(f32) * zeta).astype(bf16)
    return _dot_tn(vz, k)


def _ret_mix(q, k, v, decay, state_t, xi):
    n = q.shape[0]
    o = _dot((_dot_nt(q, k) * decay[:n, :n]).astype(bf16), v)
    if state_t is not None:
        o = o + _dot_nt(q, state_t.astype(bf16)) * xi[:n]
    return o


def _ret_emit(o, g, ng, nb):
    mu = jnp.mean(o, axis=-1, keepdims=True)
    d = o - mu
    var = jnp.mean(d * d, axis=-1, keepdims=True)
    on = d * lax.rsqrt(var + EPS) * ng + nb
    g = g.astype(f32)
    return (g * _sigmoid(g) * on).astype(bf16)


def _ret_kernel(q_ref, k_ref, v_ref, g_ref, km_ref, vm_ref, lg_ref, ng_ref, nb_ref, o_ref):
    C = RET_CHUNK
    n_chunks = SEQ // C
    chunks = [slice(c * C, (c + 1) * C) for c in range(n_chunks)]
    heads = []
    for e in range(2):
        lg, decay, idx, xi = _ret_tables(lg_ref.at[e], C)
        hv = slice(e * RET_HEAD, (e + 1) * RET_HEAD)
        heads.append(dict(
            lg=lg, decay=decay, idx=idx, xi=xi, hv=hv, hq=slice(e * RET_PAIR, (e + 1) * RET_PAIR),
            chunk_decay=jnp.exp(float(C) * jnp.concatenate([lg, lg], axis=1)),
            state_t=_ret_key_state(km_ref[...], vm_ref[:, hv], idx, lg), o_prev=None))
    for c, rows in enumerate(chunks):
        for hd in heads:
            hv = hd["hv"]
            o = _ret_mix(q_ref[rows, hd["hq"]], k_ref[rows, :], v_ref[rows, hv], hd["decay"],
                         hd["state_t"], hd["xi"])
            if c + 1 < n_chunks:
                hd["state_t"] = hd["state_t"] * hd["chunk_decay"] + _ret_key_state(
                    k_ref[rows, :], v_ref[rows, hv], hd["idx"], hd["lg"])
            if hd["o_prev"] is not None:
                o_ref[chunks[c - 1], hv] = _ret_emit(hd["o_prev"], g_ref[chunks[c - 1], hv],
                                                     ng_ref[:, hv], nb_ref[:, hv])
            hd["o_prev"] = o
    for hd in heads:
        hv = hd["hv"]
        o_ref[chunks[-1], hv] = _ret_emit(hd["o_prev"], g_ref[chunks[-1], hv], ng_ref[:, hv],
                                          nb_ref[:, hv])


def _ret_meta_kernel(q_ref, k_ref, v_ref, g_ref, lg_ref, ng_ref, nb_ref, o_ref):
    _, decay, _, _ = _ret_tables(lg_ref, BLOCK)
    o = _ret_mix(q_ref[...], k_ref[...], v_ref[...], decay, None, None)
    o_ref[...] = _ret_emit(o, g_ref[...], ng_ref[...], nb_ref[...])


def _ret_call(rq, rk, rv, rg, rkm, rvm, log_g, norm_g, norm_b, l):
    B = rq.shape[0]
    spec = pl.BlockSpec((None, SEQ, 2 * RET_HEAD), lambda b, p: (b, 0, p))
    vec = pl.BlockSpec((None, 1, 2 * RET_HEAD), lambda b, p: (l, 0, p))
    return pl.pallas_call(
        _ret_kernel, grid=(B, RET_HEADS // 2),
        in_specs=[pl.BlockSpec((None, SEQ, 2 * RET_PAIR), lambda b, p: (b, 0, p)),
                  pl.BlockSpec((None, SEQ, RET_PAIR), lambda b, p: (b, 0, p)),
                  spec, spec,
                  pl.BlockSpec((None, BLOCK, RET_PAIR), lambda b, p: (0, 0, p)),
                  pl.BlockSpec((None, BLOCK, 2 * RET_HEAD), lambda b, p: (0, 0, p)),
                  pl.BlockSpec((2, 8, LANES), lambda b, p: (p, 0, 0)), vec, vec],
        out_specs=spec,
        out_shape=jax.ShapeDtypeStruct((B, SEQ, RET_WIDTH), bf16),
        name="ret", compiler_params=_params(2),
    )(rq, rk, rv, rg, rkm, rvm, log_g, norm_g, norm_b)


def _ret_meta_call(rqm, rkm, rvm, rgm, log_g, norm_g, norm_b, l):
    spec = pl.BlockSpec((None, BLOCK, RET_HEAD), lambda h: (0, 0, h))
    vec = pl.BlockSpec((None, 1, RET_HEAD), lambda h: (l, 0, h))
    return pl.pallas_call(
        _ret_meta_kernel, grid=(RET_HEADS,),
        in_specs=[pl.BlockSpec((None, BLOCK, RET_PAIR), lambda h: (0, 0, h)),
                  pl.BlockSpec((None, BLOCK, RET_PAIR), lambda h: (0, 0, h // 2)),
                  spec, spec, pl.BlockSpec((None, 8, LANES), lambda h: (h, 0, 0)), vec, vec],
        out_specs=spec,
        out_shape=jax.ShapeDtypeStruct((1, BLOCK, RET_WIDTH), bf16),
        name="ret_meta", compiler_params=_params(1),
    )(rqm, rkm, rvm, rgm, log_g, norm_g, norm_b)


def _post_kernel(h_ref, ym_ref, yr_ref, wo_ref, fg_ref, wgu_ref, wd_ref, o_ref, act_ref):
    h1 = (h_ref[...] + _dot(ym_ref[...], wo_ref[0:MLA_WIDTH, :])
          + _dot(yr_ref[...], wo_ref[MLA_WIDTH:MLA_WIDTH + RET_WIDTH, :]))
    hf = (h1 * _rms(h1, 1.0 / D_MODEL) * fg_ref[...]).astype(bf16)
    for c in range(D_FF // FF_TILE):
        lo = c * FF_TILE
        gate = _dot(hf, wgu_ref[:, lo:lo + FF_TILE])
        up = _dot(hf, wgu_ref[:, D_FF + lo:D_FF + lo + FF_TILE])
        act_ref[:, lo:lo + FF_TILE] = (gate * _sigmoid(gate) * up).astype(bf16)
    o_ref[...] = h1 + _dot(act_ref[...], wd_ref[...])


def _post_call(h, ym, yr, p, l, tm, name):
    B, R, _ = h.shape
    row = lambda w: pl.BlockSpec((None, tm, w), lambda b, j: (b, j, 0))
    return pl.pallas_call(
        _post_kernel, grid=(B, R // tm),
        in_specs=[row(D_MODEL), row(MLA_WIDTH), row(RET_WIDTH),
                  _layer_spec(l, (MLA_WIDTH + RET_WIDTH, D_MODEL)),
                  _layer_spec(l, (1, D_MODEL)),
                  _layer_spec(l, (D_MODEL, 2 * D_FF)),
                  _layer_spec(l, (D_FF, D_MODEL))],
        out_specs=row(D_MODEL),
        out_shape=jax.ShapeDtypeStruct((B, R, D_MODEL), f32),
        scratch_shapes=[pltpu.VMEM((tm, D_FF), bf16)],
        name=name, compiler_params=_params(2),
    )(h, ym, yr, p["w_out"], p["ffn_g"], p["w_gu"], p["w_down"])


def _rope_tables(pos, valid):
    pos = pos.astype(np.float32)
    n = pos.shape[0]

    def cs(dim):
        inv = np.float32(ROPE_BASE) ** (-np.arange(0, dim, 2, dtype=np.float32) / np.float32(dim))
        ang = pos[:, None] * inv[None, :].astype(np.float32)
        return np.cos(ang).astype(np.float32), np.sin(ang).astype(np.float32)

    cm, sm = cs(QK_ROPE)
    z32 = np.zeros_like(cm)
    ca = np.concatenate([np.ones((n, HALF), np.float32), cm, z32], axis=1)
    sa = np.concatenate([np.zeros((n, HALF), np.float32), sm, z32], axis=1)
    cr, sr = cs(RET_HEAD)
    c2 = np.concatenate([cr, cr], axis=1)
    s2 = np.concatenate([sr, sr], axis=1)
    kscale = valid.astype(np.float32)[:, None] * np.float32(RET_HEAD ** -0.5)
    return tuple(jnp.asarray(t) for t in (ca, sa, c2, s2, c2 * kscale, s2 * kscale))


def _mla_head_layout(a):
    q = QK_ROPE // 2
    z = jnp.zeros(a.shape[:-1] + (q,), a.dtype)
    return jnp.concatenate([a[..., :HALF], a[..., QK_NOPE:QK_NOPE + q], z,
                            a[..., HALF:QK_NOPE], a[..., QK_NOPE + q:], z], axis=-1)


def _kpe_layout(a):
    q = QK_ROPE // 2
    z64 = jnp.zeros(a.shape[:-1] + (HALF,), a.dtype)
    z32 = jnp.zeros(a.shape[:-1] + (q,), a.dtype)
    return jnp.concatenate([z64, a[..., :q], z32, z64, a[..., q:], z32], axis=-1)


def _pair_layout(a):
    lead = a.shape[:-1]
    a = a.reshape(lead + (RET_HEADS // 2, 2, 2, HALF))
    a = jnp.swapaxes(a, -3, -2)
    return a.reshape(lead + (RET_WIDTH,))


def _prep_params(attn_norm_g, w_in, q_a_norm_g, w_q_b, kv_a_norm_g, w_kv_b, q_norm_g, k_norm_g,
                 mla_out_norm_g, ret_norm_g, ret_norm_b, w_out, ffn_norm_g, w_gate_up, w_down):
    depth = w_in.shape[0]
    o_kpe = Q_LORA + KV_LORA
    o_rq = o_kpe + QK_ROPE
    o_rk = o_rq + RET_WIDTH
    o_rv = o_rk + RET_WIDTH
    w_head = jnp.concatenate([w_in[..., :o_kpe], _kpe_layout(w_in[..., o_kpe:o_rq])],
                             axis=-1).astype(bf16)
    w_qk = jnp.concatenate([_pair_layout(w_in[..., o_rq:o_rk]), _pair_layout(w_in[..., o_rk:o_rv])],
                           axis=-1).astype(bf16)
    w_vg = w_in[..., o_rv:].astype(bf16)
    wq = _mla_head_layout(w_q_b.astype(bf16).reshape(depth, Q_LORA, MLA_HEADS, QK_HEAD))
    wkv = w_kv_b.astype(bf16).reshape(depth, KV_LORA, MLA_HEADS, QK_NOPE + V_HEAD)
    z64 = jnp.zeros((depth, KV_LORA, MLA_HEADS, HALF), bf16)
    wkv = jnp.concatenate([wkv[..., :HALF], z64, wkv[..., HALF:QK_NOPE], z64, wkv[..., QK_NOPE:]],
                          axis=-1)
    scale = QK_HEAD ** -0.5 * LOG2_E
    vec = lambda a: a[:, None, :]
    return {
        "attn_g": vec(attn_norm_g),
        "w_head": w_head,
        "w_qk": w_qk,
        "w_vg": w_vg,
        "qa_g": vec(q_a_norm_g),
        "w_qb": wq.reshape(depth, Q_LORA, MLA_HEADS * QK_PAD),
        "kva_g": vec(kv_a_norm_g),
        "w_kvb": wkv.reshape(depth, KV_LORA, MLA_HEADS * KV_PAD),
        "q_g": vec(_mla_head_layout(q_norm_g * scale)),
        "k_g": vec(_mla_head_layout(k_norm_g)),
        "out_g": vec(mla_out_norm_g),
        "ret_g": vec(ret_norm_g),
        "ret_b": vec(ret_norm_b),
        "w_out": w_out.astype(bf16),
        "ffn_g": vec(ffn_norm_g),
        "w_gu": w_gate_up.astype(bf16),
        "w_down": w_down.astype(bf16),
    }


def kernel(x, meta_tokens, attn_norm_g, w_in, q_a_norm_g, w_q_b, kv_a_norm_g, w_kv_b, q_norm_g,
           k_norm_g, mla_out_norm_g, ret_norm_g, ret_norm_b, w_out, ffn_norm_g, w_gate_up, w_down):
    depth = w_in.shape[0]
    hx = x
    hm = jnp.concatenate([jnp.zeros((PAD_ROWS, D_MODEL), x.dtype), meta_tokens.astype(x.dtype)])[None]
    r = np.arange(BLOCK)
    tabs_x = _rope_tables(np.arange(SEQ) + N_META, np.ones(SEQ))
    tabs_m = _rope_tables(np.maximum(r - PAD_ROWS, 0), r >= PAD_ROWS)
    gamma = np.float32(1.0) - np.float32(2.0) ** (np.float32(-5.0) - np.arange(RET_HEADS, dtype=np.float32))
    log_g = jnp.asarray(np.broadcast_to(np.log(gamma)[:, None, None], (RET_HEADS, 8, LANES)))
    p = _prep_params(attn_norm_g, w_in, q_a_norm_g, w_q_b, kv_a_norm_g, w_kv_b, q_norm_g, k_norm_g,
                     mla_out_norm_g, ret_norm_g, ret_norm_b, w_out, ffn_norm_g, w_gate_up, w_down)
    for l in range(depth):
        q, k, v, rq, rk, rv, rg = _pre_call(hx, p, l, tabs_x, ROW_TM, "pre")
        qm, km, vm, rqm, rkm, rvm, rgm = _pre_call(hm, p, l, tabs_m, BLOCK, "pre_meta")
        y_mla = _attn_call(q, k, v, km, vm, p["out_g"], l)
        y_ret = _ret_call(rq, rk, rv, rg, rkm, rvm, log_g, p["ret_g"], p["ret_b"], l)
        hx = _post_call(hx, y_mla, y_ret, p, l, ROW_TM, "post")
        if l + 1 < depth:
            ym_mla = _attn_meta_call(qm, km, vm, p["out_g"], l)
            ym_ret = _ret_meta_call(rqm, rkm, rvm, rgm, log_g, p["ret_g"], p["ret_b"], l)
            hm = _post_call(hm, ym_mla, ym_ret, p, l, BLOCK, "post_meta")
    return hx
```

```python
import functools

import numpy as np

import jax
import jax.numpy as jnp
from jax import lax
from jax.experimental import pallas as pl
from jax.experimental.pallas import tpu as pltpu

D_MODEL = 1024
SEQ = 2048
N_META = 16
BLOCK = 128
MLA_HEADS = 4
Q_LORA = 256
KV_LORA = 256
QK_NOPE = 128
QK_ROPE = 64
QK_HEAD = QK_NOPE + QK_ROPE
V_HEAD = 128
MLA_WIDTH = MLA_HEADS * V_HEAD
RET_HEADS = 4
RET_HEAD = 128
RET_WIDTH = RET_HEADS * RET_HEAD
D_FF = 2816
ROPE_BASE = 10000.0
EPS = 1e-6
NEG_INF = -1e30

LANES = 128
HALF = LANES // 2
PAD_ROWS = BLOCK - N_META
QK_PAD = 2 * LANES
KV_PAD = 3 * LANES
RET_PAIR = 2 * RET_HEAD

C_CQ = 0
C_CKV = C_CQ + Q_LORA
C_KPE = C_CKV + KV_LORA
C_RQ = C_KPE + 2 * LANES
C_RK = C_RQ + RET_WIDTH
C_RV = C_RK + RET_WIDTH
C_RG = C_RV + RET_WIDTH
N_IN_PAD = C_RG + RET_WIDTH

ROW_TM = 512
PRE_SUB = 256
ATT_TQ = 512
ATT_ORDER = (0, 1, 2, 3)
ATT_HPS = 4
ATT_AHEAD = 2
RET_CHUNK = 256
FF_TILE = 256
LOG2_E = 1.4426950408889634
VMEM_LIMIT = 56 * 1024 * 1024

f32 = jnp.float32
bf16 = jnp.bfloat16


def _dot(a, b):
    return jnp.dot(a, b, preferred_element_type=f32)


def _dot_nt(a, b):
    return lax.dot_general(a, b, (((1,), (1,)), ((), ())), preferred_element_type=f32)


def _dot_tn(a, b):
    return lax.dot_general(a, b, (((0,), (0,)), ((), ())), preferred_element_type=f32)


def _rms(x, inv_n):
    return lax.rsqrt(jnp.sum(x * x, axis=-1, keepdims=True) * inv_n + EPS)


def _sigmoid(x):
    return 1.0 / (1.0 + jnp.exp(-x))


def _rot(a, b, c, s):
    return a * c - b * s, b * c + a * s


def _params(n_grid_axes):
    return pltpu.CompilerParams(dimension_semantics=("parallel",) * n_grid_axes,
                                vmem_limit_bytes=VMEM_LIMIT)


def _layer_spec(l, shape):
    nd = len(shape)
    return pl.BlockSpec((None,) + shape, lambda *_: (l,) + (0,) * nd, pipeline_mode=pl.Buffered(1))


def _pre_kernel(h_ref, g_ref, whead_ref, wqk_ref, wvg_ref, qag_ref, wqb_ref, kvag_ref, wkvb_ref,
                qg_ref, kg_ref, ca_ref, sa_ref, c2_ref, s2_ref, ck_ref, sk_ref,
                q_out, k_out, v_out, rq_out, rk_out, rv_out, rg_out, z_ref, q_ref, kv_ref, *, sub):
    qga, qgb = qg_ref[:, :LANES], qg_ref[:, LANES:]
    kga, kgb = kg_ref[:, :LANES], kg_ref[:, LANES:]
    first = lax.broadcasted_iota(jnp.int32, (1, LANES), 1) < HALF

    def matmuls(rows):
        x = h_ref[rows, :]
        hb = (x * _rms(x, 1.0 / D_MODEL) * g_ref[...]).astype(bf16)
        z_ref[rows, :C_RQ] = _dot(hb, whead_ref[...])
        z_ref[rows, C_RQ:C_RV] = _dot(hb, wqk_ref[...])
        z_ref[rows, C_RV:] = _dot(hb, wvg_ref[...])
        cq = z_ref[rows, C_CQ:C_CQ + Q_LORA]
        cqn = (cq * _rms(cq, 1.0 / Q_LORA) * qag_ref[...]).astype(bf16)
        q_ref[rows, :] = _dot(cqn, wqb_ref[...])
        ckv = z_ref[rows, C_CKV:C_CKV + KV_LORA]
        ckvn = (ckv * _rms(ckv, 1.0 / KV_LORA) * kvag_ref[...]).astype(bf16)
        kv_ref[rows, :] = _dot(ckvn, wkvb_ref[...])

    def rowwise(rows):
        ca, sa = ca_ref[rows, :], sa_ref[rows, :]

        def norm_rope(a, b, ga, gb):
            r = lax.rsqrt(jnp.sum(a * a + b * b, axis=-1, keepdims=True) * (1.0 / QK_HEAD) + EPS)
            ao, bo = _rot(a * ga, b * gb, ca, sa)
            return (ao * r).astype(bf16), (bo * r).astype(bf16)

        kpa = z_ref[rows, C_KPE:C_KPE + LANES]
        kpb = z_ref[rows, C_KPE + LANES:C_KPE + 2 * LANES]
        for h in range(MLA_HEADS):
            lo = h * QK_PAD
            q_out[rows, lo:lo + LANES], q_out[rows, lo + LANES:lo + QK_PAD] = norm_rope(
                q_ref[rows, lo:lo + LANES], q_ref[rows, lo + LANES:lo + QK_PAD], qga, qgb)
            kl = h * KV_PAD
            k_out[rows, lo:lo + LANES], k_out[rows, lo + LANES:lo + QK_PAD] = norm_rope(
                kv_ref[rows, kl:kl + LANES] + kpa, kv_ref[rows, kl + LANES:kl + 2 * LANES] + kpb,
                kga, kgb)
            v_out[rows, h * V_HEAD:(h + 1) * V_HEAD] = (
                kv_ref[rows, kl + 2 * LANES:kl + KV_PAD].astype(bf16))

        c2, s2, ck, sk = c2_ref[rows, :], s2_ref[rows, :], ck_ref[rows, :], sk_ref[rows, :]
        for p in range(RET_HEADS // 2):
            lo = p * RET_PAIR
            k1, k2 = _rot(z_ref[rows, C_RK + lo:C_RK + lo + LANES],
                          z_ref[rows, C_RK + lo + LANES:C_RK + lo + RET_PAIR], ck, sk)
            rk_out[rows, lo:lo + LANES] = k1.astype(bf16)
            rk_out[rows, lo + LANES:lo + RET_PAIR] = k2.astype(bf16)
            q1, q2 = _rot(z_ref[rows, C_RQ + lo:C_RQ + lo + LANES],
                          z_ref[rows, C_RQ + lo + LANES:C_RQ + lo + RET_PAIR], c2, s2)
            for e in range(2):
                keep = first if e == 0 else jnp.logical_not(first)
                qo = (2 * p + e) * RET_PAIR
                rq_out[rows, qo:qo + LANES] = jnp.where(keep, q1, 0.0).astype(bf16)
                rq_out[rows, qo + LANES:qo + RET_PAIR] = jnp.where(keep, q2, 0.0).astype(bf16)
        rv_out[rows, :] = z_ref[rows, C_RV:C_RV + RET_WIDTH].astype(bf16)
        rg_out[rows, :] = z_ref[rows, C_RG:C_RG + RET_WIDTH].astype(bf16)

    n_sub = h_ref.shape[0] // sub
    subs = [slice(r * sub, (r + 1) * sub) for r in range(n_sub)]
    matmuls(subs[0])
    for r in range(n_sub):
        if r + 1 < n_sub:
            matmuls(subs[r + 1])
        rowwise(subs[r])


def _pre_call(h, p, l, tabs, tm, name):
    B, R, _ = h.shape
    row = lambda w: pl.BlockSpec((None, tm, w), lambda b, j: (b, j, 0))
    tab = pl.BlockSpec((tm, LANES), lambda b, j: (j, 0))
    in_specs = [
        row(D_MODEL),
        _layer_spec(l, (1, D_MODEL)),
        _layer_spec(l, (D_MODEL, C_RQ)),
        _layer_spec(l, (D_MODEL, C_RV - C_RQ)),
        _layer_spec(l, (D_MODEL, N_IN_PAD - C_RV)),
        _layer_spec(l, (1, Q_LORA)),
        _layer_spec(l, (Q_LORA, MLA_HEADS * QK_PAD)),
        _layer_spec(l, (1, KV_LORA)),
        _layer_spec(l, (KV_LORA, MLA_HEADS * KV_PAD)),
        _layer_spec(l, (1, QK_PAD)),
        _layer_spec(l, (1, QK_PAD)),
    ] + [tab] * 6
    widths = (MLA_HEADS * QK_PAD, MLA_HEADS * QK_PAD, MLA_WIDTH, RET_HEADS * RET_PAIR, RET_WIDTH,
              RET_WIDTH, RET_WIDTH)
    return pl.pallas_call(
        functools.partial(_pre_kernel, sub=min(PRE_SUB, tm)), grid=(B, R // tm),
        in_specs=in_specs, out_specs=[row(w) for w in widths],
        out_shape=[jax.ShapeDtypeStruct((B, R, w), bf16) for w in widths],
        scratch_shapes=[pltpu.VMEM((tm, N_IN_PAD), f32), pltpu.VMEM((tm, MLA_HEADS * QK_PAD), f32),
                        pltpu.VMEM((tm, MLA_HEADS * KV_PAD), f32)],
        name=name, compiler_params=_params(2),
    )(h, p["attn_g"], p["w_head"], p["w_qk"], p["w_vg"], p["qa_g"], p["w_qb"], p["kva_g"], p["w_kvb"],
      p["q_g"], p["k_g"], *tabs)


def _rowmax(s):
    return jnp.max(s, axis=-1, keepdims=True)


def _rowsum(p):
    return jnp.sum(p, axis=-1, keepdims=True)


def _attn_finish(acc, l, og):
    o = acc * (1.0 / l)
    return (o * _rms(o, 1.0 / V_HEAD) * og).astype(bf16)


def _meta_valid():
    return lax.broadcasted_iota(jnp.int32, (1, BLOCK), 1) >= PAD_ROWS


def _attn_kernel(q_ref, k_ref, v_ref, km_ref, vm_ref, og_ref, o_ref, vt_ref):
    for hh in range(ATT_HPS):
        vv = slice(hh * V_HEAD, (hh + 1) * V_HEAD)
        vt_ref[vv, 0:BLOCK] = vm_ref[:, vv].T
        vt_ref[vv, BLOCK:] = v_ref[:, vv].T
    key_valid = lax.broadcasted_iota(jnp.int32, (BLOCK, 1), 0) >= PAD_ROWS
    tri_t = (lax.broadcasted_iota(jnp.int32, (ATT_TQ, ATT_TQ), 0)
             <= lax.broadcasted_iota(jnp.int32, (ATT_TQ, ATT_TQ), 1))

    def scores(item):
        hh, i = item
        qk = slice(hh * QK_PAD, (hh + 1) * QK_PAD)
        lo, hi = i * ATT_TQ, (i + 1) * ATT_TQ
        q = q_ref[lo:hi, qk]
        s_m = jnp.where(key_valid, _dot_nt(km_ref[:, qk], q), NEG_INF)
        s_x = _dot_nt(k_ref[0:hi, qk], q)
        parts = [s_m] + ([s_x[:lo]] if lo else []) + [jnp.where(tri_t, s_x[lo:], NEG_INF)]
        return jnp.concatenate(parts, axis=0)

    items = [(hh, i) for i in ATT_ORDER for hh in range(ATT_HPS)]
    ahead = [scores(it) for it in items[:ATT_AHEAD]]
    for n, (hh, i) in enumerate(items):
        if n + ATT_AHEAD < len(items):
            ahead.append(scores(items[n + ATT_AHEAD]))
        s = ahead.pop(0)
        vv = slice(hh * V_HEAD, (hh + 1) * V_HEAD)
        lo, hi = i * ATT_TQ, (i + 1) * ATT_TQ
        p = jnp.exp2(s - jnp.max(s, axis=0, keepdims=True))
        l = jnp.sum(p, axis=0, keepdims=True)
        o_t = _dot(vt_ref[vv, 0:BLOCK + hi], p.astype(bf16)) * (1.0 / l)
        r = lax.rsqrt(jnp.sum(o_t * o_t, axis=0, keepdims=True) * (1.0 / V_HEAD) + EPS)
        o_ref[lo:hi, vv] = ((o_t * r).T * og_ref[:, vv]).astype(bf16)


def _attn_meta_kernel(q_ref, k_ref, v_ref, og_ref, o_ref):
    mask = (lax.broadcasted_iota(jnp.int32, (BLOCK, BLOCK), 0)
            >= lax.broadcasted_iota(jnp.int32, (BLOCK, BLOCK), 1)) & _meta_valid()
    s = jnp.where(mask, _dot_nt(q_ref[...], k_ref[...]), NEG_INF)
    p = jnp.exp2(s - _rowmax(s))
    o_ref[...] = _attn_finish(_dot(p.astype(bf16), v_ref[...]), _rowsum(p), og_ref[...])


def _attn_call(q, k, v, km, vm, out_g, l):
    B = q.shape[0]
    n = ATT_HPS
    qk_spec = pl.BlockSpec((None, SEQ, n * QK_PAD), lambda b, h: (b, 0, h))
    v_spec = pl.BlockSpec((None, SEQ, n * V_HEAD), lambda b, h: (b, 0, h))
    return pl.pallas_call(
        _attn_kernel, grid=(B, MLA_HEADS // n),
        in_specs=[qk_spec, qk_spec, v_spec,
                  pl.BlockSpec((None, BLOCK, n * QK_PAD), lambda b, h: (0, 0, h)),
                  pl.BlockSpec((None, BLOCK, n * V_HEAD), lambda b, h: (0, 0, h)),
                  pl.BlockSpec((None, 1, n * V_HEAD), lambda b, h: (l, 0, h))],
        out_specs=v_spec,
        out_shape=jax.ShapeDtypeStruct((B, SEQ, MLA_WIDTH), bf16),
        scratch_shapes=[pltpu.VMEM((n * V_HEAD, BLOCK + SEQ), bf16)],
        name="attn", compiler_params=_params(2),
    )(q, k, v, km, vm, out_g)


def _attn_meta_call(qm, km, vm, out_g, l):
    qk_spec = pl.BlockSpec((None, BLOCK, QK_PAD), lambda h: (0, 0, h))
    v_spec = pl.BlockSpec((None, BLOCK, V_HEAD), lambda h: (0, 0, h))
    return pl.pallas_call(
        _attn_meta_kernel, grid=(MLA_HEADS,),
        in_specs=[qk_spec, qk_spec, v_spec, pl.BlockSpec((None, 1, V_HEAD), lambda h: (l, 0, h))],
        out_specs=v_spec,
        out_shape=jax.ShapeDtypeStruct((1, BLOCK, MLA_WIDTH), bf16),
        name="attn_meta", compiler_params=_params(1),
    )(qm, km, vm, out_g)


def _ret_tables(lg_ref, C):
    lg = lg_ref[0:1, :]
    lgc = jnp.concatenate([lg] * (C // LANES), axis=1)
    ri = lax.broadcasted_iota(jnp.int32, (C, C), 0)
    ci = lax.broadcasted_iota(jnp.int32, (C, C), 1)
    diff = (ri - ci).astype(f32)
    decay = jnp.where(diff >= 0, jnp.exp(jnp.maximum(diff, 0.0) * lgc), 0.0)
    idx = lax.broadcasted_iota(jnp.int32, (C, RET_HEAD), 0).astype(f32)
    xi = jnp.exp((idx + 1.0) * lg)
    return lg, decay, idx, xi


def _ret_key_state(k, v, idx, lg):
    n = k.shape[0]
    zeta = jnp.exp((n - 1.0 - idx[:n]) * lg)
    vz = (v.astype(f32) * zeta).astype(bf16)
    return _dot_tn(vz, k)


def _ret_mix(q, k, v, decay, state_t, xi):
    n = q.shape[0]
    o = _dot((_dot_nt(q, k) * decay[:n, :n]).astype(bf16), v)
    if state_t is not None:
        o = o + _dot_nt(q, state_t.astype(bf16)) * xi[:n]
    return o


def _ret_emit(o, g, ng, nb):
    mu = jnp.mean(o, axis=-1, keepdims=True)
    d = o - mu
    var = jnp.mean(d * d, axis=-1, keepdims=True)
    on = d * lax.rsqrt(var + EPS) * ng + nb
    g = g.astype(f32)
    return (g * _sigmoid(g) * on).astype(bf16)


def _ret_kernel(q_ref, k_ref, v_ref, g_ref, km_ref, vm_ref, lg_ref, ng_ref, nb_ref, o_ref):
    C = RET_CHUNK
    n_chunks = SEQ // C
    chunks = [slice(c * C, (c + 1) * C) for c in range(n_chunks)]
    heads = []
    for e in range(2):
        lg, decay, idx, xi = _ret_tables(lg_ref.at[e], C)
        hv = slice(e * RET_HEAD, (e + 1) * RET_HEAD)
        heads.append(dict(
            lg=lg, decay=decay, idx=idx, xi=xi, hv=hv, hq=slice(e * RET_PAIR, (e + 1) * RET_PAIR),
            chunk_decay=jnp.exp(float(C) * jnp.concatenate([lg, lg], axis=1)),
            state_t=_ret_key_state(km_ref[...], vm_ref[:, hv], idx, lg), o_prev=None))
    for c, rows in enumerate(chunks):
        for hd in heads:
            hv = hd["hv"]
            o = _ret_mix(q_ref[rows, hd["hq"]], k_ref[rows, :], v_ref[rows, hv], hd["decay"],
                         hd["state_t"], hd["xi"])
            if c + 1 < n_chunks:
                hd["state_t"] = hd["state_t"] * hd["chunk_decay"] + _ret_key_state(
                    k_ref[rows, :], v_ref[rows, hv], hd["idx"], hd["lg"])
            if hd["o_prev"] is not None:
                o_ref[chunks[c - 1], hv] = _ret_emit(hd["o_prev"], g_ref[chunks[c - 1], hv],
                                                     ng_ref[:, hv], nb_ref[:, hv])
            hd["o_prev"] = o
    for hd in heads:
        hv = hd["hv"]
        o_ref[chunks[-1], hv] = _ret_emit(hd["o_prev"], g_ref[chunks[-1], hv], ng_ref[:, hv],
                                          nb_ref[:, hv])


def _ret_meta_kernel(q_ref, k_ref, v_ref, g_ref, lg_ref, ng_ref, nb_ref, o_ref):
    _, decay, _, _ = _ret_tables(lg_ref, BLOCK)
    o = _ret_mix(q_ref[...], k_ref[...], v_ref[...], decay, None, None)
    o_ref[...] = _ret_emit(o, g_ref[...], ng_ref[...], nb_ref[...])


def _ret_call(rq, rk, rv, rg, rkm, rvm, log_g, norm_g, norm_b, l):
    B = rq.shape[0]
    spec = pl.BlockSpec((None, SEQ, 2 * RET_HEAD), lambda b, p: (b, 0, p))
    vec = pl.BlockSpec((None, 1, 2 * RET_HEAD), lambda b, p: (l, 0, p))
    return pl.pallas_call(
        _ret_kernel, grid=(B, RET_HEADS // 2),
        in_specs=[pl.BlockSpec((None, SEQ, 2 * RET_PAIR), lambda b, p: (b, 0, p)),
                  pl.BlockSpec((None, SEQ, RET_PAIR), lambda b, p: (b, 0, p)),
                  spec, spec,
                  pl.BlockSpec((None, BLOCK, RET_PAIR), lambda b, p: (0, 0, p)),
                  pl.BlockSpec((None, BLOCK, 2 * RET_HEAD), lambda b, p: (0, 0, p)),
                  pl.BlockSpec((2, 8, LANES), lambda b, p: (p, 0, 0)), vec, vec],
        out_specs=spec,
        out_shape=jax.ShapeDtypeStruct((B, SEQ, RET_WIDTH), bf16),
        name="ret", compiler_params=_params(2),
    )(rq, rk, rv, rg, rkm, rvm, log_g, norm_g, norm_b)


def _ret_meta_call(rqm, rkm, rvm, rgm, log_g, norm_g, norm_b, l):
    spec = pl.BlockSpec((None, BLOCK, RET_HEAD), lambda h: (0, 0, h))
    vec = pl.BlockSpec((None, 1, RET_HEAD), lambda h: (l, 0, h))
    return pl.pallas_call(
        _ret_meta_kernel, grid=(RET_HEADS,),
        in_specs=[pl.BlockSpec((None, BLOCK, RET_PAIR), lambda h: (0, 0, h)),
                  pl.BlockSpec((None, BLOCK, RET_PAIR), lambda h: (0, 0, h // 2)),
                  spec, spec, pl.BlockSpec((None, 8, LANES), lambda h: (h, 0, 0)), vec, vec],
        out_specs=spec,
        out_shape=jax.ShapeDtypeStruct((1, BLOCK, RET_WIDTH), bf16),
        name="ret_meta", compiler_params=_params(1),
    )(rqm, rkm, rvm, rgm, log_g, norm_g, norm_b)


def _post_kernel(h_ref, ym_ref, yr_ref, wo_ref, fg_ref, wgu_ref, wd_ref, o_ref, act_ref):
    h1 = (h_ref[...] + _dot(ym_ref[...], wo_ref[0:MLA_WIDTH, :])
          + _dot(yr_ref[...], wo_ref[MLA_WIDTH:MLA_WIDTH + RET_WIDTH, :]))
    hf = (h1 * _rms(h1, 1.0 / D_MODEL) * fg_ref[...]).astype(bf16)
    for c in range(D_FF // FF_TILE):
        lo = c * FF_TILE
        gate = _dot(hf, wgu_ref[:, lo:lo + FF_TILE])
        up = _dot(hf, wgu_ref[:, D_FF + lo:D_FF + lo + FF_TILE])
        act_ref[:, lo:lo + FF_TILE] = (gate * _sigmoid(gate) * up).astype(bf16)
    o_ref[...] = h1 + _dot(act_ref[...], wd_ref[...])


def _post_call(h, ym, yr, p, l, tm, name):
    B, R, _ = h.shape
    row = lambda w: pl.BlockSpec((None, tm, w), lambda b, j: (b, j, 0))
    return pl.pallas_call(
        _post_kernel, grid=(B, R // tm),
        in_specs=[row(D_MODEL), row(MLA_WIDTH), row(RET_WIDTH),
                  _layer_spec(l, (MLA_WIDTH + RET_WIDTH, D_MODEL)),
                  _layer_spec(l, (1, D_MODEL)),
                  _layer_spec(l, (D_MODEL, 2 * D_FF)),
                  _layer_spec(l, (D_FF, D_MODEL))],
        out_specs=row(D_MODEL),
        out_shape=jax.ShapeDtypeStruct((B, R, D_MODEL), f32),
        scratch_shapes=[pltpu.VMEM((tm, D_FF), bf16)],
        name=name, compiler_params=_params(2),
    )(h, ym, yr, p["w_out"], p["ffn_g"], p["w_gu"], p["w_down"])


def _rope_tables(pos, valid):
    pos = pos.astype(np.float32)
    n = pos.shape[0]

    def cs(dim):
        inv = np.float32(ROPE_BASE) ** (-np.arange(0, dim, 2, dtype=np.float32) / np.float32(dim))
        ang = pos[:, None] * inv[None, :].astype(np.float32)
        return np.cos(ang).astype(np.float32), np.sin(ang).astype(np.float32)

    cm, sm = cs(QK_ROPE)
    z32 = np.zeros_like(cm)
    ca = np.concatenate([np.ones((n, HALF), np.float32), cm, z32], axis=1)
    sa = np.concatenate([np.zeros((n, HALF), np.float32), sm, z32], axis=1)
    cr, sr = cs(RET_HEAD)
    c2 = np.concatenate([cr, cr], axis=1)
    s2 = np.concatenate([sr, sr], axis=1)
    kscale = valid.astype(np.float32)[:, None] * np.float32(RET_HEAD ** -0.5)
    return tuple(jnp.asarray(t) for t in (ca, sa, c2, s2, c2 * kscale, s2 * kscale))


def _mla_head_layout(a):
    q = QK_ROPE // 2
    z = jnp.zeros(a.shape[:-1] + (q,), a.dtype)
    return jnp.concatenate([a[..., :HALF], a[..., QK_NOPE:QK_NOPE + q], z,
                            a[..., HALF:QK_NOPE], a[..., QK_NOPE + q:], z], axis=-1)


def _kpe_layout(a):
    q = QK_ROPE // 2
    z64 = jnp.zeros(a.shape[:-1] + (HALF,), a.dtype)
    z32 = jnp.zeros(a.shape[:-1] + (q,), a.dtype)
    return jnp.concatenate([z64, a[..., :q], z32, z64, a[..., q:], z32], axis=-1)


def _pair_layout(a):
    lead = a.shape[:-1]
    a = a.reshape(lead + (RET_HEADS // 2, 2, 2, HALF))
    a = jnp.swapaxes(a, -3, -2)
    return a.reshape(lead + (RET_WIDTH,))


def _prep_params(attn_norm_g, w_in, q_a_norm_g, w_q_b, kv_a_norm_g, w_kv_b, q_norm_g, k_norm_g,
                 mla_out_norm_g, ret_norm_g, ret_norm_b, w_out, ffn_norm_g, w_gate_up, w_down):
    depth = w_in.shape[0]
    o_kpe = Q_LORA + KV_LORA
    o_rq = o_kpe + QK_ROPE
    o_rk = o_rq + RET_WIDTH
    o_rv = o_rk + RET_WIDTH
    w_head = jnp.concatenate([w_in[..., :o_kpe], _kpe_layout(w_in[..., o_kpe:o_rq])],
                             axis=-1).astype(bf16)
    w_qk = jnp.concatenate([_pair_layout(w_in[..., o_rq:o_rk]), _pair_layout(w_in[..., o_rk:o_rv])],
                           axis=-1).astype(bf16)
    w_vg = w_in[..., o_rv:].astype(bf16)
    wq = _mla_head_layout(w_q_b.astype(bf16).reshape(depth, Q_LORA, MLA_HEADS, QK_HEAD))
    wkv = w_kv_b.astype(bf16).reshape(depth, KV_LORA, MLA_HEADS, QK_NOPE + V_HEAD)
    z64 = jnp.zeros((depth, KV_LORA, MLA_HEADS, HALF), bf16)
    wkv = jnp.concatenate([wkv[..., :HALF], z64, wkv[..., HALF:QK_NOPE], z64, wkv[..., QK_NOPE:]],
                          axis=-1)
    scale = QK_HEAD ** -0.5 * LOG2_E
    vec = lambda a: a[:, None, :]
    return {
        "attn_g": vec(attn_norm_g),
        "w_head": w_head,
        "w_qk": w_qk,
        "w_vg": w_vg,
        "qa_g": vec(q_a_norm_g),
        "w_qb": wq.reshape(depth, Q_LORA, MLA_HEADS * QK_PAD),
        "kva_g": vec(kv_a_norm_g),
        "w_kvb": wkv.reshape(depth, KV_LORA, MLA_HEADS * KV_PAD),
        "q_g": vec(_mla_head_layout(q_norm_g * scale)),
        "k_g": vec(_mla_head_layout(k_norm_g)),
        "out_g": vec(mla_out_norm_g),
        "ret_g": vec(ret_norm_g),
        "ret_b": vec(ret_norm_b),
        "w_out": w_out.astype(bf16),
        "ffn_g": vec(ffn_norm_g),
        "w_gu": w_gate_up.astype(bf16),
        "w_down": w_down.astype(bf16),
    }


def kernel(x, meta_tokens, attn_norm_g, w_in, q_a_norm_g, w_q_b, kv_a_norm_g, w_kv_b, q_norm_g,
           k_norm_g, mla_out_norm_g, ret_norm_g, ret_norm_b, w_out, ffn_norm_g, w_gate_up, w_down):
    depth = w_in.shape[0]
    hx = x
    hm = jnp.concatenate([jnp.zeros((PAD_ROWS, D_MODEL), x.dtype), meta_tokens.astype(x.dtype)])[None]
    r = np.arange(BLOCK)
    tabs_x = _rope_tables(np.arange(SEQ) + N_META, np.ones(SEQ))
    tabs_m = _rope_tables(np.maximum(r - PAD_ROWS, 0), r >= PAD_ROWS)
    gamma = np.float32(1.0) - np.float32(2.0) ** (np.float32(-5.0) - np.arange(RET_HEADS, dtype=np.float32))
    log_g = jnp.asarray(np.broadcast_to(np.log(gamma)[:, None, None], (RET_HEADS, 8, LANES)))
    p = _prep_params(attn_norm_g, w_in, q_a_norm_g, w_q_b, kv_a_norm_g, w_kv_b, q_norm_g, k_norm_g,
                     mla_out_norm_g, ret_norm_g, ret_norm_b, w_out, ffn_norm_g, w_gate_up, w_down)
    for l in range(depth):
        q, k, v, rq, rk, rv, rg = _pre_call(hx, p, l, tabs_x, ROW_TM, "pre")
        qm, km, vm, rqm, rkm, rvm, rgm = _pre_call(hm, p, l, tabs_m, BLOCK, "pre_meta")
        y_mla = _attn_call(q, k, v, km, vm, p["out_g"], l)
        y_ret = _ret_call(rq, rk, rv, rg, rkm, rvm, log_g, p["ret_g"], p["ret_b"], l)
        hx = _post_call(hx, y_mla, y_ret, p, l, ROW_TM, "post")
        if l + 1 < depth:
            ym_mla = _attn_meta_call(qm, km, vm, p["out_g"], l)
            ym_ret = _ret_meta_call(rqm, rkm, rvm, rgm, log_g, p["ret_g"], p["ret_b"], l)
            hm = _post_call(hm, ym_mla, ym_ret, p, l, BLOCK, "post_meta")
    return hx
```

```python
import functools

import numpy as np

import jax
import jax.numpy as jnp
from jax import lax
from jax.experimental import pallas as pl
from jax.experimental.pallas import tpu as pltpu

D_MODEL = 1024
SEQ = 2048
N_META = 16
BLOCK = 128
MLA_HEADS = 4
Q_LORA = 256
KV_LORA = 256
QK_NOPE = 128
QK_ROPE = 64
QK_HEAD = QK_NOPE + QK_ROPE
V_HEAD = 128
MLA_WIDTH = MLA_HEADS * V_HEAD
RET_HEADS = 4
RET_HEAD = 128
RET_WIDTH = RET_HEADS * RET_HEAD
D_FF = 2816
ROPE_BASE = 10000.0
EPS = 1e-6
NEG_INF = -1e30

LANES = 128
HALF = LANES // 2
PAD_ROWS = BLOCK - N_META
QK_PAD = 2 * LANES
KV_PAD = 3 * LANES
RET_PAIR = 2 * RET_HEAD

C_CQ = 0
C_CKV = C_CQ + Q_LORA
C_KPE = C_CKV + KV_LORA
C_RQ = C_KPE + 2 * LANES
C_RK = C_RQ + RET_WIDTH
C_RV = C_RK + RET_WIDTH
C_RG = C_RV + RET_WIDTH
N_IN_PAD = C_RG + RET_WIDTH

ROW_TM = 1024
PRE_TM = 1024
ATT_TQ = 512
ATT_ORDER = (0, 1, 2, 3)
ATT_HPS = 4
ATT_AHEAD = 2
RET_CHUNK = 256
FF_TILE = 256
LOG2_E = 1.4426950408889634
VMEM_LIMIT = 56 * 1024 * 1024

f32 = jnp.float32
bf16 = jnp.bfloat16


def _dot(a, b):
    return jnp.dot(a, b, preferred_element_type=f32)


def _dot_nt(a, b):
    return lax.dot_general(a, b, (((1,), (1,)), ((), ())), preferred_element_type=f32)


def _dot_tn(a, b):
    return lax.dot_general(a, b, (((0,), (0,)), ((), ())), preferred_element_type=f32)


def _rms(x, inv_n):
    return lax.rsqrt(jnp.sum(x * x, axis=-1, keepdims=True) * inv_n + EPS)


def _sigmoid(x):
    return 1.0 / (1.0 + jnp.exp(-x))


def _rot(a, b, c, s):
    return a * c - b * s, b * c + a * s


def _params(n_grid_axes):
    return pltpu.CompilerParams(dimension_semantics=("parallel",) * n_grid_axes,
                                vmem_limit_bytes=VMEM_LIMIT)


def _layer_spec(l, shape):
    nd = len(shape)
    return pl.BlockSpec((None,) + shape, lambda *_: (l,) + (0,) * nd, pipeline_mode=pl.Buffered(1))


def _pre_kernel(h_ref, g_ref, whead_ref, wqk_ref, wvg_ref, qag_ref, wqb_ref, kvag_ref, wkvb_ref,
                qg_ref, kg_ref, ca_ref, sa_ref, c2_ref, s2_ref, ck_ref, sk_ref,
                q_out, k_out, v_out, rq_out, rk_out, rv_out, rg_out, hb_ref, cqn_ref, ckvn_ref,
                kpe_ref):
    qga, qgb = qg_ref[:, :LANES], qg_ref[:, LANES:]
    kga, kgb = kg_ref[:, :LANES], kg_ref[:, LANES:]
    first = lax.broadcasted_iota(jnp.int32, (1, LANES), 1) < HALF
    G = 2 * LANES

    x = h_ref[...]
    hb_ref[...] = (x * _rms(x, 1.0 / D_MODEL) * g_ref[...]).astype(bf16)

    def proj(w_ref, lo):
        return _dot(hb_ref[...], w_ref[:, lo:lo + G])

    cq = proj(whead_ref, C_CQ)
    cqn_ref[...] = (cq * _rms(cq, 1.0 / Q_LORA) * qag_ref[...]).astype(bf16)
    ckv = proj(whead_ref, C_CKV)
    ckvn_ref[...] = (ckv * _rms(ckv, 1.0 / KV_LORA) * kvag_ref[...]).astype(bf16)
    kpe_ref[...] = proj(whead_ref, C_KPE)

    def norm_rope(ab, ga, gb):
        a, b = ab[:, :LANES], ab[:, LANES:]
        r = lax.rsqrt(jnp.sum(a * a + b * b, axis=-1, keepdims=True) * (1.0 / QK_HEAD) + EPS)
        ao, bo = _rot(a * ga, b * gb, ca_ref[...], sa_ref[...])
        return (ao * r).astype(bf16), (bo * r).astype(bf16)

    def mla_dots(hp):
        lo2 = hp * 2 * QK_PAD
        return (_dot(cqn_ref[...], wqb_ref[:, lo2:lo2 + 2 * QK_PAD]),
                _dot(ckvn_ref[...], wkvb_ref[:, lo2:lo2 + 2 * QK_PAD]))

    def mla_q(hp, e, q2h):
        lo = (2 * hp + e) * QK_PAD
        q_out[:, lo:lo + LANES], q_out[:, lo + LANES:lo + QK_PAD] = norm_rope(
            q2h[:, e * QK_PAD:(e + 1) * QK_PAD], qga, qgb)

    def mla_k(hp, e, k2h):
        lo = (2 * hp + e) * QK_PAD
        k_out[:, lo:lo + LANES], k_out[:, lo + LANES:lo + QK_PAD] = norm_rope(
            k2h[:, e * QK_PAD:(e + 1) * QK_PAD] + kpe_ref[...], kga, kgb)

    def ret_pairs(zq, zk):
        for p in range(RET_HEADS // 2):
            lo = p * RET_PAIR
            k1, k2 = _rot(zk[:, lo:lo + LANES], zk[:, lo + LANES:lo + RET_PAIR], ck_ref[...],
                          sk_ref[...])
            rk_out[:, lo:lo + LANES] = k1.astype(bf16)
            rk_out[:, lo + LANES:lo + RET_PAIR] = k2.astype(bf16)
            q1, q2 = _rot(zq[:, lo:lo + LANES], zq[:, lo + LANES:lo + RET_PAIR], c2_ref[...],
                          s2_ref[...])
            for e in range(2):
                keep = first if e == 0 else jnp.logical_not(first)
                qo = (2 * p + e) * RET_PAIR
                rq_out[:, qo:qo + LANES] = jnp.where(keep, q1, 0.0).astype(bf16)
                rq_out[:, qo + LANES:qo + RET_PAIR] = jnp.where(keep, q2, 0.0).astype(bf16)

    def mla_pair(hp):
        q2h, k2h = mla_dots(hp)
        for e in range(2):
            mla_q(hp, e, q2h)
            mla_k(hp, e, k2h)

    rv_out[...] = _dot(hb_ref[...], wvg_ref[:, :RET_WIDTH]).astype(bf16)
    mla_pair(0)
    rg_out[...] = _dot(hb_ref[...], wvg_ref[:, RET_WIDTH:]).astype(bf16)
    mla_pair(1)
    zk = _dot(hb_ref[...], wqk_ref[:, RET_WIDTH:])
    zq = _dot(hb_ref[...], wqk_ref[:, :RET_WIDTH])
    v_out[...] = _dot(ckvn_ref[...], wkvb_ref[:, MLA_HEADS * QK_PAD:]).astype(bf16)
    ret_pairs(zq, zk)


def _pre_call(h, p, l, tabs, tm, name):
    B, R, _ = h.shape
    row = lambda w: pl.BlockSpec((None, tm, w), lambda b, j: (b, j, 0))
    tab = pl.BlockSpec((tm, LANES), lambda b, j: (j, 0))
    in_specs = [
        row(D_MODEL),
        _layer_spec(l, (1, D_MODEL)),
        _layer_spec(l, (D_MODEL, C_RQ)),
        _layer_spec(l, (D_MODEL, C_RV - C_RQ)),
        _layer_spec(l, (D_MODEL, N_IN_PAD - C_RV)),
        _layer_spec(l, (1, Q_LORA)),
        _layer_spec(l, (Q_LORA, MLA_HEADS * QK_PAD)),
        _layer_spec(l, (1, KV_LORA)),
        _layer_spec(l, (KV_LORA, MLA_HEADS * KV_PAD)),
        _layer_spec(l, (1, QK_PAD)),
        _layer_spec(l, (1, QK_PAD)),
    ] + [tab] * 6
    widths = (MLA_HEADS * QK_PAD, MLA_HEADS * QK_PAD, MLA_WIDTH, RET_HEADS * RET_PAIR, RET_WIDTH,
              RET_WIDTH, RET_WIDTH)
    return pl.pallas_call(
        _pre_kernel, grid=(B, R // tm),
        in_specs=in_specs, out_specs=[row(w) for w in widths],
        out_shape=[jax.ShapeDtypeStruct((B, R, w), bf16) for w in widths],
        scratch_shapes=[pltpu.VMEM((tm, D_MODEL), bf16), pltpu.VMEM((tm, Q_LORA), bf16),
                        pltpu.VMEM((tm, KV_LORA), bf16), pltpu.VMEM((tm, 2 * LANES), f32)],
        name=name, compiler_params=_params(2),
    )(h, p["attn_g"], p["w_head"], p["w_qk"], p["w_vg"], p["qa_g"], p["w_qb"], p["kva_g"], p["w_kvb"],
      p["q_g"], p["k_g"], *tabs)


def _rowmax(s):
    return jnp.max(s, axis=-1, keepdims=True)


def _rowsum(p):
    return jnp.sum(p, axis=-1, keepdims=True)


def _attn_finish(acc, l, og):
    o = acc * (1.0 / l)
    return (o * _rms(o, 1.0 / V_HEAD) * og).astype(bf16)


def _meta_valid():
    return lax.broadcasted_iota(jnp.int32, (1, BLOCK), 1) >= PAD_ROWS


def _attn_kernel(q_ref, k_ref, v_ref, km_ref, vm_ref, og_ref, o_ref, vt_ref):
    for hh in range(ATT_HPS):
        vv = slice(hh * V_HEAD, (hh + 1) * V_HEAD)
        vt_ref[vv, 0:BLOCK] = vm_ref[:, vv].T
        vt_ref[vv, BLOCK:] = v_ref[:, vv].T
    key_valid = lax.broadcasted_iota(jnp.int32, (BLOCK, 1), 0) >= PAD_ROWS
    tri_t = (lax.broadcasted_iota(jnp.int32, (ATT_TQ, ATT_TQ), 0)
             <= lax.broadcasted_iota(jnp.int32, (ATT_TQ, ATT_TQ), 1))

    def scores(item):
        hh, i = item
        qk = slice(hh * QK_PAD, (hh + 1) * QK_PAD)
        lo, hi = i * ATT_TQ, (i + 1) * ATT_TQ
        q = q_ref[lo:hi, qk]
        s_m = jnp.where(key_valid, _dot_nt(km_ref[:, qk], q), NEG_INF)
        s_x = _dot_nt(k_ref[0:hi, qk], q)
        parts = [s_m] + ([s_x[:lo]] if lo else []) + [jnp.where(tri_t, s_x[lo:], NEG_INF)]
        return jnp.concatenate(parts, axis=0)

    items = [(hh, i) for i in ATT_ORDER for hh in range(ATT_HPS)]
    ahead = [scores(it) for it in items[:ATT_AHEAD]]
    for n, (hh, i) in enumerate(items):
        if n + ATT_AHEAD < len(items):
            ahead.append(scores(items[n + ATT_AHEAD]))
        s = ahead.pop(0)
        vv = slice(hh * V_HEAD, (hh + 1) * V_HEAD)
        lo, hi = i * ATT_TQ, (i + 1) * ATT_TQ
        p = jnp.exp2(s - jnp.max(s, axis=0, keepdims=True))
        l = jnp.sum(p, axis=0, keepdims=True)
        o_t = _dot(vt_ref[vv, 0:BLOCK + hi], p.astype(bf16)) * (1.0 / l)
        r = lax.rsqrt(jnp.sum(o_t * o_t, axis=0, keepdims=True) * (1.0 / V_HEAD) + EPS)
        o_ref[lo:hi, vv] = ((o_t * r).T * og_ref[:, vv]).astype(bf16)


def _attn_meta_kernel(q_ref, k_ref, v_ref, og_ref, o_ref):
    mask = (lax.broadcasted_iota(jnp.int32, (BLOCK, BLOCK), 0)
            >= lax.broadcasted_iota(jnp.int32, (BLOCK, BLOCK), 1)) & _meta_valid()
    s = jnp.where(mask, _dot_nt(q_ref[...], k_ref[...]), NEG_INF)
    p = jnp.exp2(s - _rowmax(s))
    o_ref[...] = _attn_finish(_dot(p.astype(bf16), v_ref[...]), _rowsum(p), og_ref[...])


def _attn_call(q, k, v, km, vm, out_g, l):
    B = q.shape[0]
    n = ATT_HPS
    qk_spec = pl.BlockSpec((None, SEQ, n * QK_PAD), lambda b, h: (b, 0, h))
    v_spec = pl.BlockSpec((None, SEQ, n * V_HEAD), lambda b, h: (b, 0, h))
    return pl.pallas_call(
        _attn_kernel, grid=(B, MLA_HEADS // n),
        in_specs=[qk_spec, qk_spec, v_spec,
                  pl.BlockSpec((None, BLOCK, n * QK_PAD), lambda b, h: (0, 0, h)),
                  pl.BlockSpec((None, BLOCK, n * V_HEAD), lambda b, h: (0, 0, h)),
                  pl.BlockSpec((None, 1, n * V_HEAD), lambda b, h: (l, 0, h))],
        out_specs=v_spec,
        out_shape=jax.ShapeDtypeStruct((B, SEQ, MLA_WIDTH), bf16),
        scratch_shapes=[pltpu.VMEM((n * V_HEAD, BLOCK + SEQ), bf16)],
        name="attn", compiler_params=_params(2),
    )(q, k, v, km, vm, out_g)


def _attn_meta_call(qm, km, vm, out_g, l):
    qk_spec = pl.BlockSpec((None, BLOCK, QK_PAD), lambda h: (0, 0, h))
    v_spec = pl.BlockSpec((None, BLOCK, V_HEAD), lambda h: (0, 0, h))
    return pl.pallas_call(
        _attn_meta_kernel, grid=(MLA_HEADS,),
        in_specs=[qk_spec, qk_spec, v_spec, pl.BlockSpec((None, 1, V_HEAD), lambda h: (l, 0, h))],
        out_specs=v_spec,
        out_shape=jax.ShapeDtypeStruct((1, BLOCK, MLA_WIDTH), bf16),
        name="attn_meta", compiler_params=_params(1),
    )(qm, km, vm, out_g)


def _ret_tables(lg_ref, C):
    lg = lg_ref[0:1, :]
    lgc = jnp.concatenate([lg] * (C // LANES), axis=1)
    ri = lax.broadcasted_iota(jnp.int32, (C, C), 0)
    ci = lax.broadcasted_iota(jnp.int32, (C, C), 1)
    diff = (ri - ci).astype(f32)
    decay = jnp.where(diff >= 0, jnp.exp(jnp.maximum(diff, 0.0) * lgc), 0.0)
    idx = lax.broadcasted_iota(jnp.int32, (C, RET_HEAD), 0).astype(f32)
    xi = jnp.exp((idx + 1.0) * lg)
    return lg, decay, idx, xi


def _ret_key_state(k, v, idx, lg):
    n = k.shape[0]
    zeta = jnp.exp((n - 1.0 - idx[:n]) * lg)
    vz = (v.astype(f32) * zeta).astype(bf16)
    return _dot_tn(vz, k)


def _ret_mix(q, k, v, decay, state_t, xi):
    n = q.shape[0]
    o = _dot((_dot_nt(q, k) * decay[:n, :n]).astype(bf16), v)
    if state_t is not None:
        o = o + _dot_nt(q, state_t.astype(bf16)) * xi[:n]
    return o


def _ret_emit(o, g, ng, nb):
    mu = jnp.mean(o, axis=-1, keepdims=True)
    d = o - mu
    var = jnp.mean(d * d, axis=-1, keepdims=True)
    on = d * lax.rsqrt(var + EPS) * ng + nb
    g = g.astype(f32)
    return (g * _sigmoid(g) * on).astype(bf16)


def _ret_kernel(q_ref, k_ref, v_ref, g_ref, km_ref, vm_ref, lg_ref, ng_ref, nb_ref, o_ref):
    C = RET_CHUNK
    n_chunks = SEQ // C
    chunks = [slice(c * C, (c + 1) * C) for c in range(n_chunks)]
    heads = []
    for e in range(2):
        lg, decay, idx, xi = _ret_tables(lg_ref.at[e], C)
        hv = slice(e * RET_HEAD, (e + 1) * RET_HEAD)
        heads.append(dict(
            lg=lg, decay=decay, idx=idx, xi=xi, hv=hv, hq=slice(e * RET_PAIR, (e + 1) * RET_PAIR),
            chunk_decay=jnp.exp(float(C) * jnp.concatenate([lg, lg], axis=1)),
            state_t=_ret_key_state(km_ref[...], vm_ref[:, hv], idx, lg), o_prev=None))
    for c, rows in enumerate(chunks):
        for hd in heads:
            hv = hd["hv"]
            o = _ret_mix(q_ref[rows, hd["hq"]], k_ref[rows, :], v_ref[rows, hv], hd["decay"],
                         hd["state_t"], hd["xi"])
            if c + 1 < n_chunks:
                hd["state_t"] = hd["state_t"] * hd["chunk_decay"] + _ret_key_state(
                    k_ref[rows, :], v_ref[rows, hv], hd["idx"], hd["lg"])
            if hd["o_prev"] is not None:
                o_ref[chunks[c - 1], hv] = _ret_emit(hd["o_prev"], g_ref[chunks[c - 1], hv],
                                                     ng_ref[:, hv], nb_ref[:, hv])
            hd["o_prev"] = o
    for hd in heads:
        hv = hd["hv"]
        o_ref[chunks[-1], hv] = _ret_emit(hd["o_prev"], g_ref[chunks[-1], hv], ng_ref[:, hv],
                                          nb_ref[:, hv])


def _ret_meta_kernel(q_ref, k_ref, v_ref, g_ref, lg_ref, ng_ref, nb_ref, o_ref):
    _, decay, _, _ = _ret_tables(lg_ref, BLOCK)
    o = _ret_mix(q_ref[...], k_ref[...], v_ref[...], decay, None, None)
    o_ref[...] = _ret_emit(o, g_ref[...], ng_ref[...], nb_ref[...])


def _ret_call(rq, rk, rv, rg, rkm, rvm, log_g, norm_g, norm_b, l):
    B = rq.shape[0]
    spec = pl.BlockSpec((None, SEQ, 2 * RET_HEAD), lambda b, p: (b, 0, p))
    vec = pl.BlockSpec((None, 1, 2 * RET_HEAD), lambda b, p: (l, 0, p))
    return pl.pallas_call(
        _ret_kernel, grid=(B, RET_HEADS // 2),
        in_specs=[pl.BlockSpec((None, SEQ, 2 * RET_PAIR), lambda b, p: (b, 0, p)),
                  pl.BlockSpec((None, SEQ, RET_PAIR), lambda b, p: (b, 0, p)),
                  spec, spec,
                  pl.BlockSpec((None, BLOCK, RET_PAIR), lambda b, p: (0, 0, p)),
                  pl.BlockSpec((None, BLOCK, 2 * RET_HEAD), lambda b, p: (0, 0, p)),
                  pl.BlockSpec((2, 8, LANES), lambda b, p: (p, 0, 0)), vec, vec],
        out_specs=spec,
        out_shape=jax.ShapeDtypeStruct((B, SEQ, RET_WIDTH), bf16),
        name="ret", compiler_params=_params(2),
    )(rq, rk, rv, rg, rkm, rvm, log_g, norm_g, norm_b)


def _ret_meta_call(rqm, rkm, rvm, rgm, log_g, norm_g, norm_b, l):
    spec = pl.BlockSpec((None, BLOCK, RET_HEAD), lambda h: (0, 0, h))
    vec = pl.BlockSpec((None, 1, RET_HEAD), lambda h: (l, 0, h))
    return pl.pallas_call(
        _ret_meta_kernel, grid=(RET_HEADS,),
        in_specs=[pl.BlockSpec((None, BLOCK, RET_PAIR), lambda h: (0, 0, h)),
                  pl.BlockSpec((None, BLOCK, RET_PAIR), lambda h: (0, 0, h // 2)),
                  spec, spec, pl.BlockSpec((None, 8, LANES), lambda h: (h, 0, 0)), vec, vec],
        out_specs=spec,
        out_shape=jax.ShapeDtypeStruct((1, BLOCK, RET_WIDTH), bf16),
        name="ret_meta", compiler_params=_params(1),
    )(rqm, rkm, rvm, rgm, log_g, norm_g, norm_b)


def _post_kernel(h_ref, ym_ref, yr_ref, wo_ref, fg_ref, wgu_ref, wd_ref, o_ref, act_ref):
    h1 = (h_ref[...] + _dot(ym_ref[...], wo_ref[0:MLA_WIDTH, :])
          + _dot(yr_ref[...], wo_ref[MLA_WIDTH:MLA_WIDTH + RET_WIDTH, :]))
    hf = (h1 * _rms(h1, 1.0 / D_MODEL) * fg_ref[...]).astype(bf16)
    for c in range(D_FF // FF_TILE):
        lo = c * FF_TILE
        gate = _dot(hf, wgu_ref[:, lo:lo + FF_TILE])
        up = _dot(hf, wgu_ref[:, D_FF + lo:D_FF + lo + FF_TILE])
        act_ref[:, lo:lo + FF_TILE] = (gate * _sigmoid(gate) * up).astype(bf16)
    o_ref[...] = h1 + _dot(act_ref[...], wd_ref[...])


def _post_call(h, ym, yr, p, l, tm, name):
    B, R, _ = h.shape
    row = lambda w: pl.BlockSpec((None, tm, w), lambda b, j: (b, j, 0))
    return pl.pallas_call(
        _post_kernel, grid=(B, R // tm),
        in_specs=[row(D_MODEL), row(MLA_WIDTH), row(RET_WIDTH),
                  _layer_spec(l, (MLA_WIDTH + RET_WIDTH, D_MODEL)),
                  _layer_spec(l, (1, D_MODEL)),
                  _layer_spec(l, (D_MODEL, 2 * D_FF)),
                  _layer_spec(l, (D_FF, D_MODEL))],
        out_specs=row(D_MODEL),
        out_shape=jax.ShapeDtypeStruct((B, R, D_MODEL), f32),
        scratch_shapes=[pltpu.VMEM((tm, D_FF), bf16)],
        name=name, compiler_params=_params(2),
    )(h, ym, yr, p["w_out"], p["ffn_g"], p["w_gu"], p["w_down"])


def _rope_tables(pos, valid):
    pos = pos.astype(np.float32)
    n = pos.shape[0]

    def cs(dim):
        inv = np.float32(ROPE_BASE) ** (-np.arange(0, dim, 2, dtype=np.float32) / np.float32(dim))
        ang = pos[:, None] * inv[None, :].astype(np.float32)
        return np.cos(ang).astype(np.float32), np.sin(ang).astype(np.float32)

    cm, sm = cs(QK_ROPE)
    z32 = np.zeros_like(cm)
    ca = np.concatenate([np.ones((n, HALF), np.float32), cm, z32], axis=1)
    sa = np.concatenate([np.zeros((n, HALF), np.float32), sm, z32], axis=1)
    cr, sr = cs(RET_HEAD)
    c2 = np.concatenate([cr, cr], axis=1)
    s2 = np.concatenate([sr, sr], axis=1)
    kscale = valid.astype(np.float32)[:, None] * np.float32(RET_HEAD ** -0.5)
    return tuple(jnp.asarray(t) for t in (ca, sa, c2, s2, c2 * kscale, s2 * kscale))


def _mla_head_layout(a):
    q = QK_ROPE // 2
    z = jnp.zeros(a.shape[:-1] + (q,), a.dtype)
    return jnp.concatenate([a[..., :HALF], a[..., QK_NOPE:QK_NOPE + q], z,
                            a[..., HALF:QK_NOPE], a[..., QK_NOPE + q:], z], axis=-1)


def _kpe_layout(a):
    q = QK_ROPE // 2
    z64 = jnp.zeros(a.shape[:-1] + (HALF,), a.dtype)
    z32 = jnp.zeros(a.shape[:-1] + (q,), a.dtype)
    return jnp.concatenate([z64, a[..., :q], z32, z64, a[..., q:], z32], axis=-1)


def _pair_layout(a):
    lead = a.shape[:-1]
    a = a.reshape(lead + (RET_HEADS // 2, 2, 2, HALF))
    a = jnp.swapaxes(a, -3, -2)
    return a.reshape(lead + (RET_WIDTH,))


def _prep_params(attn_norm_g, w_in, q_a_norm_g, w_q_b, kv_a_norm_g, w_kv_b, q_norm_g, k_norm_g,
                 mla_out_norm_g, ret_norm_g, ret_norm_b, w_out, ffn_norm_g, w_gate_up, w_down):
    depth = w_in.shape[0]
    o_kpe = Q_LORA + KV_LORA
    o_rq = o_kpe + QK_ROPE
    o_rk = o_rq + RET_WIDTH
    o_rv = o_rk + RET_WIDTH
    w_head = jnp.concatenate([w_in[..., :o_kpe], _kpe_layout(w_in[..., o_kpe:o_rq])],
                             axis=-1).astype(bf16)
    w_qk = jnp.concatenate([_pair_layout(w_in[..., o_rq:o_rk]), _pair_layout(w_in[..., o_rk:o_rv])],
                           axis=-1).astype(bf16)
    w_vg = w_in[..., o_rv:].astype(bf16)
    wq = _mla_head_layout(w_q_b.astype(bf16).reshape(depth, Q_LORA, MLA_HEADS, QK_HEAD))
    wkv = w_kv_b.astype(bf16).reshape(depth, KV_LORA, MLA_HEADS, QK_NOPE + V_HEAD)
    z64 = jnp.zeros((depth, KV_LORA, MLA_HEADS, HALF), bf16)
    wk = jnp.concatenate([wkv[..., :HALF], z64, wkv[..., HALF:QK_NOPE], z64], axis=-1)
    wkv = jnp.concatenate([wk.reshape(depth, KV_LORA, MLA_HEADS * QK_PAD),
                           wkv[..., QK_NOPE:].reshape(depth, KV_LORA, MLA_WIDTH)], axis=-1)
    scale = QK_HEAD ** -0.5 * LOG2_E
    vec = lambda a: a[:, None, :]
    return {
        "attn_g": vec(attn_norm_g),
        "w_head": w_head,
        "w_qk": w_qk,
        "w_vg": w_vg,
        "qa_g": vec(q_a_norm_g),
        "w_qb": wq.reshape(depth, Q_LORA, MLA_HEADS * QK_PAD),
        "kva_g": vec(kv_a_norm_g),
        "w_kvb": wkv,
        "q_g": vec(_mla_head_layout(q_norm_g * scale)),
        "k_g": vec(_mla_head_layout(k_norm_g)),
        "out_g": vec(mla_out_norm_g),
        "ret_g": vec(ret_norm_g),
        "ret_b": vec(ret_norm_b),
        "w_out": w_out.astype(bf16),
        "ffn_g": vec(ffn_norm_g),
        "w_gu": w_gate_up.astype(bf16),
        "w_down": w_down.astype(bf16),
    }


def kernel(x, meta_tokens, attn_norm_g, w_in, q_a_norm_g, w_q_b, kv_a_norm_g, w_kv_b, q_norm_g,
           k_norm_g, mla_out_norm_g, ret_norm_g, ret_norm_b, w_out, ffn_norm_g, w_gate_up, w_down):
    depth = w_in.shape[0]
    hx = x
    hm = jnp.concatenate([jnp.zeros((PAD_ROWS, D_MODEL), x.dtype), meta_tokens.astype(x.dtype)])[None]
    r = np.arange(BLOCK)
    tabs_x = _rope_tables(np.arange(SEQ) + N_META, np.ones(SEQ))
    tabs_m = _rope_tables(np.maximum(r - PAD_ROWS, 0), r >= PAD_ROWS)
    gamma = np.float32(1.0) - np.float32(2.0) ** (np.float32(-5.0) - np.arange(RET_HEADS, dtype=np.float32))
    log_g = jnp.asarray(np.broadcast_to(np.log(gamma)[:, None, None], (RET_HEADS, 8, LANES)))
    p = _prep_params(attn_norm_g, w_in, q_a_norm_g, w_q_b, kv_a_norm_g, w_kv_b, q_norm_g, k_norm_g,
                     mla_out_norm_g, ret_norm_g, ret_norm_b, w_out, ffn_norm_g, w_gate_up, w_down)
    for l in range(depth):
        q, k, v, rq, rk, rv, rg = _pre_call(hx, p, l, tabs_x, PRE_TM, "pre")
        qm, km, vm, rqm, rkm, rvm, rgm = _pre_call(hm, p, l, tabs_m, BLOCK, "pre_meta")
        y_mla = _attn_call(q, k, v, km, vm, p["out_g"], l)
        y_ret = _ret_call(rq, rk, rv, rg, rkm, rvm, log_g, p["ret_g"], p["ret_b"], l)
        hx = _post_call(hx, y_mla, y_ret, p, l, ROW_TM, "post")
        if l + 1 < depth:
            ym_mla = _attn_meta_call(qm, km, vm, p["out_g"], l)
            ym_ret = _ret_meta_call(rqm, rkm, rvm, rgm, log_g, p["ret_g"], p["ret_b"], l)
            hm = _post_call(hm, ym_mla, ym_ret, p, l, BLOCK, "post_meta")
    return hx
```

```python
import numpy as np

import jax
import jax.numpy as jnp
from jax import lax
from jax.experimental import pallas as pl
from jax.experimental.pallas import tpu as pltpu

D_MODEL = 1024
SEQ = 2048
N_META = 16
BLOCK = 128
MLA_HEADS = 4
Q_LORA = 256
KV_LORA = 256
QK_NOPE = 128
QK_ROPE = 64
QK_HEAD = QK_NOPE + QK_ROPE
V_HEAD = 128
MLA_WIDTH = MLA_HEADS * V_HEAD
RET_HEADS = 4
RET_HEAD = 128
RET_WIDTH = RET_HEADS * RET_HEAD
D_FF = 2816
ROPE_BASE = 10000.0
EPS = 1e-6
NEG_INF = -1e30

LANES = 128
HALF = LANES // 2
PAD_ROWS = BLOCK - N_META
QK_PAD = 2 * LANES
KV_WIDTH = MLA_HEADS * QK_PAD + MLA_WIDTH
RET_PAIR = 2 * RET_HEAD

C_CQ = 0
C_CKV = C_CQ + Q_LORA
C_KPE = C_CKV + KV_LORA
C_RQ = C_KPE + 2 * LANES
C_RK = C_RQ + RET_WIDTH
C_RV = C_RK + RET_WIDTH
C_RG = C_RV + RET_WIDTH
N_IN_PAD = C_RG + RET_WIDTH

ROW_TM = 1024
PRE_TM = 1024
ATT_TQ = 512
ATT_ORDER = (0, 1, 2, 3)
ATT_HPS = 4
ATT_AHEAD = 2
RET_CHUNK = 256
FF_TILE = 256
assert D_FF % FF_TILE == 0
LOG2_E = 1.4426950408889634
VMEM_LIMIT = 56 * 1024 * 1024

f32 = jnp.float32
bf16 = jnp.bfloat16


def _dot(a, b):
    return jnp.dot(a, b, preferred_element_type=f32)


def _dot_nt(a, b):
    return lax.dot_general(a, b, (((1,), (1,)), ((), ())), preferred_element_type=f32)


def _dot_tn(a, b):
    return lax.dot_general(a, b, (((0,), (0,)), ((), ())), preferred_element_type=f32)


def _rms(x, inv_n):
    return lax.rsqrt(jnp.sum(x * x, axis=-1, keepdims=True) * inv_n + EPS)


def _sigmoid(x):
    return 1.0 / (1.0 + jnp.exp(-x))


def _rot(a, b, c, s):
    return a * c - b * s, b * c + a * s


def _params(n_grid_axes):
    return pltpu.CompilerParams(dimension_semantics=("parallel",) * n_grid_axes,
                                vmem_limit_bytes=VMEM_LIMIT)


def _layer_spec(l, shape):
    nd = len(shape)
    return pl.BlockSpec((None,) + shape, lambda *_: (l,) + (0,) * nd, pipeline_mode=pl.Buffered(1))


def _pre_kernel(h_ref, g_ref, whead_ref, wqk_ref, wvg_ref, qag_ref, wqb_ref, kvag_ref, wkvb_ref,
                qg_ref, kg_ref, ca_ref, sa_ref, c2_ref, s2_ref, ck_ref, sk_ref,
                q_out, k_out, v_out, rq_out, rk_out, rv_out, rg_out, hb_ref, cqn_ref, ckvn_ref,
                kpe_ref):
    qga, qgb = qg_ref[:, :LANES], qg_ref[:, LANES:]
    kga, kgb = kg_ref[:, :LANES], kg_ref[:, LANES:]
    G = 2 * LANES

    x = h_ref[...]
    hb_ref[...] = (x * _rms(x, 1.0 / D_MODEL) * g_ref[...]).astype(bf16)

    def proj(w_ref, lo):
        return _dot(hb_ref[...], w_ref[:, lo:lo + G])

    cq = proj(whead_ref, C_CQ)
    cqn_ref[...] = (cq * _rms(cq, 1.0 / Q_LORA) * qag_ref[...]).astype(bf16)
    ckv = proj(whead_ref, C_CKV)
    ckvn_ref[...] = (ckv * _rms(ckv, 1.0 / KV_LORA) * kvag_ref[...]).astype(bf16)
    kpe_ref[...] = proj(whead_ref, C_KPE)

    def norm_rope(ab, ga, gb):
        a, b = ab[:, :LANES], ab[:, LANES:]
        r = lax.rsqrt(jnp.sum(a * a + b * b, axis=-1, keepdims=True) * (1.0 / QK_HEAD) + EPS)
        ao, bo = _rot(a * ga, b * gb, ca_ref[...], sa_ref[...])
        return (ao * r).astype(bf16), (bo * r).astype(bf16)

    def mla_pair(hp):
        lo2 = hp * 2 * QK_PAD
        q2h = _dot(cqn_ref[...], wqb_ref[:, lo2:lo2 + 2 * QK_PAD])
        k2h = _dot(ckvn_ref[...], wkvb_ref[:, lo2:lo2 + 2 * QK_PAD])
        for e in range(2):
            lo = lo2 + e * QK_PAD
            q_out[:, lo:lo + LANES], q_out[:, lo + LANES:lo + QK_PAD] = norm_rope(
                q2h[:, e * QK_PAD:(e + 1) * QK_PAD], qga, qgb)
            k_out[:, lo:lo + LANES], k_out[:, lo + LANES:lo + QK_PAD] = norm_rope(
                k2h[:, e * QK_PAD:(e + 1) * QK_PAD] + kpe_ref[...], kga, kgb)

    def ret_pairs(zq, zk):
        for p in range(RET_HEADS // 2):
            lo = p * RET_PAIR
            k1, k2 = _rot(zk[:, lo:lo + LANES], zk[:, lo + LANES:lo + RET_PAIR], ck_ref[...],
                          sk_ref[...])
            rk_out[:, lo:lo + LANES] = k1.astype(bf16)
            rk_out[:, lo + LANES:lo + RET_PAIR] = k2.astype(bf16)
            q1, q2 = _rot(zq[:, lo:lo + LANES], zq[:, lo + LANES:lo + RET_PAIR], c2_ref[...],
                          s2_ref[...])
            rq_out[:, lo:lo + LANES] = q1.astype(bf16)
            rq_out[:, lo + LANES:lo + RET_PAIR] = q2.astype(bf16)

    rv_out[...] = _dot(hb_ref[...], wvg_ref[:, :RET_WIDTH]).astype(bf16)
    mla_pair(0)
    rg_out[...] = _dot(hb_ref[...], wvg_ref[:, RET_WIDTH:]).astype(bf16)
    mla_pair(1)
    zk = _dot(hb_ref[...], wqk_ref[:, RET_WIDTH:])
    zq = _dot(hb_ref[...], wqk_ref[:, :RET_WIDTH])
    v_out[...] = _dot(ckvn_ref[...], wkvb_ref[:, MLA_HEADS * QK_PAD:]).astype(bf16)
    ret_pairs(zq, zk)


def _pre_call(h, p, l, tabs, tm, name):
    B, R, _ = h.shape
    row = lambda w: pl.BlockSpec((None, tm, w), lambda b, j: (b, j, 0))
    tab = pl.BlockSpec((tm, LANES), lambda b, j: (j, 0))
    in_specs = [
        row(D_MODEL),
        _layer_spec(l, (1, D_MODEL)),
        _layer_spec(l, (D_MODEL, C_RQ)),
        _layer_spec(l, (D_MODEL, C_RV - C_RQ)),
        _layer_spec(l, (D_MODEL, N_IN_PAD - C_RV)),
        _layer_spec(l, (1, Q_LORA)),
        _layer_spec(l, (Q_LORA, MLA_HEADS * QK_PAD)),
        _layer_spec(l, (1, KV_LORA)),
        _layer_spec(l, (KV_LORA, KV_WIDTH)),
        _layer_spec(l, (1, QK_PAD)),
        _layer_spec(l, (1, QK_PAD)),
    ] + [tab] * 6
    widths = (MLA_HEADS * QK_PAD, MLA_HEADS * QK_PAD, MLA_WIDTH, RET_WIDTH, RET_WIDTH, RET_WIDTH,
              RET_WIDTH)
    return pl.pallas_call(
        _pre_kernel, grid=(B, R // tm),
        in_specs=in_specs, out_specs=[row(w) for w in widths],
        out_shape=[jax.ShapeDtypeStruct((B, R, w), bf16) for w in widths],
        scratch_shapes=[pltpu.VMEM((tm, D_MODEL), bf16), pltpu.VMEM((tm, Q_LORA), bf16),
                        pltpu.VMEM((tm, KV_LORA), bf16), pltpu.VMEM((tm, 2 * LANES), f32)],
        name=name, compiler_params=_params(2),
    )(h, p["attn_g"], p["w_head"], p["w_qk"], p["w_vg"], p["qa_g"], p["w_qb"], p["kva_g"], p["w_kvb"],
      p["q_g"], p["k_g"], *tabs)


def _rowmax(s):
    return jnp.max(s, axis=-1, keepdims=True)


def _rowsum(p):
    return jnp.sum(p, axis=-1, keepdims=True)


def _attn_finish(acc, l, og):
    o = acc * (1.0 / l)
    return (o * _rms(o, 1.0 / V_HEAD) * og).astype(bf16)


def _meta_valid():
    return lax.broadcasted_iota(jnp.int32, (1, BLOCK), 1) >= PAD_ROWS


def _attn_kernel(q_ref, k_ref, v_ref, km_ref, vm_ref, og_ref, o_ref, vt_ref):
    for hh in range(ATT_HPS):
        vv = slice(hh * V_HEAD, (hh + 1) * V_HEAD)
        vt_ref[vv, 0:BLOCK] = vm_ref[:, vv].T
        vt_ref[vv, BLOCK:] = v_ref[:, vv].T
    key_valid = lax.broadcasted_iota(jnp.int32, (BLOCK, 1), 0) >= PAD_ROWS
    tri_t = (lax.broadcasted_iota(jnp.int32, (ATT_TQ, ATT_TQ), 0)
             <= lax.broadcasted_iota(jnp.int32, (ATT_TQ, ATT_TQ), 1))

    def scores(item):
        hh, i = item
        qk = slice(hh * QK_PAD, (hh + 1) * QK_PAD)
        lo, hi = i * ATT_TQ, (i + 1) * ATT_TQ
        q = q_ref[lo:hi, qk]
        s_m = jnp.where(key_valid, _dot_nt(km_ref[:, qk], q), NEG_INF)
        s_x = _dot_nt(k_ref[0:hi, qk], q)
        parts = [s_m] + ([s_x[:lo]] if lo else []) + [jnp.where(tri_t, s_x[lo:], NEG_INF)]
        return jnp.concatenate(parts, axis=0)

    items = [(hh, i) for i in ATT_ORDER for hh in range(ATT_HPS)]
    ahead = [scores(it) for it in items[:ATT_AHEAD]]
    for n, (hh, i) in enumerate(items):
        if n + ATT_AHEAD < len(items):
            ahead.append(scores(items[n + ATT_AHEAD]))
        s = ahead.pop(0)
        vv = slice(hh * V_HEAD, (hh + 1) * V_HEAD)
        lo, hi = i * ATT_TQ, (i + 1) * ATT_TQ
        p = jnp.exp2(s - jnp.max(s, axis=0, keepdims=True))
        l = jnp.sum(p, axis=0, keepdims=True)
        o_t = _dot(vt_ref[vv, 0:BLOCK + hi], p.astype(bf16)) * (1.0 / l)
        r = lax.rsqrt(jnp.sum(o_t * o_t, axis=0, keepdims=True) * (1.0 / V_HEAD) + EPS)
        o_ref[lo:hi, vv] = ((o_t * r).T * og_ref[:, vv]).astype(bf16)


def _attn_meta_kernel(q_ref, k_ref, v_ref, og_ref, o_ref):
    mask = (lax.broadcasted_iota(jnp.int32, (BLOCK, BLOCK), 0)
            >= lax.broadcasted_iota(jnp.int32, (BLOCK, BLOCK), 1)) & _meta_valid()
    s = jnp.where(mask, _dot_nt(q_ref[...], k_ref[...]), NEG_INF)
    p = jnp.exp2(s - _rowmax(s))
    o_ref[...] = _attn_finish(_dot(p.astype(bf16), v_ref[...]), _rowsum(p), og_ref[...])


def _attn_call(q, k, v, km, vm, out_g, l):
    B = q.shape[0]
    n = ATT_HPS
    qk_spec = pl.BlockSpec((None, SEQ, n * QK_PAD), lambda b, h: (b, 0, h))
    v_spec = pl.BlockSpec((None, SEQ, n * V_HEAD), lambda b, h: (b, 0, h))
    return pl.pallas_call(
        _attn_kernel, grid=(B, MLA_HEADS // n),
        in_specs=[qk_spec, qk_spec, v_spec,
                  pl.BlockSpec((None, BLOCK, n * QK_PAD), lambda b, h: (0, 0, h)),
                  pl.BlockSpec((None, BLOCK, n * V_HEAD), lambda b, h: (0, 0, h)),
                  pl.BlockSpec((None, 1, n * V_HEAD), lambda b, h: (l, 0, h))],
        out_specs=v_spec,
        out_shape=jax.ShapeDtypeStruct((B, SEQ, MLA_WIDTH), bf16),
        scratch_shapes=[pltpu.VMEM((n * V_HEAD, BLOCK + SEQ), bf16)],
        name="attn", compiler_params=_params(2),
    )(q, k, v, km, vm, out_g)


def _attn_meta_call(qm, km, vm, out_g, l):
    qk_spec = pl.BlockSpec((None, BLOCK, QK_PAD), lambda h: (0, 0, h))
    v_spec = pl.BlockSpec((None, BLOCK, V_HEAD), lambda h: (0, 0, h))
    return pl.pallas_call(
        _attn_meta_kernel, grid=(MLA_HEADS,),
        in_specs=[qk_spec, qk_spec, v_spec, pl.BlockSpec((None, 1, V_HEAD), lambda h: (l, 0, h))],
        out_specs=v_spec,
        out_shape=jax.ShapeDtypeStruct((1, BLOCK, MLA_WIDTH), bf16),
        name="attn_meta", compiler_params=_params(1),
    )(qm, km, vm, out_g)


def _ret_tables(lg_ref, C):
    lg = lg_ref[0:1, :]
    lgc = jnp.concatenate([lg] * (C // LANES), axis=1)
    ri = lax.broadcasted_iota(jnp.int32, (C, C), 0)
    ci = lax.broadcasted_iota(jnp.int32, (C, C), 1)
    diff = (ri - ci).astype(f32)
    decay = jnp.where(diff >= 0, jnp.exp(jnp.maximum(diff, 0.0) * lgc), 0.0)
    idx = lax.broadcasted_iota(jnp.int32, (C, RET_HEAD), 0).astype(f32)
    xi = jnp.exp((idx + 1.0) * lg)
    return lg, decay, idx, xi


def _ret_key_state(k, v, idx, lg):
    n = k.shape[0]
    zeta = jnp.exp((n - 1.0 - idx[:n]) * lg)
    vz = (v.astype(f32) * zeta).astype(bf16)
    return _dot_tn(vz, k)


def _ret_mix(q, e, k, v, decay, state_t, xi):
    lane = lax.broadcasted_iota(jnp.int32, (1, RET_PAIR), 1) % LANES
    q = jnp.where((lane < HALF) if e == 0 else (lane >= HALF), q, jnp.zeros_like(q))
    n = q.shape[0]
    o = _dot((_dot_nt(q, k) * decay[:n, :n]).astype(bf16), v)
    if state_t is not None:
        o = o + _dot_nt(q, state_t.astype(bf16)) * xi[:n]
    return o


def _ret_emit(o, g, ng, nb):
    mu = jnp.mean(o, axis=-1, keepdims=True)
    d = o - mu
    var = jnp.mean(d * d, axis=-1, keepdims=True)
    on = d * lax.rsqrt(var + EPS) * ng + nb
    g = g.astype(f32)
    return (g * _sigmoid(g) * on).astype(bf16)


def _ret_kernel(q_ref, k_ref, v_ref, g_ref, km_ref, vm_ref, lg_ref, ng_ref, nb_ref, o_ref):
    C = RET_CHUNK
    n_chunks = SEQ // C
    chunks = [slice(c * C, (c + 1) * C) for c in range(n_chunks)]
    heads = []
    for e in range(2):
        lg, decay, idx, xi = _ret_tables(lg_ref.at[e], C)
        hv = slice(e * RET_HEAD, (e + 1) * RET_HEAD)
        heads.append(dict(
            lg=lg, decay=decay, idx=idx, xi=xi, hv=hv, e=e,
            chunk_decay=jnp.exp(float(C) * jnp.concatenate([lg, lg], axis=1)),
            state_t=_ret_key_state(km_ref[...], vm_ref[:, hv], idx, lg), o_prev=None))
    for c, rows in enumerate(chunks):
        for hd in heads:
            hv = hd["hv"]
            o = _ret_mix(q_ref[rows, :], hd["e"], k_ref[rows, :], v_ref[rows, hv], hd["decay"],
                         hd["state_t"], hd["xi"])
            if c + 1 < n_chunks:
                hd["state_t"] = hd["state_t"] * hd["chunk_decay"] + _ret_key_state(
                    k_ref[rows, :], v_ref[rows, hv], hd["idx"], hd["lg"])
            if hd["o_prev"] is not None:
                o_ref[chunks[c - 1], hv] = _ret_emit(hd["o_prev"], g_ref[chunks[c - 1], hv],
                                                     ng_ref[:, hv], nb_ref[:, hv])
            hd["o_prev"] = o
    for hd in heads:
        hv = hd["hv"]
        o_ref[chunks[-1], hv] = _ret_emit(hd["o_prev"], g_ref[chunks[-1], hv], ng_ref[:, hv],
                                          nb_ref[:, hv])


def _ret_meta_kernel(q_ref, k_ref, v_ref, g_ref, lg_ref, ng_ref, nb_ref, o_ref):
    for e in range(2):
        hv = slice(e * RET_HEAD, (e + 1) * RET_HEAD)
        _, decay, _, _ = _ret_tables(lg_ref.at[e], BLOCK)
        o = _ret_mix(q_ref[...], e, k_ref[...], v_ref[:, hv], decay, None, None)
        o_ref[:, hv] = _ret_emit(o, g_ref[:, hv], ng_ref[:, hv], nb_ref[:, hv])


def _ret_call(rq, rk, rv, rg, rkm, rvm, log_g, norm_g, norm_b, l):
    B = rq.shape[0]
    pair = pl.BlockSpec((None, SEQ, RET_PAIR), lambda b, p: (b, 0, p))
    spec = pl.BlockSpec((None, SEQ, 2 * RET_HEAD), lambda b, p: (b, 0, p))
    vec = pl.BlockSpec((None, 1, 2 * RET_HEAD), lambda b, p: (l, 0, p))
    return pl.pallas_call(
        _ret_kernel, grid=(B, RET_HEADS // 2),
        in_specs=[pair, pair, spec, spec,
                  pl.BlockSpec((None, BLOCK, RET_PAIR), lambda b, p: (0, 0, p)),
                  pl.BlockSpec((None, BLOCK, 2 * RET_HEAD), lambda b, p: (0, 0, p)),
                  pl.BlockSpec((2, 8, LANES), lambda b, p: (p, 0, 0)), vec, vec],
        out_specs=spec,
        out_shape=jax.ShapeDtypeStruct((B, SEQ, RET_WIDTH), bf16),
        name="ret", compiler_params=_params(2),
    )(rq, rk, rv, rg, rkm, rvm, log_g, norm_g, norm_b)


def _ret_meta_call(rqm, rkm, rvm, rgm, log_g, norm_g, norm_b, l):
    spec = pl.BlockSpec((None, BLOCK, 2 * RET_HEAD), lambda p: (0, 0, p))
    vec = pl.BlockSpec((None, 1, 2 * RET_HEAD), lambda p: (l, 0, p))
    pair = pl.BlockSpec((None, BLOCK, RET_PAIR), lambda p: (0, 0, p))
    return pl.pallas_call(
        _ret_meta_kernel, grid=(RET_HEADS // 2,),
        in_specs=[pair, pair, spec, spec, pl.BlockSpec((2, 8, LANES), lambda p: (p, 0, 0)), vec, vec],
        out_specs=spec,
        out_shape=jax.ShapeDtypeStruct((1, BLOCK, RET_WIDTH), bf16),
        name="ret_meta", compiler_params=_params(1),
    )(rqm, rkm, rvm, rgm, log_g, norm_g, norm_b)


def _post_kernel(h_ref, ym_ref, yr_ref, wo_ref, fg_ref, wgu_ref, wd_ref, o_ref, act_ref):
    h1 = (h_ref[...] + _dot(ym_ref[...], wo_ref[0:MLA_WIDTH, :])
          + _dot(yr_ref[...], wo_ref[MLA_WIDTH:MLA_WIDTH + RET_WIDTH, :]))
    hf = (h1 * _rms(h1, 1.0 / D_MODEL) * fg_ref[...]).astype(bf16)
    for c in range(D_FF // FF_TILE):
        lo = c * FF_TILE
        gate = _dot(hf, wgu_ref[:, lo:lo + FF_TILE])
        up = _dot(hf, wgu_ref[:, D_FF + lo:D_FF + lo + FF_TILE])
        act_ref[:, lo:lo + FF_TILE] = (gate * _sigmoid(gate) * up).astype(bf16)
    o_ref[...] = h1 + _dot(act_ref[...], wd_ref[...])


def _post_call(h, ym, yr, p, l, tm, name):
    B, R, _ = h.shape
    row = lambda w: pl.BlockSpec((None, tm, w), lambda b, j: (b, j, 0))
    return pl.pallas_call(
        _post_kernel, grid=(B, R // tm),
        in_specs=[row(D_MODEL), row(MLA_WIDTH), row(RET_WIDTH),
                  _layer_spec(l, (MLA_WIDTH + RET_WIDTH, D_MODEL)),
                  _layer_spec(l, (1, D_MODEL)),
                  _layer_spec(l, (D_MODEL, 2 * D_FF)),
                  _layer_spec(l, (D_FF, D_MODEL))],
        out_specs=row(D_MODEL),
        out_shape=jax.ShapeDtypeStruct((B, R, D_MODEL), f32),
        scratch_shapes=[pltpu.VMEM((tm, D_FF), bf16)],
        name=name, compiler_params=_params(2),
    )(h, ym, yr, p["w_out"], p["ffn_g"], p["w_gu"], p["w_down"])


def _rope_tables(pos, valid):
    pos = pos.astype(np.float32)
    n = pos.shape[0]

    def cs(dim):
        inv = np.float32(ROPE_BASE) ** (-np.arange(0, dim, 2, dtype=np.float32) / np.float32(dim))
        ang = pos[:, None] * inv[None, :].astype(np.float32)
        return np.cos(ang).astype(np.float32), np.sin(ang).astype(np.float32)

    cm, sm = cs(QK_ROPE)
    z32 = np.zeros_like(cm)
    ca = np.concatenate([np.ones((n, HALF), np.float32), cm, z32], axis=1)
    sa = np.concatenate([np.zeros((n, HALF), np.float32), sm, z32], axis=1)
    cr, sr = cs(RET_HEAD)
    c2 = np.concatenate([cr, cr], axis=1)
    s2 = np.concatenate([sr, sr], axis=1)
    kscale = valid.astype(np.float32)[:, None] * np.float32(RET_HEAD ** -0.5)
    return tuple(jnp.asarray(t) for t in (ca, sa, c2, s2, c2 * kscale, s2 * kscale))


def _mla_head_layout(a):
    q = QK_ROPE // 2
    z = jnp.zeros(a.shape[:-1] + (q,), a.dtype)
    return jnp.concatenate([a[..., :HALF], a[..., QK_NOPE:QK_NOPE + q], z,
                            a[..., HALF:QK_NOPE], a[..., QK_NOPE + q:], z], axis=-1)


def _kpe_layout(a):
    q = QK_ROPE // 2
    z64 = jnp.zeros(a.shape[:-1] + (HALF,), a.dtype)
    z32 = jnp.zeros(a.shape[:-1] + (q,), a.dtype)
    return jnp.concatenate([z64, a[..., :q], z32, z64, a[..., q:], z32], axis=-1)


def _pair_layout(a):
    lead = a.shape[:-1]
    a = a.reshape(lead + (RET_HEADS // 2, 2, 2, HALF))
    a = jnp.swapaxes(a, -3, -2)
    return a.reshape(lead + (RET_WIDTH,))


def _prep_params(attn_norm_g, w_in, q_a_norm_g, w_q_b, kv_a_norm_g, w_kv_b, q_norm_g, k_norm_g,
                 mla_out_norm_g, ret_norm_g, ret_norm_b, w_out, ffn_norm_g, w_gate_up, w_down):
    depth = w_in.shape[0]
    o_kpe = Q_LORA + KV_LORA
    o_rq = o_kpe + QK_ROPE
    o_rk = o_rq + RET_WIDTH
    o_rv = o_rk + RET_WIDTH
    w_head = jnp.concatenate([w_in[..., :o_kpe], _kpe_layout(w_in[..., o_kpe:o_rq])],
                             axis=-1).astype(bf16)
    w_qk = jnp.concatenate([_pair_layout(w_in[..., o_rq:o_rk]), _pair_layout(w_in[..., o_rk:o_rv])],
                           axis=-1).astype(bf16)
    w_vg = w_in[..., o_rv:].astype(bf16)
    wq = _mla_head_layout(w_q_b.astype(bf16).reshape(depth, Q_LORA, MLA_HEADS, QK_HEAD))
    wkv = w_kv_b.astype(bf16).reshape(depth, KV_LORA, MLA_HEADS, QK_NOPE + V_HEAD)
    z64 = jnp.zeros((depth, KV_LORA, MLA_HEADS, HALF), bf16)
    wk = jnp.concatenate([wkv[..., :HALF], z64, wkv[..., HALF:QK_NOPE], z64], axis=-1)
    wkv = jnp.concatenate([wk.reshape(depth, KV_LORA, MLA_HEADS * QK_PAD),
                           wkv[..., QK_NOPE:].reshape(depth, KV_LORA, MLA_WIDTH)], axis=-1)
    scale = QK_HEAD ** -0.5 * LOG2_E
    vec = lambda a: a[:, None, :]
    return {
        "attn_g": vec(attn_norm_g),
        "w_head": w_head,
        "w_qk": w_qk,
        "w_vg": w_vg,
        "qa_g": vec(q_a_norm_g),
        "w_qb": wq.reshape(depth, Q_LORA, MLA_HEADS * QK_PAD),
        "kva_g": vec(kv_a_norm_g),
        "w_kvb": wkv,
        "q_g": vec(_mla_head_layout(q_norm_g * scale)),
        "k_g": vec(_mla_head_layout(k_norm_g)),
        "out_g": vec(mla_out_norm_g),
        "ret_g": vec(ret_norm_g),
        "ret_b": vec(ret_norm_b),
        "w_out": w_out.astype(bf16),
        "ffn_g": vec(ffn_norm_g),
        "w_gu": w_gate_up.astype(bf16),
        "w_down": w_down.astype(bf16),
    }


def kernel(x, meta_tokens, attn_norm_g, w_in, q_a_norm_g, w_q_b, kv_a_norm_g, w_kv_b, q_norm_g,
           k_norm_g, mla_out_norm_g, ret_norm_g, ret_norm_b, w_out, ffn_norm_g, w_gate_up, w_down):
    depth = w_in.shape[0]
    hx = x
    hm = jnp.concatenate([jnp.zeros((PAD_ROWS, D_MODEL), x.dtype), meta_tokens.astype(x.dtype)])[None]
    r = np.arange(BLOCK)
    tabs_x = _rope_tables(np.arange(SEQ) + N_META, np.ones(SEQ))
    tabs_m = _rope_tables(np.maximum(r - PAD_ROWS, 0), r >= PAD_ROWS)
    gamma = np.float32(1.0) - np.float32(2.0) ** (np.float32(-5.0) - np.arange(RET_HEADS, dtype=np.float32))
    log_g = jnp.asarray(np.broadcast_to(np.log(gamma)[:, None, None], (RET_HEADS, 8, LANES)))
    p = _prep_params(attn_norm_g, w_in, q_a_norm_g, w_q_b, kv_a_norm_g, w_kv_b, q_norm_g, k_norm_g,
                     mla_out_norm_g, ret_norm_g, ret_norm_b, w_out, ffn_norm_g, w_gate_up, w_down)
    for l in range(depth):
        q, k, v, rq, rk, rv, rg = _pre_call(hx, p, l, tabs_x, PRE_TM, "pre")
        qm, km, vm, rqm, rkm, rvm, rgm = _pre_call(hm, p, l, tabs_m, BLOCK, "pre_meta")
        y_mla = _attn_call(q, k, v, km, vm, p["out_g"], l)
        y_ret = _ret_call(rq, rk, rv, rg, rkm, rvm, log_g, p["ret_g"], p["ret_b"], l)
        hx = _post_call(hx, y_mla, y_ret, p, l, ROW_TM, "post")
        if l + 1 < depth:
            ym_mla = _attn_meta_call(qm, km, vm, p["out_g"], l)
            ym_ret = _ret_meta_call(rqm, rkm, rvm, rgm, log_g, p["ret_g"], p["ret_b"], l)
            hm = _post_call(hm, ym_mla, ym_ret, p, l, BLOCK, "post_meta")
    return hx
```

```python
import numpy as np

import jax
import jax.numpy as jnp
from jax import lax
from jax.experimental import pallas as pl
from jax.experimental.pallas import tpu as pltpu

D_MODEL = 1024
SEQ = 2048
N_META = 16
BLOCK = 128
MLA_HEADS = 4
Q_LORA = 256
KV_LORA = 256
QK_NOPE = 128
QK_ROPE = 64
QK_HEAD = QK_NOPE + QK_ROPE
V_HEAD = 128
MLA_WIDTH = MLA_HEADS * V_HEAD
RET_HEADS = 4
RET_HEAD = 128
RET_WIDTH = RET_HEADS * RET_HEAD
D_FF = 2816
ROPE_BASE = 10000.0
EPS = 1e-6
NEG_INF = -1e30

LANES = 128
HALF = LANES // 2
PAD_ROWS = BLOCK - N_META
QK_PAD = 2 * LANES
KV_WIDTH = MLA_HEADS * QK_PAD + MLA_WIDTH
RET_PAIR = 2 * RET_HEAD

C_CQ = 0
C_CKV = C_CQ + Q_LORA
C_KPE = C_CKV + KV_LORA
C_RQ = C_KPE + 2 * LANES
C_RK = C_RQ + RET_WIDTH
C_RV = C_RK + RET_WIDTH
C_RG = C_RV + RET_WIDTH
N_IN_PAD = C_RG + RET_WIDTH

ROW_TM = 1024
PRE_TM = 1024
ATT_TQ = 512
ATT_ORDER = (0, 1, 2, 3)
ATT_HPS = 4
ATT_AHEAD = 2
RET_CHUNK = 256
FF_TILE = 256
assert D_FF % FF_TILE == 0
LOG2_E = 1.4426950408889634
VMEM_LIMIT = 56 * 1024 * 1024

f32 = jnp.float32
bf16 = jnp.bfloat16


def _dot(a, b):
    return jnp.dot(a, b, preferred_element_type=f32)


def _dot_nt(a, b):
    return lax.dot_general(a, b, (((1,), (1,)), ((), ())), preferred_element_type=f32)


def _dot_tn(a, b):
    return lax.dot_general(a, b, (((0,), (0,)), ((), ())), preferred_element_type=f32)


def _rms(x, inv_n):
    return lax.rsqrt(jnp.sum(x * x, axis=-1, keepdims=True) * inv_n + EPS)


def _sigmoid(x):
    return 1.0 / (1.0 + jnp.exp(-x))


def _rot(a, b, c, s):
    return a * c - b * s, b * c + a * s


def _params(n_grid_axes):
    return pltpu.CompilerParams(dimension_semantics=("parallel",) * n_grid_axes,
                                vmem_limit_bytes=VMEM_LIMIT)


def _layer_spec(l, shape):
    nd = len(shape)
    return pl.BlockSpec((None,) + shape, lambda *_: (l,) + (0,) * nd, pipeline_mode=pl.Buffered(1))


def _pre_kernel(h_ref, g_ref, whead_ref, wqk_ref, wvg_ref, qag_ref, wqb_ref, kvag_ref, wkvb_ref,
                qg_ref, kg_ref, ca_ref, sa_ref, c2_ref, s2_ref, ck_ref, sk_ref,
                q_out, k_out, v_out, rq_out, rk_out, rv_out, rg_out, hb_ref, cqn_ref, ckvn_ref,
                kpe_ref):
    qga, qgb = qg_ref[:, :LANES], qg_ref[:, LANES:]
    kga, kgb = kg_ref[:, :LANES], kg_ref[:, LANES:]
    G = 2 * LANES

    x = h_ref[...]
    hb_ref[...] = (x * _rms(x, 1.0 / D_MODEL) * g_ref[...]).astype(bf16)

    def proj(w_ref, lo):
        return _dot_nt(hb_ref[...], w_ref[lo:lo + G, :])

    cq = proj(whead_ref, C_CQ)
    cqn_ref[...] = (cq * _rms(cq, 1.0 / Q_LORA) * qag_ref[...]).astype(bf16)
    ckv = proj(whead_ref, C_CKV)
    ckvn_ref[...] = (ckv * _rms(ckv, 1.0 / KV_LORA) * kvag_ref[...]).astype(bf16)
    kpe_ref[...] = proj(whead_ref, C_KPE)

    def norm_rope(ab, ga, gb):
        a, b = ab[:, :LANES], ab[:, LANES:]
        r = lax.rsqrt(jnp.sum(a * a + b * b, axis=-1, keepdims=True) * (1.0 / QK_HEAD) + EPS)
        ao, bo = _rot(a * ga, b * gb, ca_ref[...], sa_ref[...])
        return (ao * r).astype(bf16), (bo * r).astype(bf16)

    def mla_pair(hp):
        lo2 = hp * 2 * QK_PAD
        q2h = _dot(cqn_ref[...], wqb_ref[:, lo2:lo2 + 2 * QK_PAD])
        k2h = _dot(ckvn_ref[...], wkvb_ref[:, lo2:lo2 + 2 * QK_PAD])
        for e in range(2):
            lo = lo2 + e * QK_PAD
            q_out[:, lo:lo + LANES], q_out[:, lo + LANES:lo + QK_PAD] = norm_rope(
                q2h[:, e * QK_PAD:(e + 1) * QK_PAD], qga, qgb)
            k_out[:, lo:lo + LANES], k_out[:, lo + LANES:lo + QK_PAD] = norm_rope(
                k2h[:, e * QK_PAD:(e + 1) * QK_PAD] + kpe_ref[...], kga, kgb)

    def ret_pairs(zq, zk):
        for p in range(RET_HEADS // 2):
            lo = p * RET_PAIR
            k1, k2 = _rot(zk[:, lo:lo + LANES], zk[:, lo + LANES:lo + RET_PAIR], ck_ref[...],
                          sk_ref[...])
            rk_out[:, lo:lo + LANES] = k1.astype(bf16)
            rk_out[:, lo + LANES:lo + RET_PAIR] = k2.astype(bf16)
            q1, q2 = _rot(zq[:, lo:lo + LANES], zq[:, lo + LANES:lo + RET_PAIR], c2_ref[...],
                          s2_ref[...])
            rq_out[:, lo:lo + LANES] = q1.astype(bf16)
            rq_out[:, lo + LANES:lo + RET_PAIR] = q2.astype(bf16)

    rv_out[...] = _dot_nt(hb_ref[...], wvg_ref[:RET_WIDTH, :]).astype(bf16)
    mla_pair(0)
    rg_out[...] = _dot_nt(hb_ref[...], wvg_ref[RET_WIDTH:, :]).astype(bf16)
    mla_pair(1)
    zk = _dot_nt(hb_ref[...], wqk_ref[RET_WIDTH:, :])
    zq = _dot_nt(hb_ref[...], wqk_ref[:RET_WIDTH, :])
    v_out[...] = _dot(ckvn_ref[...], wkvb_ref[:, MLA_HEADS * QK_PAD:]).astype(bf16)
    ret_pairs(zq, zk)


def _pre_call(h, p, l, tabs, tm, name):
    B, R, _ = h.shape
    row = lambda w: pl.BlockSpec((None, tm, w), lambda b, j: (b, j, 0))
    tab = pl.BlockSpec((tm, LANES), lambda b, j: (j, 0))
    in_specs = [
        row(D_MODEL),
        _layer_spec(l, (1, D_MODEL)),
        _layer_spec(l, (C_RQ, D_MODEL)),
        _layer_spec(l, (C_RV - C_RQ, D_MODEL)),
        _layer_spec(l, (N_IN_PAD - C_RV, D_MODEL)),
        _layer_spec(l, (1, Q_LORA)),
        _layer_spec(l, (Q_LORA, MLA_HEADS * QK_PAD)),
        _layer_spec(l, (1, KV_LORA)),
        _layer_spec(l, (KV_LORA, KV_WIDTH)),
        _layer_spec(l, (1, QK_PAD)),
        _layer_spec(l, (1, QK_PAD)),
    ] + [tab] * 6
    widths = (MLA_HEADS * QK_PAD, MLA_HEADS * QK_PAD, MLA_WIDTH, RET_WIDTH, RET_WIDTH, RET_WIDTH,
              RET_WIDTH)
    return pl.pallas_call(
        _pre_kernel, grid=(B, R // tm),
        in_specs=in_specs, out_specs=[row(w) for w in widths],
        out_shape=[jax.ShapeDtypeStruct((B, R, w), bf16) for w in widths],
        scratch_shapes=[pltpu.VMEM((tm, D_MODEL), bf16), pltpu.VMEM((tm, Q_LORA), bf16),
                        pltpu.VMEM((tm, KV_LORA), bf16), pltpu.VMEM((tm, 2 * LANES), f32)],
        name=name, compiler_params=_params(2),
    )(h, p["attn_g"], p["w_head"], p["w_qk"], p["w_vg"], p["qa_g"], p["w_qb"], p["kva_g"], p["w_kvb"],
      p["q_g"], p["k_g"], *tabs)


def _rowmax(s):
    return jnp.max(s, axis=-1, keepdims=True)


def _rowsum(p):
    return jnp.sum(p, axis=-1, keepdims=True)


def _attn_finish(acc, l, og):
    o = acc * (1.0 / l)
    return (o * _rms(o, 1.0 / V_HEAD) * og).astype(bf16)


def _meta_valid():
    return lax.broadcasted_iota(jnp.int32, (1, BLOCK), 1) >= PAD_ROWS


def _attn_kernel(q_ref, k_ref, v_ref, km_ref, vm_ref, og_ref, o_ref, vt_ref):
    for hh in range(ATT_HPS):
        vv = slice(hh * V_HEAD, (hh + 1) * V_HEAD)
        vt_ref[vv, 0:BLOCK] = vm_ref[:, vv].T
        vt_ref[vv, BLOCK:] = v_ref[:, vv].T
    key_valid = lax.broadcasted_iota(jnp.int32, (BLOCK, 1), 0) >= PAD_ROWS
    tri_t = (lax.broadcasted_iota(jnp.int32, (ATT_TQ, ATT_TQ), 0)
             <= lax.broadcasted_iota(jnp.int32, (ATT_TQ, ATT_TQ), 1))

    def scores(item):
        hh, i = item
        qk = slice(hh * QK_PAD, (hh + 1) * QK_PAD)
        lo, hi = i * ATT_TQ, (i + 1) * ATT_TQ
        q = q_ref[lo:hi, qk]
        s_m = jnp.where(key_valid, _dot_nt(km_ref[:, qk], q), NEG_INF)
        s_x = _dot_nt(k_ref[0:hi, qk], q)
        parts = [s_m] + ([s_x[:lo]] if lo else []) + [jnp.where(tri_t, s_x[lo:], NEG_INF)]
        return jnp.concatenate(parts, axis=0)

    items = [(hh, i) for i in ATT_ORDER for hh in range(ATT_HPS)]
    ahead = [scores(it) for it in items[:ATT_AHEAD]]
    for n, (hh, i) in enumerate(items):
        if n + ATT_AHEAD < len(items):
            ahead.append(scores(items[n + ATT_AHEAD]))
        s = ahead.pop(0)
        vv = slice(hh * V_HEAD, (hh + 1) * V_HEAD)
        lo, hi = i * ATT_TQ, (i + 1) * ATT_TQ
        p = jnp.exp2(s - jnp.max(s, axis=0, keepdims=True))
        l = jnp.sum(p, axis=0, keepdims=True)
        o_t = _dot(vt_ref[vv, 0:BLOCK + hi], p.astype(bf16)) * (1.0 / l)
        r = lax.rsqrt(jnp.sum(o_t * o_t, axis=0, keepdims=True) * (1.0 / V_HEAD) + EPS)
        o_ref[lo:hi, vv] = ((o_t * r).T * og_ref[:, vv]).astype(bf16)


def _attn_meta_kernel(q_ref, k_ref, v_ref, og_ref, o_ref):
    mask = (lax.broadcasted_iota(jnp.int32, (BLOCK, BLOCK), 0)
            >= lax.broadcasted_iota(jnp.int32, (BLOCK, BLOCK), 1)) & _meta_valid()
    s = jnp.where(mask, _dot_nt(q_ref[...], k_ref[...]), NEG_INF)
    p = jnp.exp2(s - _rowmax(s))
    o_ref[...] = _attn_finish(_dot(p.astype(bf16), v_ref[...]), _rowsum(p), og_ref[...])


def _attn_call(q, k, v, km, vm, out_g, l):
    B = q.shape[0]
    n = ATT_HPS
    qk_spec = pl.BlockSpec((None, SEQ, n * QK_PAD), lambda b, h: (b, 0, h))
    v_spec = pl.BlockSpec((None, SEQ, n * V_HEAD), lambda b, h: (b, 0, h))
    return pl.pallas_call(
        _attn_kernel, grid=(B, MLA_HEADS // n),
        in_specs=[qk_spec, qk_spec, v_spec,
                  pl.BlockSpec((None, BLOCK, n * QK_PAD), lambda b, h: (0, 0, h)),
                  pl.BlockSpec((None, BLOCK, n * V_HEAD), lambda b, h: (0, 0, h)),
                  pl.BlockSpec((None, 1, n * V_HEAD), lambda b, h: (l, 0, h))],
        out_specs=v_spec,
        out_shape=jax.ShapeDtypeStruct((B, SEQ, MLA_WIDTH), bf16),
        scratch_shapes=[pltpu.VMEM((n * V_HEAD, BLOCK + SEQ), bf16)],
        name="attn", compiler_params=_params(2),
    )(q, k, v, km, vm, out_g)


def _attn_meta_call(qm, km, vm, out_g, l):
    qk_spec = pl.BlockSpec((None, BLOCK, QK_PAD), lambda h: (0, 0, h))
    v_spec = pl.BlockSpec((None, BLOCK, V_HEAD), lambda h: (0, 0, h))
    return pl.pallas_call(
        _attn_meta_kernel, grid=(MLA_HEADS,),
        in_specs=[qk_spec, qk_spec, v_spec, pl.BlockSpec((None, 1, V_HEAD), lambda h: (l, 0, h))],
        out_specs=v_spec,
        out_shape=jax.ShapeDtypeStruct((1, BLOCK, MLA_WIDTH), bf16),
        name="attn_meta", compiler_params=_params(1),
    )(qm, km, vm, out_g)


def _ret_tables(lg_ref, C):
    lg = lg_ref[0:1, :]
    lgc = jnp.concatenate([lg] * (C // LANES), axis=1)
    ri = lax.broadcasted_iota(jnp.int32, (C, C), 0)
    ci = lax.broadcasted_iota(jnp.int32, (C, C), 1)
    diff = (ri - ci).astype(f32)
    decay = jnp.where(diff >= 0, jnp.exp(jnp.maximum(diff, 0.0) * lgc), 0.0)
    idx = lax.broadcasted_iota(jnp.int32, (C, RET_HEAD), 0).astype(f32)
    xi = jnp.exp((idx + 1.0) * lg)
    return lg, decay, idx, xi


def _ret_key_state(k, v, idx, lg):
    n = k.shape[0]
    zeta = jnp.exp((n - 1.0 - idx[:n]) * lg)
    vz = (v.astype(f32) * zeta).astype(bf16)
    return _dot_tn(vz, k)


def _ret_mix(q, e, k, v, decay, state_t, xi):
    lane = lax.broadcasted_iota(jnp.int32, (1, RET_PAIR), 1) % LANES
    q = jnp.where((lane < HALF) if e == 0 else (lane >= HALF), q, jnp.zeros_like(q))
    n = q.shape[0]
    o = _dot((_dot_nt(q, k) * decay[:n, :n]).astype(bf16), v)
    if state_t is not None:
        o = o + _dot_nt(q, state_t.astype(bf16)) * xi[:n]
    return o


def _ret_emit(o, g, ng, nb):
    mu = jnp.mean(o, axis=-1, keepdims=True)
    d = o - mu
    var = jnp.mean(d * d, axis=-1, keepdims=True)
    on = d * lax.rsqrt(var + EPS) * ng + nb
    g = g.astype(f32)
    return (g * _sigmoid(g) * on).astype(bf16)


def _ret_kernel(q_ref, k_ref, v_ref, g_ref, km_ref, vm_ref, lg_ref, ng_ref, nb_ref, o_ref):
    C = RET_CHUNK
    n_chunks = SEQ // C
    chunks = [slice(c * C, (c + 1) * C) for c in range(n_chunks)]
    heads = []
    for e in range(2):
        lg, decay, idx, xi = _ret_tables(lg_ref.at[e], C)
        hv = slice(e * RET_HEAD, (e + 1) * RET_HEAD)
        heads.append(dict(
            lg=lg, decay=decay, idx=idx, xi=xi, hv=hv, e=e,
            chunk_decay=jnp.exp(float(C) * jnp.concatenate([lg, lg], axis=1)),
            state_t=_ret_key_state(km_ref[...], vm_ref[:, hv], idx, lg), o_prev=None))
    for c, rows in enumerate(chunks):
        for hd in heads:
            hv = hd["hv"]
            o = _ret_mix(q_ref[rows, :], hd["e"], k_ref[rows, :], v_ref[rows, hv], hd["decay"],
                         hd["state_t"], hd["xi"])
            if c + 1 < n_chunks:
                hd["state_t"] = hd["state_t"] * hd["chunk_decay"] + _ret_key_state(
                    k_ref[rows, :], v_ref[rows, hv], hd["idx"], hd["lg"])
            if hd["o_prev"] is not None:
                o_ref[chunks[c - 1], hv] = _ret_emit(hd["o_prev"], g_ref[chunks[c - 1], hv],
                                                     ng_ref[:, hv], nb_ref[:, hv])
            hd["o_prev"] = o
    for hd in heads:
        hv = hd["hv"]
        o_ref[chunks[-1], hv] = _ret_emit(hd["o_prev"], g_ref[chunks[-1], hv], ng_ref[:, hv],
                                          nb_ref[:, hv])


def _ret_meta_kernel(q_ref, k_ref, v_ref, g_ref, lg_ref, ng_ref, nb_ref, o_ref):
    for e in range(2):
        hv = slice(e * RET_HEAD, (e + 1) * RET_HEAD)
        _, decay, _, _ = _ret_tables(lg_ref.at[e], BLOCK)
        o = _ret_mix(q_ref[...], e, k_ref[...], v_ref[:, hv], decay, None, None)
        o_ref[:, hv] = _ret_emit(o, g_ref[:, hv], ng_ref[:, hv], nb_ref[:, hv])


def _ret_call(rq, rk, rv, rg, rkm, rvm, log_g, norm_g, norm_b, l):
    B = rq.shape[0]
    pair = pl.BlockSpec((None, SEQ, RET_PAIR), lambda b, p: (b, 0, p))
    spec = pl.BlockSpec((None, SEQ, 2 * RET_HEAD), lambda b, p: (b, 0, p))
    vec = pl.BlockSpec((None, 1, 2 * RET_HEAD), lambda b, p: (l, 0, p))
    return pl.pallas_call(
        _ret_kernel, grid=(B, RET_HEADS // 2),
        in_specs=[pair, pair, spec, spec,
                  pl.BlockSpec((None, BLOCK, RET_PAIR), lambda b, p: (0, 0, p)),
                  pl.BlockSpec((None, BLOCK, 2 * RET_HEAD), lambda b, p: (0, 0, p)),
                  pl.BlockSpec((2, 8, LANES), lambda b, p: (p, 0, 0)), vec, vec],
        out_specs=spec,
        out_shape=jax.ShapeDtypeStruct((B, SEQ, RET_WIDTH), bf16),
        name="ret", compiler_params=_params(2),
    )(rq, rk, rv, rg, rkm, rvm, log_g, norm_g, norm_b)


def _ret_meta_call(rqm, rkm, rvm, rgm, log_g, norm_g, norm_b, l):
    spec = pl.BlockSpec((None, BLOCK, 2 * RET_HEAD), lambda p: (0, 0, p))
    vec = pl.BlockSpec((None, 1, 2 * RET_HEAD), lambda p: (l, 0, p))
    pair = pl.BlockSpec((None, BLOCK, RET_PAIR), lambda p: (0, 0, p))
    return pl.pallas_call(
        _ret_meta_kernel, grid=(RET_HEADS // 2,),
        in_specs=[pair, pair, spec, spec, pl.BlockSpec((2, 8, LANES), lambda p: (p, 0, 0)), vec, vec],
        out_specs=spec,
        out_shape=jax.ShapeDtypeStruct((1, BLOCK, RET_WIDTH), bf16),
        name="ret_meta", compiler_params=_params(1),
    )(rqm, rkm, rvm, rgm, log_g, norm_g, norm_b)


def _post_kernel(h_ref, ym_ref, yr_ref, wo_ref, fg_ref, wgu_ref, wd_ref, o_ref, act_ref):
    h1 = (h_ref[...] + _dot(ym_ref[...], wo_ref[0:MLA_WIDTH, :])
          + _dot(yr_ref[...], wo_ref[MLA_WIDTH:MLA_WIDTH + RET_WIDTH, :]))
    hf = (h1 * _rms(h1, 1.0 / D_MODEL) * fg_ref[...]).astype(bf16)
    for c in range(D_FF // FF_TILE):
        lo = c * FF_TILE
        gate = _dot(hf, wgu_ref[:, lo:lo + FF_TILE])
        up = _dot(hf, wgu_ref[:, D_FF + lo:D_FF + lo + FF_TILE])
        act_ref[:, lo:lo + FF_TILE] = (gate * _sigmoid(gate) * up).astype(bf16)
    o_ref[...] = h1 + _dot(act_ref[...], wd_ref[...])


def _post_call(h, ym, yr, p, l, tm, name):
    B, R, _ = h.shape
    row = lambda w: pl.BlockSpec((None, tm, w), lambda b, j: (b, j, 0))
    return pl.pallas_call(
        _post_kernel, grid=(B, R // tm),
        in_specs=[row(D_MODEL), row(MLA_WIDTH), row(RET_WIDTH),
                  _layer_spec(l, (MLA_WIDTH + RET_WIDTH, D_MODEL)),
                  _layer_spec(l, (1, D_MODEL)),
                  _layer_spec(l, (D_MODEL, 2 * D_FF)),
                  _layer_spec(l, (D_FF, D_MODEL))],
        out_specs=row(D_MODEL),
        out_shape=jax.ShapeDtypeStruct((B, R, D_MODEL), f32),
        scratch_shapes=[pltpu.VMEM((tm, D_FF), bf16)],
        name=name, compiler_params=_params(2),
    )(h, ym, yr, p["w_out"], p["ffn_g"], p["w_gu"], p["w_down"])


W_IN_COLS = 2 * Q_LORA + QK_ROPE + 4 * RET_WIDTH
O_KPE = Q_LORA + KV_LORA
O_RQ = O_KPE + QK_ROPE
O_RK = O_RQ + RET_WIDTH
O_RV = O_RK + RET_WIDTH
WPREP_COLS = 256


def _wprep_kernel(w_ref, head_ref, qk_ref, vg_ref):
    def put(dst_ref, dst, src, n):
        dst_ref[dst:dst + n, :] = w_ref[src:src + n, :].astype(bf16)

    q = QK_ROPE // 2
    head_ref[...] = jnp.zeros(head_ref.shape, bf16)
    put(head_ref, 0, 0, O_KPE)
    put(head_ref, C_KPE + HALF, O_KPE, q)
    put(head_ref, C_KPE + LANES + HALF, O_KPE + q, q)
    for s, src0 in enumerate((O_RQ, O_RK)):
        for p in range(RET_HEADS // 2):
            src, dst = src0 + p * RET_PAIR, s * RET_WIDTH + p * RET_PAIR
            put(qk_ref, dst, src, HALF)
            put(qk_ref, dst + HALF, src + 2 * HALF, HALF)
            put(qk_ref, dst + 2 * HALF, src + HALF, HALF)
            put(qk_ref, dst + 3 * HALF, src + 3 * HALF, HALF)
    put(vg_ref, 0, O_RV, 2 * RET_WIDTH)


def _wprep_call(w_in):
    depth = w_in.shape[0]
    wt = jnp.swapaxes(w_in, 1, 2)
    cb = WPREP_COLS
    out = lambda n: pl.BlockSpec((None, n, cb), lambda l, c: (l, 0, c))
    rows = (C_RQ, C_RV - C_RQ, N_IN_PAD - C_RV)
    return pl.pallas_call(
        _wprep_kernel, grid=(depth, D_MODEL // cb),
        in_specs=[out(W_IN_COLS)], out_specs=[out(n) for n in rows],
        out_shape=[jax.ShapeDtypeStruct((depth, n, D_MODEL), bf16) for n in rows],
        name="wprep", compiler_params=_params(2),
    )(wt)


def _rope_tables(pos, valid):
    pos = pos.astype(np.float32)
    n = pos.shape[0]

    def cs(dim):
        inv = np.float32(ROPE_BASE) ** (-np.arange(0, dim, 2, dtype=np.float32) / np.float32(dim))
        ang = pos[:, None] * inv[None, :].astype(np.float32)
        return np.cos(ang).astype(np.float32), np.sin(ang).astype(np.float32)

    cm, sm = cs(QK_ROPE)
    z32 = np.zeros_like(cm)
    ca = np.concatenate([np.ones((n, HALF), np.float32), cm, z32], axis=1)
    sa = np.concatenate([np.zeros((n, HALF), np.float32), sm, z32], axis=1)
    cr, sr = cs(RET_HEAD)
    c2 = np.concatenate([cr, cr], axis=1)
    s2 = np.concatenate([sr, sr], axis=1)
    kscale = valid.astype(np.float32)[:, None] * np.float32(RET_HEAD ** -0.5)
    return tuple(jnp.asarray(t) for t in (ca, sa, c2, s2, c2 * kscale, s2 * kscale))


def _mla_head_layout(a):
    q = QK_ROPE // 2
    z = jnp.zeros(a.shape[:-1] + (q,), a.dtype)
    return jnp.concatenate([a[..., :HALF], a[..., QK_NOPE:QK_NOPE + q], z,
                            a[..., HALF:QK_NOPE], a[..., QK_NOPE + q:], z], axis=-1)


def _prep_params(attn_norm_g, w_in, q_a_norm_g, w_q_b, kv_a_norm_g, w_kv_b, q_norm_g, k_norm_g,
                 mla_out_norm_g, ret_norm_g, ret_norm_b, w_out, ffn_norm_g, w_gate_up, w_down):
    depth = w_in.shape[0]
    w_head, w_qk, w_vg = _wprep_call(w_in)
    wq = _mla_head_layout(w_q_b.astype(bf16).reshape(depth, Q_LORA, MLA_HEADS, QK_HEAD))
    wkv = w_kv_b.astype(bf16).reshape(depth, KV_LORA, MLA_HEADS, QK_NOPE + V_HEAD)
    z64 = jnp.zeros((depth, KV_LORA, MLA_HEADS, HALF), bf16)
    wk = jnp.concatenate([wkv[..., :HALF], z64, wkv[..., HALF:QK_NOPE], z64], axis=-1)
    wkv = jnp.concatenate([wk.reshape(depth, KV_LORA, MLA_HEADS * QK_PAD),
                           wkv[..., QK_NOPE:].reshape(depth, KV_LORA, MLA_WIDTH)], axis=-1)
    scale = QK_HEAD ** -0.5 * LOG2_E
    vec = lambda a: a[:, None, :]
    return {
        "attn_g": vec(attn_norm_g),
        "w_head": w_head,
        "w_qk": w_qk,
        "w_vg": w_vg,
        "qa_g": vec(q_a_norm_g),
        "w_qb": wq.reshape(depth, Q_LORA, MLA_HEADS * QK_PAD),
        "kva_g": vec(kv_a_norm_g),
        "w_kvb": wkv,
        "q_g": vec(_mla_head_layout(q_norm_g * scale)),
        "k_g": vec(_mla_head_layout(k_norm_g)),
        "out_g": vec(mla_out_norm_g),
        "ret_g": vec(ret_norm_g),
        "ret_b": vec(ret_norm_b),
        "w_out": w_out.astype(bf16),
        "ffn_g": vec(ffn_norm_g),
        "w_gu": w_gate_up.astype(bf16),
        "w_down": w_down.astype(bf16),
    }


def kernel(x, meta_tokens, attn_norm_g, w_in, q_a_norm_g, w_q_b, kv_a_norm_g, w_kv_b, q_norm_g,
           k_norm_g, mla_out_norm_g, ret_norm_g, ret_norm_b, w_out, ffn_norm_g, w_gate_up, w_down):
    depth = w_in.shape[0]
    hx = x
    hm = jnp.concatenate([jnp.zeros((PAD_ROWS, D_MODEL), x.dtype), meta_tokens.astype(x.dtype)])[None]
    r = np.arange(BLOCK)
    tabs_x = _rope_tables(np.arange(SEQ) + N_META, np.ones(SEQ))
    tabs_m = _rope_tables(np.maximum(r - PAD_ROWS, 0), r >= PAD_ROWS)
    gamma = np.float32(1.0) - np.float32(2.0) ** (np.float32(-5.0) - np.arange(RET_HEADS, dtype=np.float32))
    log_g = jnp.asarray(np.broadcast_to(np.log(gamma)[:, None, None], (RET_HEADS, 8, LANES)))
    p = _prep_params(attn_norm_g, w_in, q_a_norm_g, w_q_b, kv_a_norm_g, w_kv_b, q_norm_g, k_norm_g,
                     mla_out_norm_g, ret_norm_g, ret_norm_b, w_out, ffn_norm_g, w_gate_up, w_down)
    for l in range(depth):
        q, k, v, rq, rk, rv, rg = _pre_call(hx, p, l, tabs_x, PRE_TM, "pre")
        qm, km, vm, rqm, rkm, rvm, rgm = _pre_call(hm, p, l, tabs_m, BLOCK, "pre_meta")
        y_mla = _attn_call(q, k, v, km, vm, p["out_g"], l)
        y_ret = _ret_call(rq, rk, rv, rg, rkm, rvm, log_g, p["ret_g"], p["ret_b"], l)
        hx = _post_call(hx, y_mla, y_ret, p, l, ROW_TM, "post")
        if l + 1 < depth:
            ym_mla = _attn_meta_call(qm, km, vm, p["out_g"], l)
            ym_ret = _ret_meta_call(rqm, rkm, rvm, rgm, log_g, p["ret_g"], p["ret_b"], l)
            hm = _post_call(hm, ym_mla, ym_ret, p, l, BLOCK, "post_meta")
    return hx
```

```python
import numpy as np

import jax
import jax.numpy as jnp
from jax import lax
from jax.experimental import pallas as pl
from jax.experimental.pallas import tpu as pltpu

D_MODEL = 1024
SEQ = 2048
N_META = 16
BLOCK = 128
MLA_HEADS = 4
Q_LORA = 256
KV_LORA = 256
QK_NOPE = 128
QK_ROPE = 64
QK_HEAD = QK_NOPE + QK_ROPE
V_HEAD = 128
MLA_WIDTH = MLA_HEADS * V_HEAD
RET_HEADS = 4
RET_HEAD = 128
RET_WIDTH = RET_HEADS * RET_HEAD
D_FF = 2816
ROPE_BASE = 10000.0
EPS = 1e-6
NEG_INF = -1e30

LANES = 128
HALF = LANES // 2
PAD_ROWS = BLOCK - N_META
QK_PAD = 2 * LANES
KV_WIDTH = MLA_HEADS * QK_PAD + MLA_WIDTH
RET_PAIR = 2 * RET_HEAD

C_CQ = 0
C_CKV = C_CQ + Q_LORA
C_KPE = C_CKV + KV_LORA
C_RQ = C_KPE + 2 * LANES
C_RK = C_RQ + RET_WIDTH
C_RV = C_RK + RET_WIDTH
C_RG = C_RV + RET_WIDTH
N_IN_PAD = C_RG + RET_WIDTH

ROW_TM = 1024
PRE_TM = 1024
ATT_TQ = 512
ATT_ORDER = (0, 1, 2, 3)
ATT_HPS = 4
ATT_AHEAD = 2
RET_CHUNK = 256
FF_TILE = 256
assert D_FF % FF_TILE == 0
LOG2_E = 1.4426950408889634
VMEM_LIMIT = 56 * 1024 * 1024

f32 = jnp.float32
bf16 = jnp.bfloat16


def _dot(a, b):
    return jnp.dot(a, b, preferred_element_type=f32)


def _dot_nt(a, b):
    return lax.dot_general(a, b, (((1,), (1,)), ((), ())), preferred_element_type=f32)


def _dot_tn(a, b):
    return lax.dot_general(a, b, (((0,), (0,)), ((), ())), preferred_element_type=f32)


def _rms(x, inv_n):
    return lax.rsqrt(jnp.sum(x * x, axis=-1, keepdims=True) * inv_n + EPS)


def _sigmoid(x):
    return 1.0 / (1.0 + jnp.exp(-x))


def _rot(a, b, c, s):
    return a * c - b * s, b * c + a * s


def _params(n_grid_axes):
    return pltpu.CompilerParams(dimension_semantics=("parallel",) * n_grid_axes,
                                vmem_limit_bytes=VMEM_LIMIT)


G_ATTN = 0
G_FFN = G_ATTN + D_MODEL
G_OUT = G_FFN + D_MODEL
G_RETG = G_OUT + MLA_WIDTH
G_RETB = G_RETG + RET_WIDTH
G_QA = G_RETB + RET_WIDTH
G_KVA = G_QA + Q_LORA
G_Q = G_KVA + KV_LORA
G_K = G_Q + QK_PAD
G_TOTAL = G_K + QK_PAD


def _gain_spec(l, off, width):
    assert off % width == 0
    return pl.BlockSpec((None, 1, width), lambda *_: (l, 0, off // width),
                        pipeline_mode=pl.Buffered(1))


def _layer_spec(l, shape):
    nd = len(shape)
    return pl.BlockSpec((None,) + shape, lambda *_: (l,) + (0,) * nd, pipeline_mode=pl.Buffered(1))


def _pre_kernel(h_ref, g_ref, whead_ref, wqk_ref, wvg_ref, qag_ref, wqb_ref, kvag_ref, wkvb_ref,
                qg_ref, kg_ref, ca_ref, sa_ref, c2_ref, s2_ref, ck_ref, sk_ref,
                q_out, k_out, v_out, rq_out, rk_out, rv_out, rg_out, hb_ref, cqn_ref, ckvn_ref,
                kpe_ref):
    qga, qgb = qg_ref[:, :LANES], qg_ref[:, LANES:]
    kga, kgb = kg_ref[:, :LANES], kg_ref[:, LANES:]
    G = 2 * LANES

    x = h_ref[...]
    hb_ref[...] = (x * _rms(x, 1.0 / D_MODEL) * g_ref[...]).astype(bf16)

    def proj(w_ref, lo):
        return _dot_nt(hb_ref[...], w_ref[lo:lo + G, :])

    cq = proj(whead_ref, C_CQ)
    cqn_ref[...] = (cq * _rms(cq, 1.0 / Q_LORA) * qag_ref[...]).astype(bf16)
    ckv = proj(whead_ref, C_CKV)
    ckvn_ref[...] = (ckv * _rms(ckv, 1.0 / KV_LORA) * kvag_ref[...]).astype(bf16)
    kpe_ref[...] = proj(whead_ref, C_KPE)

    def norm_rope(ab, ga, gb):
        a, b = ab[:, :LANES], ab[:, LANES:]
        r = lax.rsqrt(jnp.sum(a * a + b * b, axis=-1, keepdims=True) * (1.0 / QK_HEAD) + EPS)
        ao, bo = _rot(a * ga, b * gb, ca_ref[...], sa_ref[...])
        return (ao * r).astype(bf16), (bo * r).astype(bf16)

    def mla_pair(hp):
        lo2 = hp * 2 * QK_PAD
        q2h = _dot(cqn_ref[...], wqb_ref[:, lo2:lo2 + 2 * QK_PAD])
        k2h = _dot(ckvn_ref[...], wkvb_ref[:, lo2:lo2 + 2 * QK_PAD])
        for e in range(2):
            lo = lo2 + e * QK_PAD
            q_out[:, lo:lo + LANES], q_out[:, lo + LANES:lo + QK_PAD] = norm_rope(
                q2h[:, e * QK_PAD:(e + 1) * QK_PAD], qga, qgb)
            k_out[:, lo:lo + LANES], k_out[:, lo + LANES:lo + QK_PAD] = norm_rope(
                k2h[:, e * QK_PAD:(e + 1) * QK_PAD] + kpe_ref[...], kga, kgb)

    def ret_pairs(zq, zk):
        for p in range(RET_HEADS // 2):
            lo = p * RET_PAIR
            k1, k2 = _rot(zk[:, lo:lo + LANES], zk[:, lo + LANES:lo + RET_PAIR], ck_ref[...],
                          sk_ref[...])
            rk_out[:, lo:lo + LANES] = k1.astype(bf16)
            rk_out[:, lo + LANES:lo + RET_PAIR] = k2.astype(bf16)
            q1, q2 = _rot(zq[:, lo:lo + LANES], zq[:, lo + LANES:lo + RET_PAIR], c2_ref[...],
                          s2_ref[...])
            rq_out[:, lo:lo + LANES] = q1.astype(bf16)
            rq_out[:, lo + LANES:lo + RET_PAIR] = q2.astype(bf16)

    rv_out[...] = _dot_nt(hb_ref[...], wvg_ref[:RET_WIDTH, :]).astype(bf16)
    mla_pair(0)
    rg_out[...] = _dot_nt(hb_ref[...], wvg_ref[RET_WIDTH:, :]).astype(bf16)
    mla_pair(1)
    zk = _dot_nt(hb_ref[...], wqk_ref[RET_WIDTH:, :])
    zq = _dot_nt(hb_ref[...], wqk_ref[:RET_WIDTH, :])
    v_out[...] = _dot(ckvn_ref[...], wkvb_ref[:, MLA_HEADS * QK_PAD:]).astype(bf16)
    ret_pairs(zq, zk)


def _pre_call(h, p, l, tabs, tm, name):
    B, R, _ = h.shape
    row = lambda w: pl.BlockSpec((None, tm, w), lambda b, j: (b, j, 0))
    tab = pl.BlockSpec((tm, LANES), lambda b, j: (j, 0))
    in_specs = [
        row(D_MODEL),
        _gain_spec(l, G_ATTN, D_MODEL),
        _layer_spec(l, (C_RQ, D_MODEL)),
        _layer_spec(l, (C_RV - C_RQ, D_MODEL)),
        _layer_spec(l, (N_IN_PAD - C_RV, D_MODEL)),
        _gain_spec(l, G_QA, Q_LORA),
        _layer_spec(l, (Q_LORA, MLA_HEADS * QK_PAD)),
        _gain_spec(l, G_KVA, KV_LORA),
        _layer_spec(l, (KV_LORA, KV_WIDTH)),
        _gain_spec(l, G_Q, QK_PAD),
        _gain_spec(l, G_K, QK_PAD),
    ] + [tab] * 6
    widths = (MLA_HEADS * QK_PAD, MLA_HEADS * QK_PAD, MLA_WIDTH, RET_WIDTH, RET_WIDTH, RET_WIDTH,
              RET_WIDTH)
    return pl.pallas_call(
        _pre_kernel, grid=(B, R // tm),
        in_specs=in_specs, out_specs=[row(w) for w in widths],
        out_shape=[jax.ShapeDtypeStruct((B, R, w), bf16) for w in widths],
        scratch_shapes=[pltpu.VMEM((tm, D_MODEL), bf16), pltpu.VMEM((tm, Q_LORA), bf16),
                        pltpu.VMEM((tm, KV_LORA), bf16), pltpu.VMEM((tm, 2 * LANES), f32)],
        name=name, compiler_params=_params(2),
    )(h, p["gains"], p["w_head"], p["w_qk"], p["w_vg"], p["gains"], p["w_qb"], p["gains"], p["w_kvb"],
      p["gains"], p["gains"], *tabs)


def _rowmax(s):
    return jnp.max(s, axis=-1, keepdims=True)


def _rowsum(p):
    return jnp.sum(p, axis=-1, keepdims=True)


def _attn_finish(acc, l, og):
    o = acc * (1.0 / l)
    return (o * _rms(o, 1.0 / V_HEAD) * og).astype(bf16)


def _meta_valid():
    return lax.broadcasted_iota(jnp.int32, (1, BLOCK), 1) >= PAD_ROWS


def _attn_kernel(q_ref, k_ref, v_ref, km_ref, vm_ref, og_ref, o_ref, vt_ref):
    for hh in range(ATT_HPS):
        vv = slice(hh * V_HEAD, (hh + 1) * V_HEAD)
        vt_ref[vv, 0:BLOCK] = vm_ref[:, vv].T
        vt_ref[vv, BLOCK:] = v_ref[:, vv].T
    key_valid = lax.broadcasted_iota(jnp.int32, (BLOCK, 1), 0) >= PAD_ROWS
    tri_t = (lax.broadcasted_iota(jnp.int32, (ATT_TQ, ATT_TQ), 0)
             <= lax.broadcasted_iota(jnp.int32, (ATT_TQ, ATT_TQ), 1))

    def scores(item):
        hh, i = item
        qk = slice(hh * QK_PAD, (hh + 1) * QK_PAD)
        lo, hi = i * ATT_TQ, (i + 1) * ATT_TQ
        q = q_ref[lo:hi, qk]
        s_m = jnp.where(key_valid, _dot_nt(km_ref[:, qk], q), NEG_INF)
        s_x = _dot_nt(k_ref[0:hi, qk], q)
        parts = [s_m] + ([s_x[:lo]] if lo else []) + [jnp.where(tri_t, s_x[lo:], NEG_INF)]
        return jnp.concatenate(parts, axis=0)

    items = [(hh, i) for i in ATT_ORDER for hh in range(ATT_HPS)]
    ahead = [scores(it) for it in items[:ATT_AHEAD]]
    for n, (hh, i) in enumerate(items):
        if n + ATT_AHEAD < len(items):
            ahead.append(scores(items[n + ATT_AHEAD]))
        s = ahead.pop(0)
        vv = slice(hh * V_HEAD, (hh + 1) * V_HEAD)
        lo, hi = i * ATT_TQ, (i + 1) * ATT_TQ
        p = jnp.exp2(s - jnp.max(s, axis=0, keepdims=True))
        l = jnp.sum(p, axis=0, keepdims=True)
        o_t = _dot(vt_ref[vv, 0:BLOCK + hi], p.astype(bf16)) * (1.0 / l)
        r = lax.rsqrt(jnp.sum(o_t * o_t, axis=0, keepdims=True) * (1.0 / V_HEAD) + EPS)
        o_ref[lo:hi, vv] = ((o_t * r).T * og_ref[:, vv]).astype(bf16)


def _attn_meta_kernel(q_ref, k_ref, v_ref, og_ref, o_ref):
    mask = (lax.broadcasted_iota(jnp.int32, (BLOCK, BLOCK), 0)
            >= lax.broadcasted_iota(jnp.int32, (BLOCK, BLOCK), 1)) & _meta_valid()
    s = jnp.where(mask, _dot_nt(q_ref[...], k_ref[...]), NEG_INF)
    p = jnp.exp2(s - _rowmax(s))
    o_ref[...] = _attn_finish(_dot(p.astype(bf16), v_ref[...]), _rowsum(p), og_ref[...])


def _attn_call(q, k, v, km, vm, out_g, l):
    B = q.shape[0]
    n = ATT_HPS
    qk_spec = pl.BlockSpec((None, SEQ, n * QK_PAD), lambda b, h: (b, 0, h))
    v_spec = pl.BlockSpec((None, SEQ, n * V_HEAD), lambda b, h: (b, 0, h))
    return pl.pallas_call(
        _attn_kernel, grid=(B, MLA_HEADS // n),
        in_specs=[qk_spec, qk_spec, v_spec,
                  pl.BlockSpec((None, BLOCK, n * QK_PAD), lambda b, h: (0, 0, h)),
                  pl.BlockSpec((None, BLOCK, n * V_HEAD), lambda b, h: (0, 0, h)),
                  pl.BlockSpec((None, 1, n * V_HEAD), lambda b, h: (l, 0, G_OUT // (n * V_HEAD) + h))],
        out_specs=v_spec,
        out_shape=jax.ShapeDtypeStruct((B, SEQ, MLA_WIDTH), bf16),
        scratch_shapes=[pltpu.VMEM((n * V_HEAD, BLOCK + SEQ), bf16)],
        name="attn", compiler_params=_params(2),
    )(q, k, v, km, vm, out_g)


def _attn_meta_call(qm, km, vm, out_g, l):
    qk_spec = pl.BlockSpec((None, BLOCK, QK_PAD), lambda h: (0, 0, h))
    v_spec = pl.BlockSpec((None, BLOCK, V_HEAD), lambda h: (0, 0, h))
    return pl.pallas_call(
        _attn_meta_kernel, grid=(MLA_HEADS,),
        in_specs=[qk_spec, qk_spec, v_spec,
                  pl.BlockSpec((None, 1, V_HEAD), lambda h: (l, 0, G_OUT // V_HEAD + h))],
        out_specs=v_spec,
        out_shape=jax.ShapeDtypeStruct((1, BLOCK, MLA_WIDTH), bf16),
        name="attn_meta", compiler_params=_params(1),
    )(qm, km, vm, out_g)


def _ret_tables(lg_ref, C):
    lg = lg_ref[0:1, :]
    lgc = jnp.concatenate([lg] * (C // LANES), axis=1)
    ri = lax.broadcasted_iota(jnp.int32, (C, C), 0)
    ci = lax.broadcasted_iota(jnp.int32, (C, C), 1)
    diff = (ri - ci).astype(f32)
    decay = jnp.where(diff >= 0, jnp.exp(jnp.maximum(diff, 0.0) * lgc), 0.0)
    idx = lax.broadcasted_iota(jnp.int32, (C, RET_HEAD), 0).astype(f32)
    xi = jnp.exp((idx + 1.0) * lg)
    return lg, decay, idx, xi


def _ret_key_state(k, v, idx, lg):
    n = k.shape[0]
    zeta = jnp.exp((n - 1.0 - idx[:n]) * lg)
    vz = (v.astype(f32) * zeta).astype(bf16)
    return _dot_tn(vz, k)


def _ret_mix(q, e, k, v, decay, state_t, xi):
    lane = lax.broadcasted_iota(jnp.int32, (1, RET_PAIR), 1) % LANES
    q = jnp.where((lane < HALF) if e == 0 else (lane >= HALF), q, jnp.zeros_like(q))
    n = q.shape[0]
    o = _dot((_dot_nt(q, k) * decay[:n, :n]).astype(bf16), v)
    if state_t is not None:
        o = o + _dot_nt(q, state_t.astype(bf16)) * xi[:n]
    return o


def _ret_emit(o, g, ng, nb):
    mu = jnp.mean(o, axis=-1, keepdims=True)
    d = o - mu
    var = jnp.mean(d * d, axis=-1, keepdims=True)
    on = d * lax.rsqrt(var + EPS) * ng + nb
    g = g.astype(f32)
    return (g * _sigmoid(g) * on).astype(bf16)


def _ret_kernel(q_ref, k_ref, v_ref, g_ref, km_ref, vm_ref, lg_ref, ng_ref, nb_ref, o_ref):
    C = RET_CHUNK
    n_chunks = SEQ // C
    chunks = [slice(c * C, (c + 1) * C) for c in range(n_chunks)]
    heads = []
    for e in range(2):
        lg, decay, idx, xi = _ret_tables(lg_ref.at[e], C)
        hv = slice(e * RET_HEAD, (e + 1) * RET_HEAD)
        heads.append(dict(
            lg=lg, decay=decay, idx=idx, xi=xi, hv=hv, e=e,
            chunk_decay=jnp.exp(float(C) * jnp.concatenate([lg, lg], axis=1)),
            state_t=_ret_key_state(km_ref[...], vm_ref[:, hv], idx, lg), o_prev=None))
    for c, rows in enumerate(chunks):
        for hd in heads:
            hv = hd["hv"]
            o = _ret_mix(q_ref[rows, :], hd["e"], k_ref[rows, :], v_ref[rows, hv], hd["decay"],
                         hd["state_t"], hd["xi"])
            if c + 1 < n_chunks:
                hd["state_t"] = hd["state_t"] * hd["chunk_decay"] + _ret_key_state(
                    k_ref[rows, :], v_ref[rows, hv], hd["idx"], hd["lg"])
            if hd["o_prev"] is not None:
                o_ref[chunks[c - 1], hv] = _ret_emit(hd["o_prev"], g_ref[chunks[c - 1], hv],
                                                     ng_ref[:, hv], nb_ref[:, hv])
            hd["o_prev"] = o
    for hd in heads:
        hv = hd["hv"]
        o_ref[chunks[-1], hv] = _ret_emit(hd["o_prev"], g_ref[chunks[-1], hv], ng_ref[:, hv],
                                          nb_ref[:, hv])


def _ret_meta_kernel(q_ref, k_ref, v_ref, g_ref, lg_ref, ng_ref, nb_ref, o_ref):
    for e in range(2):
        hv = slice(e * RET_HEAD, (e + 1) * RET_HEAD)
        _, decay, _, _ = _ret_tables(lg_ref.at[e], BLOCK)
        o = _ret_mix(q_ref[...], e, k_ref[...], v_ref[:, hv], decay, None, None)
        o_ref[:, hv] = _ret_emit(o, g_ref[:, hv], ng_ref[:, hv], nb_ref[:, hv])


def _ret_call(rq, rk, rv, rg, rkm, rvm, log_g, norm_g, norm_b, l):
    B = rq.shape[0]
    pair = pl.BlockSpec((None, SEQ, RET_PAIR), lambda b, p: (b, 0, p))
    spec = pl.BlockSpec((None, SEQ, 2 * RET_HEAD), lambda b, p: (b, 0, p))
    w = 2 * RET_HEAD
    vec_g = pl.BlockSpec((None, 1, w), lambda b, p: (l, 0, G_RETG // w + p))
    vec_b = pl.BlockSpec((None, 1, w), lambda b, p: (l, 0, G_RETB // w + p))
    return pl.pallas_call(
        _ret_kernel, grid=(B, RET_HEADS // 2),
        in_specs=[pair, pair, spec, spec,
                  pl.BlockSpec((None, BLOCK, RET_PAIR), lambda b, p: (0, 0, p)),
                  pl.BlockSpec((None, BLOCK, 2 * RET_HEAD), lambda b, p: (0, 0, p)),
                  pl.BlockSpec((2, 8, LANES), lambda b, p: (p, 0, 0)), vec_g, vec_b],
        out_specs=spec,
        out_shape=jax.ShapeDtypeStruct((B, SEQ, RET_WIDTH), bf16),
        name="ret", compiler_params=_params(2),
    )(rq, rk, rv, rg, rkm, rvm, log_g, norm_g, norm_b)


def _ret_meta_call(rqm, rkm, rvm, rgm, log_g, norm_g, norm_b, l):
    spec = pl.BlockSpec((None, BLOCK, 2 * RET_HEAD), lambda p: (0, 0, p))
    w = 2 * RET_HEAD
    vec_g = pl.BlockSpec((None, 1, w), lambda p: (l, 0, G_RETG // w + p))
    vec_b = pl.BlockSpec((None, 1, w), lambda p: (l, 0, G_RETB // w + p))
    pair = pl.BlockSpec((None, BLOCK, RET_PAIR), lambda p: (0, 0, p))
    return pl.pallas_call(
        _ret_meta_kernel, grid=(RET_HEADS // 2,),
        in_specs=[pair, pair, spec, spec, pl.BlockSpec((2, 8, LANES), lambda p: (p, 0, 0)),
                  vec_g, vec_b],
        out_specs=spec,
        out_shape=jax.ShapeDtypeStruct((1, BLOCK, RET_WIDTH), bf16),
        name="ret_meta", compiler_params=_params(1),
    )(rqm, rkm, rvm, rgm, log_g, norm_g, norm_b)


def _post_kernel(h_ref, ym_ref, yr_ref, wo_ref, fg_ref, wgu_ref, wd_ref, o_ref, act_ref):
    h1 = (h_ref[...] + _dot(ym_ref[...], wo_ref[0:MLA_WIDTH, :])
          + _dot(yr_ref[...], wo_ref[MLA_WIDTH:MLA_WIDTH + RET_WIDTH, :]))
    hf = (h1 * _rms(h1, 1.0 / D_MODEL) * fg_ref[...]).astype(bf16)
    for c in range(D_FF // FF_TILE):
        lo = c * FF_TILE
        gate = _dot(hf, wgu_ref[:, lo:lo + FF_TILE])
        up = _dot(hf, wgu_ref[:, D_FF + lo:D_FF + lo + FF_TILE])
        act_ref[:, lo:lo + FF_TILE] = (gate * _sigmoid(gate) * up).astype(bf16)
    o_ref[...] = h1 + _dot(act_ref[...], wd_ref[...])


def _post_call(h, ym, yr, p, l, tm, name):
    B, R, _ = h.shape
    row = lambda w: pl.BlockSpec((None, tm, w), lambda b, j: (b, j, 0))
    return pl.pallas_call(
        _post_kernel, grid=(B, R // tm),
        in_specs=[row(D_MODEL), row(MLA_WIDTH), row(RET_WIDTH),
                  _layer_spec(l, (MLA_WIDTH + RET_WIDTH, D_MODEL)),
                  _gain_spec(l, G_FFN, D_MODEL),
                  _layer_spec(l, (D_MODEL, 2 * D_FF)),
                  _layer_spec(l, (D_FF, D_MODEL))],
        out_specs=row(D_MODEL),
        out_shape=jax.ShapeDtypeStruct((B, R, D_MODEL), f32),
        scratch_shapes=[pltpu.VMEM((tm, D_FF), bf16)],
        name=name, compiler_params=_params(2),
    )(h, ym, yr, p["w_out"], p["gains"], p["w_gu"], p["w_down"])


W_IN_COLS = 2 * Q_LORA + QK_ROPE + 4 * RET_WIDTH
O_KPE = Q_LORA + KV_LORA
O_RQ = O_KPE + QK_ROPE
O_RK = O_RQ + RET_WIDTH
O_RV = O_RK + RET_WIDTH
WPREP_COLS = 256


def _wprep_kernel(w_ref, head_ref, qk_ref, vg_ref):
    def put(dst_ref, dst, src, n):
        dst_ref[dst:dst + n, :] = w_ref[src:src + n, :].astype(bf16)

    q = QK_ROPE // 2
    head_ref[...] = jnp.zeros(head_ref.shape, bf16)
    put(head_ref, 0, 0, O_KPE)
    put(head_ref, C_KPE + HALF, O_KPE, q)
    put(head_ref, C_KPE + LANES + HALF, O_KPE + q, q)
    for s, src0 in enumerate((O_RQ, O_RK)):
        for p in range(RET_HEADS // 2):
            src, dst = src0 + p * RET_PAIR, s * RET_WIDTH + p * RET_PAIR
            put(qk_ref, dst, src, HALF)
            put(qk_ref, dst + HALF, src + 2 * HALF, HALF)
            put(qk_ref, dst + 2 * HALF, src + HALF, HALF)
            put(qk_ref, dst + 3 * HALF, src + 3 * HALF, HALF)
    put(vg_ref, 0, O_RV, 2 * RET_WIDTH)


def _wprep_call(w_in):
    depth = w_in.shape[0]
    wt = jnp.swapaxes(w_in, 1, 2)
    cb = WPREP_COLS
    out = lambda n: pl.BlockSpec((None, n, cb), lambda l, c: (l, 0, c))
    rows = (C_RQ, C_RV - C_RQ, N_IN_PAD - C_RV)
    return pl.pallas_call(
        _wprep_kernel, grid=(depth, D_MODEL // cb),
        in_specs=[out(W_IN_COLS)], out_specs=[out(n) for n in rows],
        out_shape=[jax.ShapeDtypeStruct((depth, n, D_MODEL), bf16) for n in rows],
        name="wprep", compiler_params=_params(2),
    )(wt)


def _rope_tables(pos, valid):
    pos = pos.astype(np.float32)
    n = pos.shape[0]

    def cs(dim):
        inv = np.float32(ROPE_BASE) ** (-np.arange(0, dim, 2, dtype=np.float32) / np.float32(dim))
        ang = pos[:, None] * inv[None, :].astype(np.float32)
        return np.cos(ang).astype(np.float32), np.sin(ang).astype(np.float32)

    cm, sm = cs(QK_ROPE)
    z32 = np.zeros_like(cm)
    ca = np.concatenate([np.ones((n, HALF), np.float32), cm, z32], axis=1)
    sa = np.concatenate([np.zeros((n, HALF), np.float32), sm, z32], axis=1)
    cr, sr = cs(RET_HEAD)
    c2 = np.concatenate([cr, cr], axis=1)
    s2 = np.concatenate([sr, sr], axis=1)
    kscale = valid.astype(np.float32)[:, None] * np.float32(RET_HEAD ** -0.5)
    return tuple(jnp.asarray(t) for t in (ca, sa, c2, s2, c2 * kscale, s2 * kscale))


def _mla_head_layout(a):
    q = QK_ROPE // 2
    z = jnp.zeros(a.shape[:-1] + (q,), a.dtype)
    return jnp.concatenate([a[..., :HALF], a[..., QK_NOPE:QK_NOPE + q], z,
                            a[..., HALF:QK_NOPE], a[..., QK_NOPE + q:], z], axis=-1)


def _prep_params(attn_norm_g, w_in, q_a_norm_g, w_q_b, kv_a_norm_g, w_kv_b, q_norm_g, k_norm_g,
                 mla_out_norm_g, ret_norm_g, ret_norm_b, w_out, ffn_norm_g, w_gate_up, w_down):
    depth = w_in.shape[0]
    w_head, w_qk, w_vg = _wprep_call(w_in)
    wq = _mla_head_layout(w_q_b.astype(bf16).reshape(depth, Q_LORA, MLA_HEADS, QK_HEAD))
    wkv = w_kv_b.astype(bf16).reshape(depth, KV_LORA, MLA_HEADS, QK_NOPE + V_HEAD)
    z64 = jnp.zeros((depth, KV_LORA, MLA_HEADS, HALF), bf16)
    wk = jnp.concatenate([wkv[..., :HALF], z64, wkv[..., HALF:QK_NOPE], z64], axis=-1)
    wkv = jnp.concatenate([wk.reshape(depth, KV_LORA, MLA_HEADS * QK_PAD),
                           wkv[..., QK_NOPE:].reshape(depth, KV_LORA, MLA_WIDTH)], axis=-1)
    scale = QK_HEAD ** -0.5 * LOG2_E
    gains = jnp.concatenate(
        [attn_norm_g, ffn_norm_g, mla_out_norm_g, ret_norm_g, ret_norm_b, q_a_norm_g, kv_a_norm_g,
         _mla_head_layout(q_norm_g * scale), _mla_head_layout(k_norm_g)], axis=-1)[:, None, :]
    assert gains.shape[-1] == G_TOTAL
    return {
        "gains": gains,
        "w_head": w_head,
        "w_qk": w_qk,
        "w_vg": w_vg,
        "w_qb": wq.reshape(depth, Q_LORA, MLA_HEADS * QK_PAD),
        "w_kvb": wkv,
        "w_out": w_out.astype(bf16),
        "w_gu": w_gate_up.astype(bf16),
        "w_down": w_down.astype(bf16),
    }


def kernel(x, meta_tokens, attn_norm_g, w_in, q_a_norm_g, w_q_b, kv_a_norm_g, w_kv_b, q_norm_g,
           k_norm_g, mla_out_norm_g, ret_norm_g, ret_norm_b, w_out, ffn_norm_g, w_gate_up, w_down):
    depth = w_in.shape[0]
    hx = x
    hm = jnp.concatenate([jnp.zeros((PAD_ROWS, D_MODEL), x.dtype), meta_tokens.astype(x.dtype)])[None]
    r = np.arange(BLOCK)
    tabs_x = _rope_tables(np.arange(SEQ) + N_META, np.ones(SEQ))
    tabs_m = _rope_tables(np.maximum(r - PAD_ROWS, 0), r >= PAD_ROWS)
    gamma = np.float32(1.0) - np.float32(2.0) ** (np.float32(-5.0) - np.arange(RET_HEADS, dtype=np.float32))
    log_g = jnp.asarray(np.broadcast_to(np.log(gamma)[:, None, None], (RET_HEADS, 8, LANES)))
    p = _prep_params(attn_norm_g, w_in, q_a_norm_g, w_q_b, kv_a_norm_g, w_kv_b, q_norm_g, k_norm_g,
                     mla_out_norm_g, ret_norm_g, ret_norm_b, w_out, ffn_norm_g, w_gate_up, w_down)
    for l in range(depth):
        q, k, v, rq, rk, rv, rg = _pre_call(hx, p, l, tabs_x, PRE_TM, "pre")
        qm, km, vm, rqm, rkm, rvm, rgm = _pre_call(hm, p, l, tabs_m, BLOCK, "pre_meta")
        y_mla = _attn_call(q, k, v, km, vm, p["gains"], l)
        y_ret = _ret_call(rq, rk, rv, rg, rkm, rvm, log_g, p["gains"], p["gains"], l)
        hx = _post_call(hx, y_mla, y_ret, p, l, ROW_TM, "post")
        if l + 1 < depth:
            ym_mla = _attn_meta_call(qm, km, vm, p["gains"], l)
            ym_ret = _ret_meta_call(rqm, rkm, rvm, rgm, log_g, p["gains"], p["gains"], l)
            hm = _post_call(hm, ym_mla, ym_ret, p, l, BLOCK, "post_meta")
    return hx
```

```python
import numpy as np

import jax
import jax.numpy as jnp
from jax import lax
from jax.experimental import pallas as pl
from jax.experimental.pallas import tpu as pltpu

D_MODEL = 1024
SEQ = 2048
N_META = 16
BLOCK = 128
MLA_HEADS = 4
Q_LORA = 256
KV_LORA = 256
QK_NOPE = 128
QK_ROPE = 64
QK_HEAD = QK_NOPE + QK_ROPE
V_HEAD = 128
MLA_WIDTH = MLA_HEADS * V_HEAD
RET_HEADS = 4
RET_HEAD = 128
RET_WIDTH = RET_HEADS * RET_HEAD
D_FF = 2816
ROPE_BASE = 10000.0
EPS = 1e-6
NEG_INF = -1e30

LANES = 128
HALF = LANES // 2
PAD_ROWS = BLOCK - N_META
QK_PAD = 2 * LANES
KV_WIDTH = MLA_HEADS * QK_PAD + MLA_WIDTH
RET_PAIR = 2 * RET_HEAD

C_CQ = 0
C_CKV = C_CQ + Q_LORA
C_KPE = C_CKV + KV_LORA
C_RQ = C_KPE + 2 * LANES
C_RK = C_RQ + RET_WIDTH
C_RV = C_RK + RET_WIDTH
C_RG = C_RV + RET_WIDTH
N_IN_PAD = C_RG + RET_WIDTH

ROW_TM = 1024
PRE_TM = 1024
ATT_TQ = 512
ATT_ORDER = (0, 1, 2, 3)
ATT_HPS = 2
ATT_AHEAD = 2
RET_CHUNK = 256
RET_PPS = 2
FF_TILE = 256
assert D_FF % FF_TILE == 0
LOG2_E = 1.4426950408889634
VMEM_LIMIT = 56 * 1024 * 1024

f32 = jnp.float32
bf16 = jnp.bfloat16


def _dot(a, b):
    return jnp.dot(a, b, preferred_element_type=f32)


def _dot_nt(a, b):
    return lax.dot_general(a, b, (((1,), (1,)), ((), ())), preferred_element_type=f32)


def _dot_tn(a, b):
    return lax.dot_general(a, b, (((0,), (0,)), ((), ())), preferred_element_type=f32)


def _rms(x, inv_n):
    return lax.rsqrt(jnp.sum(x * x, axis=-1, keepdims=True) * inv_n + EPS)


def _sigmoid(x):
    return 1.0 / (1.0 + jnp.exp(-x))


def _rot(a, b, c, s):
    return a * c - b * s, b * c + a * s


def _params(n_grid_axes):
    return pltpu.CompilerParams(dimension_semantics=("parallel",) * n_grid_axes,
                                vmem_limit_bytes=VMEM_LIMIT)


G_ATTN = 0
G_FFN = G_ATTN + D_MODEL
G_OUT = G_FFN + D_MODEL
G_RETG = G_OUT + MLA_WIDTH
G_RETB = G_RETG + RET_WIDTH
G_QA = G_RETB + RET_WIDTH
G_KVA = G_QA + Q_LORA
G_Q = G_KVA + KV_LORA
G_K = G_Q + QK_PAD
G_TOTAL = G_K + QK_PAD


def _gain_spec(l, off, width):
    assert off % width == 0
    return pl.BlockSpec((None, 1, width), lambda *_: (l, 0, off // width),
                        pipeline_mode=pl.Buffered(1))


def _layer_spec(l, shape):
    nd = len(shape)
    return pl.BlockSpec((None,) + shape, lambda *_: (l,) + (0,) * nd, pipeline_mode=pl.Buffered(1))


def _pre_kernel(h_ref, g_ref, whead_ref, wqk_ref, wvg_ref, qag_ref, wqb_ref, kvag_ref, wkvb_ref,
                qg_ref, kg_ref, ca_ref, sa_ref, c2_ref, s2_ref, ck_ref, sk_ref,
                q_out, k_out, v_out, rq_out, rk_out, rv_out, rg_out, hb_ref, cqn_ref, ckvn_ref,
                kpe_ref):
    qga, qgb = qg_ref[:, :LANES], qg_ref[:, LANES:]
    kga, kgb = kg_ref[:, :LANES], kg_ref[:, LANES:]
    G = 2 * LANES

    x = h_ref[...]
    hb_ref[...] = (x * _rms(x, 1.0 / D_MODEL) * g_ref[...]).astype(bf16)

    def proj(w_ref, lo):
        return _dot_nt(hb_ref[...], w_ref[lo:lo + G, :])

    cq = proj(whead_ref, C_CQ)
    cqn_ref[...] = (cq * _rms(cq, 1.0 / Q_LORA) * qag_ref[...]).astype(bf16)
    ckv = proj(whead_ref, C_CKV)
    ckvn_ref[...] = (ckv * _rms(ckv, 1.0 / KV_LORA) * kvag_ref[...]).astype(bf16)
    kpe_ref[...] = proj(whead_ref, C_KPE)

    def norm_rope(ab, ga, gb):
        a, b = ab[:, :LANES], ab[:, LANES:]
        r = lax.rsqrt(jnp.sum(a * a + b * b, axis=-1, keepdims=True) * (1.0 / QK_HEAD) + EPS)
        ao, bo = _rot(a * ga, b * gb, ca_ref[...], sa_ref[...])
        return (ao * r).astype(bf16), (bo * r).astype(bf16)

    def mla_pair(hp):
        lo2 = hp * 2 * QK_PAD
        q2h = _dot(cqn_ref[...], wqb_ref[:, lo2:lo2 + 2 * QK_PAD])
        k2h = _dot(ckvn_ref[...], wkvb_ref[:, lo2:lo2 + 2 * QK_PAD])
        for e in range(2):
            lo = lo2 + e * QK_PAD
            q_out[:, lo:lo + LANES], q_out[:, lo + LANES:lo + QK_PAD] = norm_rope(
                q2h[:, e * QK_PAD:(e + 1) * QK_PAD], qga, qgb)
            k_out[:, lo:lo + LANES], k_out[:, lo + LANES:lo + QK_PAD] = norm_rope(
                k2h[:, e * QK_PAD:(e + 1) * QK_PAD] + kpe_ref[...], kga, kgb)

    def ret_pairs(zq, zk):
        for p in range(RET_HEADS // 2):
            lo = p * RET_PAIR
            k1, k2 = _rot(zk[:, lo:lo + LANES], zk[:, lo + LANES:lo + RET_PAIR], ck_ref[...],
                          sk_ref[...])
            rk_out[:, lo:lo + LANES] = k1.astype(bf16)
            rk_out[:, lo + LANES:lo + RET_PAIR] = k2.astype(bf16)
            q1, q2 = _rot(zq[:, lo:lo + LANES], zq[:, lo + LANES:lo + RET_PAIR], c2_ref[...],
                          s2_ref[...])
            rq_out[:, lo:lo + LANES] = q1.astype(bf16)
            rq_out[:, lo + LANES:lo + RET_PAIR] = q2.astype(bf16)

    rv_out[...] = _dot_nt(hb_ref[...], wvg_ref[:RET_WIDTH, :]).astype(bf16)
    mla_pair(0)
    rg_out[...] = _dot_nt(hb_ref[...], wvg_ref[RET_WIDTH:, :]).astype(bf16)
    mla_pair(1)
    zk = _dot_nt(hb_ref[...], wqk_ref[RET_WIDTH:, :])
    zq = _dot_nt(hb_ref[...], wqk_ref[:RET_WIDTH, :])
    v_out[...] = _dot(ckvn_ref[...], wkvb_ref[:, MLA_HEADS * QK_PAD:]).astype(bf16)
    ret_pairs(zq, zk)


def _pre_call(h, p, l, tabs, tm, name):
    B, R, _ = h.shape
    row = lambda w: pl.BlockSpec((None, tm, w), lambda b, j: (b, j, 0))
    tab = pl.BlockSpec((tm, LANES), lambda b, j: (j, 0))
    in_specs = [
        row(D_MODEL),
        _gain_spec(l, G_ATTN, D_MODEL),
        _layer_spec(l, (C_RQ, D_MODEL)),
        _layer_spec(l, (C_RV - C_RQ, D_MODEL)),
        _layer_spec(l, (N_IN_PAD - C_RV, D_MODEL)),
        _gain_spec(l, G_QA, Q_LORA),
        _layer_spec(l, (Q_LORA, MLA_HEADS * QK_PAD)),
        _gain_spec(l, G_KVA, KV_LORA),
        _layer_spec(l, (KV_LORA, KV_WIDTH)),
        _gain_spec(l, G_Q, QK_PAD),
        _gain_spec(l, G_K, QK_PAD),
    ] + [tab] * 6
    widths = (MLA_HEADS * QK_PAD, MLA_HEADS * QK_PAD, MLA_WIDTH, RET_WIDTH, RET_WIDTH, RET_WIDTH,
              RET_WIDTH)
    return pl.pallas_call(
        _pre_kernel, grid=(B, R // tm),
        in_specs=in_specs, out_specs=[row(w) for w in widths],
        out_shape=[jax.ShapeDtypeStruct((B, R, w), bf16) for w in widths],
        scratch_shapes=[pltpu.VMEM((tm, D_MODEL), bf16), pltpu.VMEM((tm, Q_LORA), bf16),
                        pltpu.VMEM((tm, KV_LORA), bf16), pltpu.VMEM((tm, 2 * LANES), f32)],
        name=name, compiler_params=_params(2),
    )(h, p["gains"], p["w_head"], p["w_qk"], p["w_vg"], p["gains"], p["w_qb"], p["gains"], p["w_kvb"],
      p["gains"], p["gains"], *tabs)


def _rowmax(s):
    return jnp.max(s, axis=-1, keepdims=True)


def _rowsum(p):
    return jnp.sum(p, axis=-1, keepdims=True)


def _attn_finish(acc, l, og):
    o = acc * (1.0 / l)
    return (o * _rms(o, 1.0 / V_HEAD) * og).astype(bf16)


def _meta_valid():
    return lax.broadcasted_iota(jnp.int32, (1, BLOCK), 1) >= PAD_ROWS


def _attn_kernel(q_ref, k_ref, v_ref, km_ref, vm_ref, og_ref, o_ref, vt_ref):
    for hh in range(ATT_HPS):
        vv = slice(hh * V_HEAD, (hh + 1) * V_HEAD)
        vt_ref[vv, 0:BLOCK] = vm_ref[:, vv].T
        vt_ref[vv, BLOCK:] = v_ref[:, vv].T
    key_valid = lax.broadcasted_iota(jnp.int32, (BLOCK, 1), 0) >= PAD_ROWS
    tri_t = (lax.broadcasted_iota(jnp.int32, (ATT_TQ, ATT_TQ), 0)
             <= lax.broadcasted_iota(jnp.int32, (ATT_TQ, ATT_TQ), 1))

    def scores(item):
        hh, i = item
        qk = slice(hh * QK_PAD, (hh + 1) * QK_PAD)
        lo, hi = i * ATT_TQ, (i + 1) * ATT_TQ
        q = q_ref[lo:hi, qk]
        s_m = jnp.where(key_valid, _dot_nt(km_ref[:, qk], q), NEG_INF)
        s_x = _dot_nt(k_ref[0:hi, qk], q)
        parts = [s_m] + ([s_x[:lo]] if lo else []) + [jnp.where(tri_t, s_x[lo:], NEG_INF)]
        return jnp.concatenate(parts, axis=0)

    items = [(hh, i) for i in ATT_ORDER for hh in range(ATT_HPS)]
    ahead = [scores(it) for it in items[:ATT_AHEAD]]
    for n, (hh, i) in enumerate(items):
        if n + ATT_AHEAD < len(items):
            ahead.append(scores(items[n + ATT_AHEAD]))
        s = ahead.pop(0)
        vv = slice(hh * V_HEAD, (hh + 1) * V_HEAD)
        lo, hi = i * ATT_TQ, (i + 1) * ATT_TQ
        p = jnp.exp2(s - jnp.max(s, axis=0, keepdims=True))
        l = jnp.sum(p, axis=0, keepdims=True)
        o_t = _dot(vt_ref[vv, 0:BLOCK + hi], p.astype(bf16)) * (1.0 / l)
        r = lax.rsqrt(jnp.sum(o_t * o_t, axis=0, keepdims=True) * (1.0 / V_HEAD) + EPS)
        o_ref[lo:hi, vv] = ((o_t * r).T * og_ref[:, vv]).astype(bf16)


def _attn_meta_kernel(q_ref, k_ref, v_ref, og_ref, o_ref):
    mask = (lax.broadcasted_iota(jnp.int32, (BLOCK, BLOCK), 0)
            >= lax.broadcasted_iota(jnp.int32, (BLOCK, BLOCK), 1)) & _meta_valid()
    s = jnp.where(mask, _dot_nt(q_ref[...], k_ref[...]), NEG_INF)
    p = jnp.exp2(s - _rowmax(s))
    o_ref[...] = _attn_finish(_dot(p.astype(bf16), v_ref[...]), _rowsum(p), og_ref[...])


def _attn_call(q, k, v, km, vm, out_g, l):
    B = q.shape[0]
    n = ATT_HPS
    qk_spec = pl.BlockSpec((None, SEQ, n * QK_PAD), lambda b, h: (b, 0, h))
    v_spec = pl.BlockSpec((None, SEQ, n * V_HEAD), lambda b, h: (b, 0, h))
    return pl.pallas_call(
        _attn_kernel, grid=(B, MLA_HEADS // n),
        in_specs=[qk_spec, qk_spec, v_spec,
                  pl.BlockSpec((None, BLOCK, n * QK_PAD), lambda b, h: (0, 0, h)),
                  pl.BlockSpec((None, BLOCK, n * V_HEAD), lambda b, h: (0, 0, h)),
                  pl.BlockSpec((None, 1, n * V_HEAD), lambda b, h: (l, 0, G_OUT // (n * V_HEAD) + h))],
        out_specs=v_spec,
        out_shape=jax.ShapeDtypeStruct((B, SEQ, MLA_WIDTH), bf16),
        scratch_shapes=[pltpu.VMEM((n * V_HEAD, BLOCK + SEQ), bf16)],
        name="attn", compiler_params=_params(2),
    )(q, k, v, km, vm, out_g)


def _attn_meta_call(qm, km, vm, out_g, l):
    qk_spec = pl.BlockSpec((None, BLOCK, QK_PAD), lambda h: (0, 0, h))
    v_spec = pl.BlockSpec((None, BLOCK, V_HEAD), lambda h: (0, 0, h))
    return pl.pallas_call(
        _attn_meta_kernel, grid=(MLA_HEADS,),
        in_specs=[qk_spec, qk_spec, v_spec,
                  pl.BlockSpec((None, 1, V_HEAD), lambda h: (l, 0, G_OUT // V_HEAD + h))],
        out_specs=v_spec,
        out_shape=jax.ShapeDtypeStruct((1, BLOCK, MLA_WIDTH), bf16),
        name="attn_meta", compiler_params=_params(1),
    )(qm, km, vm, out_g)


def _ret_tables(lg_ref, C):
    lg = lg_ref[0:1, :]
    lgc = jnp.concatenate([lg] * (C // LANES), axis=1)
    ri = lax.broadcasted_iota(jnp.int32, (C, C), 0)
    ci = lax.broadcasted_iota(jnp.int32, (C, C), 1)
    diff = (ri - ci).astype(f32)
    decay = jnp.where(diff >= 0, jnp.exp(jnp.maximum(diff, 0.0) * lgc), 0.0)
    idx = lax.broadcasted_iota(jnp.int32, (C, RET_HEAD), 0).astype(f32)
    xi = jnp.exp((idx + 1.0) * lg)
    return lg, decay, idx, xi


def _ret_key_state(k, v, idx, lg):
    n = k.shape[0]
    zeta = jnp.exp((n - 1.0 - idx[:n]) * lg)
    vz = (v.astype(f32) * zeta).astype(bf16)
    return _dot_tn(vz, k)


def _ret_mix(q, e, k, v, decay, state_t, xi):
    lane = lax.broadcasted_iota(jnp.int32, (1, RET_PAIR), 1) % LANES
    q = jnp.where((lane < HALF) if e == 0 else (lane >= HALF), q, jnp.zeros_like(q))
    n = q.shape[0]
    o = _dot((_dot_nt(q, k) * decay[:n, :n]).astype(bf16), v)
    if state_t is not None:
        o = o + _dot_nt(q, state_t.astype(bf16)) * xi[:n]
    return o


def _ret_emit(o, g, ng, nb):
    mu = jnp.mean(o, axis=-1, keepdims=True)
    d = o - mu
    var = jnp.mean(d * d, axis=-1, keepdims=True)
    on = d * lax.rsqrt(var + EPS) * ng + nb
    g = g.astype(f32)
    return (g * _sigmoid(g) * on).astype(bf16)


def _ret_kernel(q_ref, k_ref, v_ref, g_ref, km_ref, vm_ref, lg_ref, ng_ref, nb_ref, o_ref):
    C = RET_CHUNK
    n_chunks = SEQ // C
    chunks = [slice(c * C, (c + 1) * C) for c in range(n_chunks)]
    heads = []
    for h in range(2 * RET_PPS):
        lg, decay, idx, xi = _ret_tables(lg_ref.at[h], C)
        hv = slice(h * RET_HEAD, (h + 1) * RET_HEAD)
        pr = slice((h // 2) * RET_PAIR, (h // 2 + 1) * RET_PAIR)
        heads.append(dict(
            lg=lg, decay=decay, idx=idx, xi=xi, hv=hv, pr=pr, e=h % 2,
            chunk_decay=jnp.exp(float(C) * jnp.concatenate([lg, lg], axis=1)),
            state_t=_ret_key_state(km_ref[:, pr], vm_ref[:, hv], idx, lg), o_prev=None))
    for c, rows in enumerate(chunks):
        for hd in heads:
            hv, pr = hd["hv"], hd["pr"]
            o = _ret_mix(q_ref[rows, pr], hd["e"], k_ref[rows, pr], v_ref[rows, hv], hd["decay"],
                         hd["state_t"], hd["xi"])
            if c + 1 < n_chunks:
                hd["state_t"] = hd["state_t"] * hd["chunk_decay"] + _ret_key_state(
                    k_ref[rows, pr], v_ref[rows, hv], hd["idx"], hd["lg"])
            if hd["o_prev"] is not None:
                o_ref[chunks[c - 1], hv] = _ret_emit(hd["o_prev"], g_ref[chunks[c - 1], hv],
                                                     ng_ref[:, hv], nb_ref[:, hv])
            hd["o_prev"] = o
    for hd in heads:
        hv = hd["hv"]
        o_ref[chunks[-1], hv] = _ret_emit(hd["o_prev"], g_ref[chunks[-1], hv], ng_ref[:, hv],
                                          nb_ref[:, hv])


def _ret_meta_kernel(q_ref, k_ref, v_ref, g_ref, lg_ref, ng_ref, nb_ref, o_ref):
    for e in range(2):
        hv = slice(e * RET_HEAD, (e + 1) * RET_HEAD)
        _, decay, _, _ = _ret_tables(lg_ref.at[e], BLOCK)
        o = _ret_mix(q_ref[...], e, k_ref[...], v_ref[:, hv], decay, None, None)
        o_ref[:, hv] = _ret_emit(o, g_ref[:, hv], ng_ref[:, hv], nb_ref[:, hv])


def _ret_call(rq, rk, rv, rg, rkm, rvm, log_g, norm_g, norm_b, l):
    B = rq.shape[0]
    n = RET_PPS
    w = 2 * n * RET_HEAD
    pair = pl.BlockSpec((None, SEQ, n * RET_PAIR), lambda b, p: (b, 0, p))
    spec = pl.BlockSpec((None, SEQ, w), lambda b, p: (b, 0, p))
    vec_g = pl.BlockSpec((None, 1, w), lambda b, p: (l, 0, G_RETG // w + p))
    vec_b = pl.BlockSpec((None, 1, w), lambda b, p: (l, 0, G_RETB // w + p))
    return pl.pallas_call(
        _ret_kernel, grid=(B, RET_HEADS // (2 * n)),
        in_specs=[pair, pair, spec, spec,
                  pl.BlockSpec((None, BLOCK, n * RET_PAIR), lambda b, p: (0, 0, p)),
                  pl.BlockSpec((None, BLOCK, w), lambda b, p: (0, 0, p)),
                  pl.BlockSpec((2 * n, 8, LANES), lambda b, p: (p, 0, 0)), vec_g, vec_b],
        out_specs=spec,
        out_shape=jax.ShapeDtypeStruct((B, SEQ, RET_WIDTH), bf16),
        name="ret", compiler_params=_params(2),
    )(rq, rk, rv, rg, rkm, rvm, log_g, norm_g, norm_b)


def _ret_meta_call(rqm, rkm, rvm, rgm, log_g, norm_g, norm_b, l):
    spec = pl.BlockSpec((None, BLOCK, 2 * RET_HEAD), lambda p: (0, 0, p))
    w = 2 * RET_HEAD
    vec_g = pl.BlockSpec((None, 1, w), lambda p: (l, 0, G_RETG // w + p))
    vec_b = pl.BlockSpec((None, 1, w), lambda p: (l, 0, G_RETB // w + p))
    pair = pl.BlockSpec((None, BLOCK, RET_PAIR), lambda p: (0, 0, p))
    return pl.pallas_call(
        _ret_meta_kernel, grid=(RET_HEADS // 2,),
        in_specs=[pair, pair, spec, spec, pl.BlockSpec((2, 8, LANES), lambda p: (p, 0, 0)),
                  vec_g, vec_b],
        out_specs=spec,
        out_shape=jax.ShapeDtypeStruct((1, BLOCK, RET_WIDTH), bf16),
        name="ret_meta", compiler_params=_params(1),
    )(rqm, rkm, rvm, rgm, log_g, norm_g, norm_b)


def _post_kernel(h_ref, ym_ref, yr_ref, wo_ref, fg_ref, wgu_ref, wd_ref, o_ref, act_ref):
    h1 = (h_ref[...] + _dot(ym_ref[...], wo_ref[0:MLA_WIDTH, :])
          + _dot(yr_ref[...], wo_ref[MLA_WIDTH:MLA_WIDTH + RET_WIDTH, :]))
    hf = (h1 * _rms(h1, 1.0 / D_MODEL) * fg_ref[...]).astype(bf16)
    for c in range(D_FF // FF_TILE):
        lo = c * FF_TILE
        gate = _dot(hf, wgu_ref[:, lo:lo + FF_TILE])
        up = _dot(hf, wgu_ref[:, D_FF + lo:D_FF + lo + FF_TILE])
        act_ref[:, lo:lo + FF_TILE] = (gate * _sigmoid(gate) * up).astype(bf16)
    o_ref[...] = h1 + _dot(act_ref[...], wd_ref[...])


def _post_call(h, ym, yr, p, l, tm, name):
    B, R, _ = h.shape
    row = lambda w: pl.BlockSpec((None, tm, w), lambda b, j: (b, j, 0))
    return pl.pallas_call(
        _post_kernel, grid=(B, R // tm),
        in_specs=[row(D_MODEL), row(MLA_WIDTH), row(RET_WIDTH),
                  _layer_spec(l, (MLA_WIDTH + RET_WIDTH, D_MODEL)),
                  _gain_spec(l, G_FFN, D_MODEL),
                  _layer_spec(l, (D_MODEL, 2 * D_FF)),
                  _layer_spec(l, (D_FF, D_MODEL))],
        out_specs=row(D_MODEL),
        out_shape=jax.ShapeDtypeStruct((B, R, D_MODEL), f32),
        scratch_shapes=[pltpu.VMEM((tm, D_FF), bf16)],
        name=name, compiler_params=_params(2),
    )(h, ym, yr, p["w_out"], p["gains"], p["w_gu"], p["w_down"])


W_IN_COLS = 2 * Q_LORA + QK_ROPE + 4 * RET_WIDTH
O_KPE = Q_LORA + KV_LORA
O_RQ = O_KPE + QK_ROPE
O_RK = O_RQ + RET_WIDTH
O_RV = O_RK + RET_WIDTH
WPREP_COLS = 256


def _wprep_kernel(w_ref, head_ref, qk_ref, vg_ref):
    def put(dst_ref, dst, src, n):
        dst_ref[dst:dst + n, :] = w_ref[src:src + n, :].astype(bf16)

    q = QK_ROPE // 2
    head_ref[...] = jnp.zeros(head_ref.shape, bf16)
    put(head_ref, 0, 0, O_KPE)
    put(head_ref, C_KPE + HALF, O_KPE, q)
    put(head_ref, C_KPE + LANES + HALF, O_KPE + q, q)
    for s, src0 in enumerate((O_RQ, O_RK)):
        for p in range(RET_HEADS // 2):
            src, dst = src0 + p * RET_PAIR, s * RET_WIDTH + p * RET_PAIR
            put(qk_ref, dst, src, HALF)
            put(qk_ref, dst + HALF, src + 2 * HALF, HALF)
            put(qk_ref, dst + 2 * HALF, src + HALF, HALF)
            put(qk_ref, dst + 3 * HALF, src + 3 * HALF, HALF)
    put(vg_ref, 0, O_RV, 2 * RET_WIDTH)


def _wprep_call(w_in):
    depth = w_in.shape[0]
    wt = jnp.swapaxes(w_in, 1, 2)
    cb = WPREP_COLS
    out = lambda n: pl.BlockSpec((None, n, cb), lambda l, c: (l, 0, c))
    rows = (C_RQ, C_RV - C_RQ, N_IN_PAD - C_RV)
    return pl.pallas_call(
        _wprep_kernel, grid=(depth, D_MODEL // cb),
        in_specs=[out(W_IN_COLS)], out_specs=[out(n) for n in rows],
        out_shape=[jax.ShapeDtypeStruct((depth, n, D_MODEL), bf16) for n in rows],
        name="wprep", compiler_params=_params(2),
    )(wt)


def _rope_tables(pos, valid):
    pos = pos.astype(np.float32)
    n = pos.shape[0]

    def cs(dim):
        inv = np.float32(ROPE_BASE) ** (-np.arange(0, dim, 2, dtype=np.float32) / np.float32(dim))
        ang = pos[:, None] * inv[None, :].astype(np.float32)
        return np.cos(ang).astype(np.float32), np.sin(ang).astype(np.float32)

    cm, sm = cs(QK_ROPE)
    z32 = np.zeros_like(cm)
    ca = np.concatenate([np.ones((n, HALF), np.float32), cm, z32], axis=1)
    sa = np.concatenate([np.zeros((n, HALF), np.float32), sm, z32], axis=1)
    cr, sr = cs(RET_HEAD)
    c2 = np.concatenate([cr, cr], axis=1)
    s2 = np.concatenate([sr, sr], axis=1)
    kscale = valid.astype(np.float32)[:, None] * np.float32(RET_HEAD ** -0.5)
    return tuple(jnp.asarray(t) for t in (ca, sa, c2, s2, c2 * kscale, s2 * kscale))


def _mla_head_layout(a):
    q = QK_ROPE // 2
    z = jnp.zeros(a.shape[:-1] + (q,), a.dtype)
    return jnp.concatenate([a[..., :HALF], a[..., QK_NOPE:QK_NOPE + q], z,
                            a[..., HALF:QK_NOPE], a[..., QK_NOPE + q:], z], axis=-1)


def _prep_params(attn_norm_g, w_in, q_a_norm_g, w_q_b, kv_a_norm_g, w_kv_b, q_norm_g, k_norm_g,
                 mla_out_norm_g, ret_norm_g, ret_norm_b, w_out, ffn_norm_g, w_gate_up, w_down):
    depth = w_in.shape[0]
    w_head, w_qk, w_vg = _wprep_call(w_in)
    wq = _mla_head_layout(w_q_b.astype(bf16).reshape(depth, Q_LORA, MLA_HEADS, QK_HEAD))
    wkv = w_kv_b.astype(bf16).reshape(depth, KV_LORA, MLA_HEADS, QK_NOPE + V_HEAD)
    z64 = jnp.zeros((depth, KV_LORA, MLA_HEADS, HALF), bf16)
    wk = jnp.concatenate([wkv[..., :HALF], z64, wkv[..., HALF:QK_NOPE], z64], axis=-1)
    wkv = jnp.concatenate([wk.reshape(depth, KV_LORA, MLA_HEADS * QK_PAD),
                           wkv[..., QK_NOPE:].reshape(depth, KV_LORA, MLA_WIDTH)], axis=-1)
    scale = QK_HEAD ** -0.5 * LOG2_E
    gains = jnp.concatenate(
        [attn_norm_g, ffn_norm_g, mla_out_norm_g, ret_norm_g, ret_norm_b, q_a_norm_g, kv_a_norm_g,
         _mla_head_layout(q_norm_g * scale), _mla_head_layout(k_norm_g)], axis=-1)[:, None, :]
    assert gains.shape[-1] == G_TOTAL
    return {
        "gains": gains,
        "w_head": w_head,
        "w_qk": w_qk,
        "w_vg": w_vg,
        "w_qb": wq.reshape(depth, Q_LORA, MLA_HEADS * QK_PAD),
        "w_kvb": wkv,
        "w_out": w_out.astype(bf16),
        "w_gu": w_gate_up.astype(bf16),
        "w_down": w_down.astype(bf16),
    }


def kernel(x, meta_tokens, attn_norm_g, w_in, q_a_norm_g, w_q_b, kv_a_norm_g, w_kv_b, q_norm_g,
           k_norm_g, mla_out_norm_g, ret_norm_g, ret_norm_b, w_out, ffn_norm_g, w_gate_up, w_down):
    depth = w_in.shape[0]
    hx = x
    hm = jnp.concatenate([jnp.zeros((PAD_ROWS, D_MODEL), x.dtype), meta_tokens.astype(x.dtype)])[None]
    r = np.arange(BLOCK)
    tabs_x = _rope_tables(np.arange(SEQ) + N_META, np.ones(SEQ))
    tabs_m = _rope_tables(np.maximum(r - PAD_ROWS, 0), r >= PAD_ROWS)
    gamma = np.float32(1.0) - np.float32(2.0) ** (np.float32(-5.0) - np.arange(RET_HEADS, dtype=np.float32))
    log_g = jnp.asarray(np.broadcast_to(np.log(gamma)[:, None, None], (RET_HEADS, 8, LANES)))
    p = _prep_params(attn_norm_g, w_in, q_a_norm_g, w_q_b, kv_a_norm_g, w_kv_b, q_norm_g, k_norm_g,
                     mla_out_norm_g, ret_norm_g, ret_norm_b, w_out, ffn_norm_g, w_gate_up, w_down)
    for l in range(depth):
        q, k, v, rq, rk, rv, rg = _pre_call(hx, p, l, tabs_x, PRE_TM, "pre")
        qm, km, vm, rqm, rkm, rvm, rgm = _pre_call(hm, p, l, tabs_m, BLOCK, "pre_meta")
        y_mla = _attn_call(q, k, v, km, vm, p["gains"], l)
        y_ret = _ret_call(rq, rk, rv, rg, rkm, rvm, log_g, p["gains"], p["gains"], l)
        hx = _post_call(hx, y_mla, y_ret, p, l, ROW_TM, "post")
        if l + 1 < depth:
            ym_mla = _attn_meta_call(qm, km, vm, p["gains"], l)
            ym_ret = _ret_meta_call(rqm, rkm, rvm, rgm, log_g, p["gains"], p["gains"], l)
            hm = _post_call(hm, ym_mla, ym_ret, p, l, BLOCK, "post_meta")
    return hx
```

```python
import functools

import numpy as np

import jax
import jax.numpy as jnp
from jax import lax
from jax.experimental import pallas as pl
from jax.experimental.pallas import tpu as pltpu

D_MODEL = 1024
SEQ = 2048
N_META = 16
BLOCK = 128
MLA_HEADS = 4
Q_LORA = 256
KV_LORA = 256
QK_NOPE = 128
QK_ROPE = 64
QK_HEAD = QK_NOPE + QK_ROPE
V_HEAD = 128
MLA_WIDTH = MLA_HEADS * V_HEAD
RET_HEADS = 4
RET_HEAD = 128
RET_WIDTH = RET_HEADS * RET_HEAD
D_FF = 2816
ROPE_BASE = 10000.0
EPS = 1e-6
NEG_INF = -1e30

LANES = 128
HALF = LANES // 2
PAD_ROWS = BLOCK - N_META
QK_PAD = 2 * LANES
KV_WIDTH = MLA_HEADS * QK_PAD + MLA_WIDTH
RET_PAIR = 2 * RET_HEAD

C_CQ = 0
C_CKV = C_CQ + Q_LORA
C_KPE = C_CKV + KV_LORA
C_RQ = C_KPE + 2 * LANES
C_RK = C_RQ + RET_WIDTH
C_RV = C_RK + RET_WIDTH
C_RG = C_RV + RET_WIDTH
N_IN_PAD = C_RG + RET_WIDTH

ROW_TM = 1024
PRE_TM = 1024
ATT_TQ = 512
ATT_ORDER = (0, 1, 2, 3)
ATT_HPS = 2
ATT_AHEAD = 2
RET_CHUNK = 256
RET_PPS = 2
FF_TILE = 256
assert D_FF % FF_TILE == 0
LOG2_E = 1.4426950408889634
VMEM_LIMIT = 56 * 1024 * 1024

f32 = jnp.float32
bf16 = jnp.bfloat16


def _dot(a, b):
    return jnp.dot(a, b, preferred_element_type=f32)


def _dot_nt(a, b):
    return lax.dot_general(a, b, (((1,), (1,)), ((), ())), preferred_element_type=f32)


def _dot_tn(a, b):
    return lax.dot_general(a, b, (((0,), (0,)), ((), ())), preferred_element_type=f32)


def _rms(x, inv_n):
    return lax.rsqrt(jnp.sum(x * x, axis=-1, keepdims=True) * inv_n + EPS)


def _sigmoid(x):
    return 1.0 / (1.0 + jnp.exp(-x))


def _rot(a, b, c, s):
    return a * c - b * s, b * c + a * s


def _params(n_grid_axes):
    return pltpu.CompilerParams(dimension_semantics=("parallel",) * n_grid_axes,
                                vmem_limit_bytes=VMEM_LIMIT)


G_ATTN = 0
G_FFN = G_ATTN + D_MODEL
G_OUT = G_FFN + D_MODEL
G_RETG = G_OUT + MLA_WIDTH
G_RETB = G_RETG + RET_WIDTH
G_QA = G_RETB + RET_WIDTH
G_KVA = G_QA + Q_LORA
G_Q = G_KVA + KV_LORA
G_K = G_Q + QK_PAD
G_TOTAL = G_K + QK_PAD


def _gain_spec(l, off, width):
    assert off % width == 0
    return pl.BlockSpec((None, 1, width), lambda *_: (l, 0, off // width),
                        pipeline_mode=pl.Buffered(1))


def _layer_spec(l, shape):
    nd = len(shape)
    return pl.BlockSpec((None,) + shape, lambda *_: (l,) + (0,) * nd, pipeline_mode=pl.Buffered(1))


def _pre_kernel(h_ref, g_ref, whead_ref, wqk_ref, wvg_ref, qag_ref, wqb_ref, kvag_ref, wkvb_ref,
                qg_ref, kg_ref, ca_ref, sa_ref, c2_ref, s2_ref, ck_ref, sk_ref, *rest, n_cast):
    cast_in, rest = rest[:n_cast], rest[n_cast:]
    q_out, k_out, v_out, rq_out, rk_out, rv_out, rg_out = rest[:7]
    cast_out = rest[7:7 + n_cast]
    hb_ref, cqn_ref, ckvn_ref, kpe_ref = rest[7 + n_cast:]

    def cast(i):
        if i < n_cast:
            cast_out[i][...] = cast_in[i][...].astype(bf16)

    qga, qgb = qg_ref[:, :LANES], qg_ref[:, LANES:]
    kga, kgb = kg_ref[:, :LANES], kg_ref[:, LANES:]
    G = 2 * LANES

    x = h_ref[...]
    hb_ref[...] = (x * _rms(x, 1.0 / D_MODEL) * g_ref[...]).astype(bf16)

    def proj(w_ref, lo):
        return _dot_nt(hb_ref[...], w_ref[lo:lo + G, :])

    cq = proj(whead_ref, C_CQ)
    cqn_ref[...] = (cq * _rms(cq, 1.0 / Q_LORA) * qag_ref[...]).astype(bf16)
    ckv = proj(whead_ref, C_CKV)
    ckvn_ref[...] = (ckv * _rms(ckv, 1.0 / KV_LORA) * kvag_ref[...]).astype(bf16)
    kpe_ref[...] = proj(whead_ref, C_KPE)

    def norm_rope(ab, ga, gb):
        a, b = ab[:, :LANES], ab[:, LANES:]
        r = lax.rsqrt(jnp.sum(a * a + b * b, axis=-1, keepdims=True) * (1.0 / QK_HEAD) + EPS)
        ao, bo = _rot(a * ga, b * gb, ca_ref[...], sa_ref[...])
        return (ao * r).astype(bf16), (bo * r).astype(bf16)

    def mla_pair(hp):
        lo2 = hp * 2 * QK_PAD
        q2h = _dot(cqn_ref[...], wqb_ref[:, lo2:lo2 + 2 * QK_PAD])
        k2h = _dot(ckvn_ref[...], wkvb_ref[:, lo2:lo2 + 2 * QK_PAD])
        for e in range(2):
            lo = lo2 + e * QK_PAD
            q_out[:, lo:lo + LANES], q_out[:, lo + LANES:lo + QK_PAD] = norm_rope(
                q2h[:, e * QK_PAD:(e + 1) * QK_PAD], qga, qgb)
            k_out[:, lo:lo + LANES], k_out[:, lo + LANES:lo + QK_PAD] = norm_rope(
                k2h[:, e * QK_PAD:(e + 1) * QK_PAD] + kpe_ref[...], kga, kgb)

    def ret_pairs(zq, zk):
        for p in range(RET_HEADS // 2):
            lo = p * RET_PAIR
            k1, k2 = _rot(zk[:, lo:lo + LANES], zk[:, lo + LANES:lo + RET_PAIR], ck_ref[...],
                          sk_ref[...])
            rk_out[:, lo:lo + LANES] = k1.astype(bf16)
            rk_out[:, lo + LANES:lo + RET_PAIR] = k2.astype(bf16)
            q1, q2 = _rot(zq[:, lo:lo + LANES], zq[:, lo + LANES:lo + RET_PAIR], c2_ref[...],
                          s2_ref[...])
            rq_out[:, lo:lo + LANES] = q1.astype(bf16)
            rq_out[:, lo + LANES:lo + RET_PAIR] = q2.astype(bf16)

    rv_out[...] = _dot_nt(hb_ref[...], wvg_ref[:RET_WIDTH, :]).astype(bf16)
    cast(0)
    mla_pair(0)
    rg_out[...] = _dot_nt(hb_ref[...], wvg_ref[RET_WIDTH:, :]).astype(bf16)
    cast(1)
    mla_pair(1)
    zk = _dot_nt(hb_ref[...], wqk_ref[RET_WIDTH:, :])
    cast(2)
    zq = _dot_nt(hb_ref[...], wqk_ref[:RET_WIDTH, :])
    v_out[...] = _dot(ckvn_ref[...], wkvb_ref[:, MLA_HEADS * QK_PAD:]).astype(bf16)
    ret_pairs(zq, zk)


def _pre_call(h, p, l, tabs, tm, name, cast_ws=()):
    B, R, _ = h.shape
    nj = R // tm
    steps = B * nj
    row = lambda w: pl.BlockSpec((None, tm, w), lambda b, j: (b, j, 0))
    tab = pl.BlockSpec((tm, LANES), lambda b, j: (j, 0))
    slabs = []
    for w in cast_ws:
        assert w.shape[1] % (steps * 16) == 0, w.shape
        slabs.append((w.shape[1] // steps, w.shape[2]))
    cast_in = [pl.BlockSpec((None,) + s, lambda b, j: (l, b * nj + j, 0)) for s in slabs]
    cast_out = [pl.BlockSpec(s, lambda b, j: (b * nj + j, 0)) for s in slabs]
    in_specs = [
        row(D_MODEL),
        _gain_spec(l, G_ATTN, D_MODEL),
        _layer_spec(l, (C_RQ, D_MODEL)),
        _layer_spec(l, (C_RV - C_RQ, D_MODEL)),
        _layer_spec(l, (N_IN_PAD - C_RV, D_MODEL)),
        _gain_spec(l, G_QA, Q_LORA),
        _layer_spec(l, (Q_LORA, MLA_HEADS * QK_PAD)),
        _gain_spec(l, G_KVA, KV_LORA),
        _layer_spec(l, (KV_LORA, KV_WIDTH)),
        _gain_spec(l, G_Q, QK_PAD),
        _gain_spec(l, G_K, QK_PAD),
    ] + [tab] * 6 + cast_in
    widths = (MLA_HEADS * QK_PAD, MLA_HEADS * QK_PAD, MLA_WIDTH, RET_WIDTH, RET_WIDTH, RET_WIDTH,
              RET_WIDTH)
    return pl.pallas_call(
        functools.partial(_pre_kernel, n_cast=len(cast_ws)), grid=(B, nj),
        in_specs=in_specs, out_specs=[row(w) for w in widths] + cast_out,
        out_shape=[jax.ShapeDtypeStruct((B, R, w), bf16) for w in widths]
        + [jax.ShapeDtypeStruct(w.shape[1:], bf16) for w in cast_ws],
        scratch_shapes=[pltpu.VMEM((tm, D_MODEL), bf16), pltpu.VMEM((tm, Q_LORA), bf16),
                        pltpu.VMEM((tm, KV_LORA), bf16), pltpu.VMEM((tm, 2 * LANES), f32)],
        name=name, compiler_params=_params(2),
    )(h, p["gains"], p["w_head"], p["w_qk"], p["w_vg"], p["gains"], p["w_qb"], p["gains"], p["w_kvb"],
      p["gains"], p["gains"], *tabs, *cast_ws)


def _rowmax(s):
    return jnp.max(s, axis=-1, keepdims=True)


def _rowsum(p):
    return jnp.sum(p, axis=-1, keepdims=True)


def _attn_finish(acc, l, og):
    o = acc * (1.0 / l)
    return (o * _rms(o, 1.0 / V_HEAD) * og).astype(bf16)


def _meta_valid():
    return lax.broadcasted_iota(jnp.int32, (1, BLOCK), 1) >= PAD_ROWS


def _attn_kernel(q_ref, k_ref, v_ref, km_ref, vm_ref, og_ref, o_ref, vt_ref):
    for hh in range(ATT_HPS):
        vv = slice(hh * V_HEAD, (hh + 1) * V_HEAD)
        vt_ref[vv, 0:BLOCK] = vm_ref[:, vv].T
        vt_ref[vv, BLOCK:] = v_ref[:, vv].T
    key_valid = lax.broadcasted_iota(jnp.int32, (BLOCK, 1), 0) >= PAD_ROWS
    tri_t = (lax.broadcasted_iota(jnp.int32, (ATT_TQ, ATT_TQ), 0)
             <= lax.broadcasted_iota(jnp.int32, (ATT_TQ, ATT_TQ), 1))

    def scores(item):
        hh, i = item
        qk = slice(hh * QK_PAD, (hh + 1) * QK_PAD)
        lo, hi = i * ATT_TQ, (i + 1) * ATT_TQ
        q = q_ref[lo:hi, qk]
        s_m = jnp.where(key_valid, _dot_nt(km_ref[:, qk], q), NEG_INF)
        s_x = _dot_nt(k_ref[0:hi, qk], q)
        parts = [s_m] + ([s_x[:lo]] if lo else []) + [jnp.where(tri_t, s_x[lo:], NEG_INF)]
        return jnp.concatenate(parts, axis=0)

    items = [(hh, i) for i in ATT_ORDER for hh in range(ATT_HPS)]
    ahead = [scores(it) for it in items[:ATT_AHEAD]]
    for n, (hh, i) in enumerate(items):
        if n + ATT_AHEAD < len(items):
            ahead.append(scores(items[n + ATT_AHEAD]))
        s = ahead.pop(0)
        vv = slice(hh * V_HEAD, (hh + 1) * V_HEAD)
        lo, hi = i * ATT_TQ, (i + 1) * ATT_TQ
        p = jnp.exp2(s - jnp.max(s, axis=0, keepdims=True))
        l = jnp.sum(p, axis=0, keepdims=True)
        o_t = _dot(vt_ref[vv, 0:BLOCK + hi], p.astype(bf16)) * (1.0 / l)
        r = lax.rsqrt(jnp.sum(o_t * o_t, axis=0, keepdims=True) * (1.0 / V_HEAD) + EPS)
        o_ref[lo:hi, vv] = ((o_t * r).T * og_ref[:, vv]).astype(bf16)


def _attn_meta_kernel(q_ref, k_ref, v_ref, og_ref, o_ref):
    mask = (lax.broadcasted_iota(jnp.int32, (BLOCK, BLOCK), 0)
            >= lax.broadcasted_iota(jnp.int32, (BLOCK, BLOCK), 1)) & _meta_valid()
    s = jnp.where(mask, _dot_nt(q_ref[...], k_ref[...]), NEG_INF)
    p = jnp.exp2(s - _rowmax(s))
    o_ref[...] = _attn_finish(_dot(p.astype(bf16), v_ref[...]), _rowsum(p), og_ref[...])


def _attn_call(q, k, v, km, vm, out_g, l):
    B = q.shape[0]
    n = ATT_HPS
    qk_spec = pl.BlockSpec((None, SEQ, n * QK_PAD), lambda b, h: (b, 0, h))
    v_spec = pl.BlockSpec((None, SEQ, n * V_HEAD), lambda b, h: (b, 0, h))
    return pl.pallas_call(
        _attn_kernel, grid=(B, MLA_HEADS // n),
        in_specs=[qk_spec, qk_spec, v_spec,
                  pl.BlockSpec((None, BLOCK, n * QK_PAD), lambda b, h: (0, 0, h)),
                  pl.BlockSpec((None, BLOCK, n * V_HEAD), lambda b, h: (0, 0, h)),
                  pl.BlockSpec((None, 1, n * V_HEAD), lambda b, h: (l, 0, G_OUT // (n * V_HEAD) + h))],
        out_specs=v_spec,
        out_shape=jax.ShapeDtypeStruct((B, SEQ, MLA_WIDTH), bf16),
        scratch_shapes=[pltpu.VMEM((n * V_HEAD, BLOCK + SEQ), bf16)],
        name="attn", compiler_params=_params(2),
    )(q, k, v, km, vm, out_g)


def _attn_meta_call(qm, km, vm, out_g, l):
    qk_spec = pl.BlockSpec((None, BLOCK, QK_PAD), lambda h: (0, 0, h))
    v_spec = pl.BlockSpec((None, BLOCK, V_HEAD), lambda h: (0, 0, h))
    return pl.pallas_call(
        _attn_meta_kernel, grid=(MLA_HEADS,),
        in_specs=[qk_spec, qk_spec, v_spec,
                  pl.BlockSpec((None, 1, V_HEAD), lambda h: (l, 0, G_OUT // V_HEAD + h))],
        out_specs=v_spec,
        out_shape=jax.ShapeDtypeStruct((1, BLOCK, MLA_WIDTH), bf16),
        name="attn_meta", compiler_params=_params(1),
    )(qm, km, vm, out_g)


def _ret_tables(lg_ref, C):
    lg = lg_ref[0:1, :]
    lgc = jnp.concatenate([lg] * (C // LANES), axis=1)
    ri = lax.broadcasted_iota(jnp.int32, (C, C), 0)
    ci = lax.broadcasted_iota(jnp.int32, (C, C), 1)
    diff = (ri - ci).astype(f32)
    decay = jnp.where(diff >= 0, jnp.exp(jnp.maximum(diff, 0.0) * lgc), 0.0)
    idx = lax.broadcasted_iota(jnp.int32, (C, RET_HEAD), 0).astype(f32)
    xi = jnp.exp((idx + 1.0) * lg)
    return lg, decay, idx, xi


def _ret_key_state(k, v, idx, lg):
    n = k.shape[0]
    zeta = jnp.exp((n - 1.0 - idx[:n]) * lg)
    vz = (v.astype(f32) * zeta).astype(bf16)
    return _dot_tn(vz, k)


def _ret_mix(q, e, k, v, decay, state_t, xi):
    lane = lax.broadcasted_iota(jnp.int32, (1, RET_PAIR), 1) % LANES
    q = jnp.where((lane < HALF) if e == 0 else (lane >= HALF), q, jnp.zeros_like(q))
    n = q.shape[0]
    o = _dot((_dot_nt(q, k) * decay[:n, :n]).astype(bf16), v)
    if state_t is not None:
        o = o + _dot_nt(q, state_t.astype(bf16)) * xi[:n]
    return o


def _ret_emit(o, g, ng, nb):
    mu = jnp.mean(o, axis=-1, keepdims=True)
    d = o - mu
    var = jnp.mean(d * d, axis=-1, keepdims=True)
    on = d * lax.rsqrt(var + EPS) * ng + nb
    g = g.astype(f32)
    return (g * _sigmoid(g) * on).astype(bf16)


def _ret_kernel(q_ref, k_ref, v_ref, g_ref, km_ref, vm_ref, lg_ref, ng_ref, nb_ref, o_ref):
    C = RET_CHUNK
    n_chunks = SEQ // C
    chunks = [slice(c * C, (c + 1) * C) for c in range(n_chunks)]
    heads = []
    for h in range(2 * RET_PPS):
        lg, decay, idx, xi = _ret_tables(lg_ref.at[h], C)
        hv = slice(h * RET_HEAD, (h + 1) * RET_HEAD)
        pr = slice((h // 2) * RET_PAIR, (h // 2 + 1) * RET_PAIR)
        heads.append(dict(
            lg=lg, decay=decay, idx=idx, xi=xi, hv=hv, pr=pr, e=h % 2,
            chunk_decay=jnp.exp(float(C) * jnp.concatenate([lg, lg], axis=1)),
            state_t=_ret_key_state(km_ref[:, pr], vm_ref[:, hv], idx, lg), o_prev=None))
    for c, rows in enumerate(chunks):
        for hd in heads:
            hv, pr = hd["hv"], hd["pr"]
            o = _ret_mix(q_ref[rows, pr], hd["e"], k_ref[rows, pr], v_ref[rows, hv], hd["decay"],
                         hd["state_t"], hd["xi"])
            if c + 1 < n_chunks:
                hd["state_t"] = hd["state_t"] * hd["chunk_decay"] + _ret_key_state(
                    k_ref[rows, pr], v_ref[rows, hv], hd["idx"], hd["lg"])
            if hd["o_prev"] is not None:
                o_ref[chunks[c - 1], hv] = _ret_emit(hd["o_prev"], g_ref[chunks[c - 1], hv],
                                                     ng_ref[:, hv], nb_ref[:, hv])
            hd["o_prev"] = o
    for hd in heads:
        hv = hd["hv"]
        o_ref[chunks[-1], hv] = _ret_emit(hd["o_prev"], g_ref[chunks[-1], hv], ng_ref[:, hv],
                                          nb_ref[:, hv])


def _ret_meta_kernel(q_ref, k_ref, v_ref, g_ref, lg_ref, ng_ref, nb_ref, o_ref):
    for e in range(2):
        hv = slice(e * RET_HEAD, (e + 1) * RET_HEAD)
        _, decay, _, _ = _ret_tables(lg_ref.at[e], BLOCK)
        o = _ret_mix(q_ref[...], e, k_ref[...], v_ref[:, hv], decay, None, None)
        o_ref[:, hv] = _ret_emit(o, g_ref[:, hv], ng_ref[:, hv], nb_ref[:, hv])


def _ret_call(rq, rk, rv, rg, rkm, rvm, log_g, norm_g, norm_b, l):
    B = rq.shape[0]
    n = RET_PPS
    w = 2 * n * RET_HEAD
    pair = pl.BlockSpec((None, SEQ, n * RET_PAIR), lambda b, p: (b, 0, p))
    spec = pl.BlockSpec((None, SEQ, w), lambda b, p: (b, 0, p))
    vec_g = pl.BlockSpec((None, 1, w), lambda b, p: (l, 0, G_RETG // w + p))
    vec_b = pl.BlockSpec((None, 1, w), lambda b, p: (l, 0, G_RETB // w + p))
    return pl.pallas_call(
        _ret_kernel, grid=(B, RET_HEADS // (2 * n)),
        in_specs=[pair, pair, spec, spec,
                  pl.BlockSpec((None, BLOCK, n * RET_PAIR), lambda b, p: (0, 0, p)),
                  pl.BlockSpec((None, BLOCK, w), lambda b, p: (0, 0, p)),
                  pl.BlockSpec((2 * n, 8, LANES), lambda b, p: (p, 0, 0)), vec_g, vec_b],
        out_specs=spec,
        out_shape=jax.ShapeDtypeStruct((B, SEQ, RET_WIDTH), bf16),
        name="ret", compiler_params=_params(2),
    )(rq, rk, rv, rg, rkm, rvm, log_g, norm_g, norm_b)


def _ret_meta_call(rqm, rkm, rvm, rgm, log_g, norm_g, norm_b, l):
    spec = pl.BlockSpec((None, BLOCK, 2 * RET_HEAD), lambda p: (0, 0, p))
    w = 2 * RET_HEAD
    vec_g = pl.BlockSpec((None, 1, w), lambda p: (l, 0, G_RETG // w + p))
    vec_b = pl.BlockSpec((None, 1, w), lambda p: (l, 0, G_RETB // w + p))
    pair = pl.BlockSpec((None, BLOCK, RET_PAIR), lambda p: (0, 0, p))
    return pl.pallas_call(
        _ret_meta_kernel, grid=(RET_HEADS // 2,),
        in_specs=[pair, pair, spec, spec, pl.BlockSpec((2, 8, LANES), lambda p: (p, 0, 0)),
                  vec_g, vec_b],
        out_specs=spec,
        out_shape=jax.ShapeDtypeStruct((1, BLOCK, RET_WIDTH), bf16),
        name="ret_meta", compiler_params=_params(1),
    )(rqm, rkm, rvm, rgm, log_g, norm_g, norm_b)


def _post_kernel(h_ref, ym_ref, yr_ref, wo_ref, fg_ref, wgu_ref, wd_ref, o_ref, act_ref):
    h1 = (h_ref[...] + _dot(ym_ref[...], wo_ref[0:MLA_WIDTH, :])
          + _dot(yr_ref[...], wo_ref[MLA_WIDTH:MLA_WIDTH + RET_WIDTH, :]))
    hf = (h1 * _rms(h1, 1.0 / D_MODEL) * fg_ref[...]).astype(bf16)
    for c in range(D_FF // FF_TILE):
        lo = c * FF_TILE
        gate = _dot(hf, wgu_ref[:, lo:lo + FF_TILE])
        up = _dot(hf, wgu_ref[:, D_FF + lo:D_FF + lo + FF_TILE])
        act_ref[:, lo:lo + FF_TILE] = (gate * _sigmoid(gate) * up).astype(bf16)
    o_ref[...] = h1 + _dot(act_ref[...], wd_ref[...])


def _post_call(h, ym, yr, p, l, w_bf, tm, name):
    B, R, _ = h.shape
    row = lambda w: pl.BlockSpec((None, tm, w), lambda b, j: (b, j, 0))
    whole = lambda w: pl.BlockSpec(w.shape, lambda *_: (0, 0), pipeline_mode=pl.Buffered(1))
    w_out, w_gu, w_down = w_bf
    assert w_out.shape == (MLA_WIDTH + RET_WIDTH, D_MODEL) and w_gu.shape == (D_MODEL, 2 * D_FF)
    assert w_down.shape == (D_FF, D_MODEL)
    return pl.pallas_call(
        _post_kernel, grid=(B, R // tm),
        in_specs=[row(D_MODEL), row(MLA_WIDTH), row(RET_WIDTH), whole(w_out),
                  _gain_spec(l, G_FFN, D_MODEL), whole(w_gu), whole(w_down)],
        out_specs=row(D_MODEL),
        out_shape=jax.ShapeDtypeStruct((B, R, D_MODEL), f32),
        scratch_shapes=[pltpu.VMEM((tm, D_FF), bf16)],
        name=name, compiler_params=_params(2),
    )(h, ym, yr, w_out, p["gains"], w_gu, w_down)


W_IN_COLS = 2 * Q_LORA + QK_ROPE + 4 * RET_WIDTH
O_KPE = Q_LORA + KV_LORA
O_RQ = O_KPE + QK_ROPE
O_RK = O_RQ + RET_WIDTH
O_RV = O_RK + RET_WIDTH
WPREP_COLS = 256


def _wprep_kernel(w_ref, head_ref, qk_ref, vg_ref):
    def put(dst_ref, dst, src, n):
        dst_ref[dst:dst + n, :] = w_ref[src:src + n, :].astype(bf16)

    q = QK_ROPE // 2
    head_ref[...] = jnp.zeros(head_ref.shape, bf16)
    put(head_ref, 0, 0, O_KPE)
    put(head_ref, C_KPE + HALF, O_KPE, q)
    put(head_ref, C_KPE + LANES + HALF, O_KPE + q, q)
    for s, src0 in enumerate((O_RQ, O_RK)):
        for p in range(RET_HEADS // 2):
            src, dst = src0 + p * RET_PAIR, s * RET_WIDTH + p * RET_PAIR
            put(qk_ref, dst, src, HALF)
            put(qk_ref, dst + HALF, src + 2 * HALF, HALF)
            put(qk_ref, dst + 2 * HALF, src + HALF, HALF)
            put(qk_ref, dst + 3 * HALF, src + 3 * HALF, HALF)
    put(vg_ref, 0, O_RV, 2 * RET_WIDTH)


def _wprep_call(w_in):
    depth = w_in.shape[0]
    wt = jnp.swapaxes(w_in, 1, 2)
    cb = WPREP_COLS
    out = lambda n: pl.BlockSpec((None, n, cb), lambda l, c: (l, 0, c))
    rows = (C_RQ, C_RV - C_RQ, N_IN_PAD - C_RV)
    return pl.pallas_call(
        _wprep_kernel, grid=(depth, D_MODEL // cb),
        in_specs=[out(W_IN_COLS)], out_specs=[out(n) for n in rows],
        out_shape=[jax.ShapeDtypeStruct((depth, n, D_MODEL), bf16) for n in rows],
        name="wprep", compiler_params=_params(2),
    )(wt)


def _rope_tables(pos, valid):
    pos = pos.astype(np.float32)
    n = pos.shape[0]

    def cs(dim):
        inv = np.float32(ROPE_BASE) ** (-np.arange(0, dim, 2, dtype=np.float32) / np.float32(dim))
        ang = pos[:, None] * inv[None, :].astype(np.float32)
        return np.cos(ang).astype(np.float32), np.sin(ang).astype(np.float32)

    cm, sm = cs(QK_ROPE)
    z32 = np.zeros_like(cm)
    ca = np.concatenate([np.ones((n, HALF), np.float32), cm, z32], axis=1)
    sa = np.concatenate([np.zeros((n, HALF), np.float32), sm, z32], axis=1)
    cr, sr = cs(RET_HEAD)
    c2 = np.concatenate([cr, cr], axis=1)
    s2 = np.concatenate([sr, sr], axis=1)
    kscale = valid.astype(np.float32)[:, None] * np.float32(RET_HEAD ** -0.5)
    return tuple(jnp.asarray(t) for t in (ca, sa, c2, s2, c2 * kscale, s2 * kscale))


def _mla_head_layout(a):
    q = QK_ROPE // 2
    z = jnp.zeros(a.shape[:-1] + (q,), a.dtype)
    return jnp.concatenate([a[..., :HALF], a[..., QK_NOPE:QK_NOPE + q], z,
                            a[..., HALF:QK_NOPE], a[..., QK_NOPE + q:], z], axis=-1)


def _prep_params(attn_norm_g, w_in, q_a_norm_g, w_q_b, kv_a_norm_g, w_kv_b, q_norm_g, k_norm_g,
                 mla_out_norm_g, ret_norm_g, ret_norm_b, ffn_norm_g):
    depth = w_in.shape[0]
    w_head, w_qk, w_vg = _wprep_call(w_in)
    wq = _mla_head_layout(w_q_b.astype(bf16).reshape(depth, Q_LORA, MLA_HEADS, QK_HEAD))
    wkv = w_kv_b.astype(bf16).reshape(depth, KV_LORA, MLA_HEADS, QK_NOPE + V_HEAD)
    z64 = jnp.zeros((depth, KV_LORA, MLA_HEADS, HALF), bf16)
    wk = jnp.concatenate([wkv[..., :HALF], z64, wkv[..., HALF:QK_NOPE], z64], axis=-1)
    wkv = jnp.concatenate([wk.reshape(depth, KV_LORA, MLA_HEADS * QK_PAD),
                           wkv[..., QK_NOPE:].reshape(depth, KV_LORA, MLA_WIDTH)], axis=-1)
    scale = QK_HEAD ** -0.5 * LOG2_E
    gains = jnp.concatenate(
        [attn_norm_g, ffn_norm_g, mla_out_norm_g, ret_norm_g, ret_norm_b, q_a_norm_g, kv_a_norm_g,
         _mla_head_layout(q_norm_g * scale), _mla_head_layout(k_norm_g)], axis=-1)[:, None, :]
    assert gains.shape[-1] == G_TOTAL
    return {
        "gains": gains,
        "w_head": w_head,
        "w_qk": w_qk,
        "w_vg": w_vg,
        "w_qb": wq.reshape(depth, Q_LORA, MLA_HEADS * QK_PAD),
        "w_kvb": wkv,
    }


def kernel(x, meta_tokens, attn_norm_g, w_in, q_a_norm_g, w_q_b, kv_a_norm_g, w_kv_b, q_norm_g,
           k_norm_g, mla_out_norm_g, ret_norm_g, ret_norm_b, w_out, ffn_norm_g, w_gate_up, w_down):
    depth = w_in.shape[0]
    hx = x
    hm = jnp.concatenate([jnp.zeros((PAD_ROWS, D_MODEL), x.dtype), meta_tokens.astype(x.dtype)])[None]
    r = np.arange(BLOCK)
    tabs_x = _rope_tables(np.arange(SEQ) + N_META, np.ones(SEQ))
    tabs_m = _rope_tables(np.maximum(r - PAD_ROWS, 0), r >= PAD_ROWS)
    gamma = np.float32(1.0) - np.float32(2.0) ** (np.float32(-5.0) - np.arange(RET_HEADS, dtype=np.float32))
    log_g = jnp.asarray(np.broadcast_to(np.log(gamma)[:, None, None], (RET_HEADS, 8, LANES)))
    p = _prep_params(attn_norm_g, w_in, q_a_norm_g, w_q_b, kv_a_norm_g, w_kv_b, q_norm_g, k_norm_g,
                     mla_out_norm_g, ret_norm_g, ret_norm_b, ffn_norm_g)
    for l in range(depth):
        q, k, v, rq, rk, rv, rg, *w_bf = _pre_call(hx, p, l, tabs_x, PRE_TM, "pre",
                                                   (w_out, w_gate_up, w_down))
        qm, km, vm, rqm, rkm, rvm, rgm = _pre_call(hm, p, l, tabs_m, BLOCK, "pre_meta")
        y_mla = _attn_call(q, k, v, km, vm, p["gains"], l)
        y_ret = _ret_call(rq, rk, rv, rg, rkm, rvm, log_g, p["gains"], p["gains"], l)
        hx = _post_call(hx, y_mla, y_ret, p, l, w_bf, ROW_TM, "post")
        if l + 1 < depth:
            ym_mla = _attn_meta_call(qm, km, vm, p["gains"], l)
            ym_ret = _ret_meta_call(rqm, rkm, rvm, rgm, log_g, p["gains"], p["gains"], l)
            hm = _post_call(hm, ym_mla, ym_ret, p, l, w_bf, BLOCK, "post_meta")
    return hx
```

```python
import functools

import numpy as np

import jax
import jax.numpy as jnp
from jax import lax
from jax.experimental import pallas as pl
from jax.experimental.pallas import tpu as pltpu

D_MODEL = 1024
SEQ = 2048
N_META = 16
BLOCK = 128
MLA_HEADS = 4
Q_LORA = 256
KV_LORA = 256
QK_NOPE = 128
QK_ROPE = 64
QK_HEAD = QK_NOPE + QK_ROPE
V_HEAD = 128
MLA_WIDTH = MLA_HEADS * V_HEAD
RET_HEADS = 4
RET_HEAD = 128
RET_WIDTH = RET_HEADS * RET_HEAD
D_FF = 2816
ROPE_BASE = 10000.0
EPS = 1e-6
NEG_INF = -1e30

LANES = 128
HALF = LANES // 2
PAD_ROWS = BLOCK - N_META
QK_PAD = 2 * LANES
KV_WIDTH = MLA_HEADS * QK_PAD + MLA_WIDTH
RET_PAIR = 2 * RET_HEAD

C_CQ = 0
C_CKV = C_CQ + Q_LORA
C_KPE = C_CKV + KV_LORA
C_RQ = C_KPE + 2 * LANES
C_RK = C_RQ + RET_WIDTH
C_RV = C_RK + RET_WIDTH
C_RG = C_RV + RET_WIDTH
N_IN_PAD = C_RG + RET_WIDTH

ROW_TM = 1024
PRE_TM = 1024
ATT_TQ = 512
ATT_ORDER = (0, 1, 2, 3)
ATT_HPS = 2
ATT_AHEAD = 2
RET_CHUNK = 256
RET_PPS = 2
FF_TILE = 256
assert D_FF % FF_TILE == 0
LOG2_E = 1.4426950408889634
VMEM_LIMIT = 56 * 1024 * 1024

f32 = jnp.float32
bf16 = jnp.bfloat16


def _dot(a, b):
    return jnp.dot(a, b, preferred_element_type=f32)


def _dot_nt(a, b):
    return lax.dot_general(a, b, (((1,), (1,)), ((), ())), preferred_element_type=f32)


def _dot_tn(a, b):
    return lax.dot_general(a, b, (((0,), (0,)), ((), ())), preferred_element_type=f32)


def _rms(x, inv_n):
    return lax.rsqrt(jnp.sum(x * x, axis=-1, keepdims=True) * inv_n + EPS)


def _sigmoid(x):
    return 1.0 / (1.0 + jnp.exp(-x))


def _rot(a, b, c, s):
    return a * c - b * s, b * c + a * s


def _params(n_grid_axes):
    return pltpu.CompilerParams(dimension_semantics=("parallel",) * n_grid_axes,
                                vmem_limit_bytes=VMEM_LIMIT)


G_ATTN = 0
G_FFN = G_ATTN + D_MODEL
G_OUT = G_FFN + D_MODEL
G_RETG = G_OUT + MLA_WIDTH
G_RETB = G_RETG + RET_WIDTH
G_QA = G_RETB + RET_WIDTH
G_KVA = G_QA + Q_LORA
G_Q = G_KVA + KV_LORA
G_K = G_Q + QK_PAD
G_TOTAL = G_K + QK_PAD


def _gain_spec(l, off, width):
    assert off % width == 0
    return pl.BlockSpec((None, 1, width), lambda *_: (l, 0, off // width),
                        pipeline_mode=pl.Buffered(1))


def _layer_spec(l, shape):
    nd = len(shape)
    return pl.BlockSpec((None,) + shape, lambda *_: (l,) + (0,) * nd, pipeline_mode=pl.Buffered(1))


def _cast_specs(cast_ws, l, steps, step_of):
    slabs = []
    for w in cast_ws:
        assert w.shape[1] % (steps * 16) == 0, w.shape
        slabs.append((w.shape[1] // steps, w.shape[2]))
    return ([pl.BlockSpec((None,) + s, lambda *g: (l, step_of(*g), 0)) for s in slabs],
            [pl.BlockSpec(s, lambda *g: (step_of(*g), 0)) for s in slabs],
            [jax.ShapeDtypeStruct(w.shape[1:], bf16) for w in cast_ws])


def _pre_kernel(h_ref, g_ref, whead_ref, wqk_ref, wvg_ref, qag_ref, wqb_ref, kvag_ref, wkvb_ref,
                qg_ref, kg_ref, ca_ref, sa_ref, c2_ref, s2_ref, ck_ref, sk_ref, *rest, n_cast):
    cast_in, rest = rest[:n_cast], rest[n_cast:]
    q_out, k_out, v_out, rq_out, rk_out, rv_out, rg_out = rest[:7]
    cast_out = rest[7:7 + n_cast]
    hb_ref, cqn_ref, ckvn_ref, kpe_ref = rest[7 + n_cast:]

    def cast(i):
        if i < n_cast:
            cast_out[i][...] = cast_in[i][...].astype(bf16)

    qga, qgb = qg_ref[:, :LANES], qg_ref[:, LANES:]
    kga, kgb = kg_ref[:, :LANES], kg_ref[:, LANES:]
    G = 2 * LANES

    x = h_ref[...]
    hb_ref[...] = (x * _rms(x, 1.0 / D_MODEL) * g_ref[...]).astype(bf16)

    def proj(w_ref, lo):
        return _dot_nt(hb_ref[...], w_ref[lo:lo + G, :])

    cq = proj(whead_ref, C_CQ)
    cqn_ref[...] = (cq * _rms(cq, 1.0 / Q_LORA) * qag_ref[...]).astype(bf16)
    ckv = proj(whead_ref, C_CKV)
    ckvn_ref[...] = (ckv * _rms(ckv, 1.0 / KV_LORA) * kvag_ref[...]).astype(bf16)
    kpe_ref[...] = proj(whead_ref, C_KPE)

    def norm_rope(ab, ga, gb):
        a, b = ab[:, :LANES], ab[:, LANES:]
        r = lax.rsqrt(jnp.sum(a * a + b * b, axis=-1, keepdims=True) * (1.0 / QK_HEAD) + EPS)
        ao, bo = _rot(a * ga, b * gb, ca_ref[...], sa_ref[...])
        return (ao * r).astype(bf16), (bo * r).astype(bf16)

    def mla_pair(hp):
        lo2 = hp * 2 * QK_PAD
        q2h = _dot(cqn_ref[...], wqb_ref[:, lo2:lo2 + 2 * QK_PAD])
        k2h = _dot(ckvn_ref[...], wkvb_ref[:, lo2:lo2 + 2 * QK_PAD])
        for e in range(2):
            lo = lo2 + e * QK_PAD
            q_out[:, lo:lo + LANES], q_out[:, lo + LANES:lo + QK_PAD] = norm_rope(
                q2h[:, e * QK_PAD:(e + 1) * QK_PAD], qga, qgb)
            k_out[:, lo:lo + LANES], k_out[:, lo + LANES:lo + QK_PAD] = norm_rope(
                k2h[:, e * QK_PAD:(e + 1) * QK_PAD] + kpe_ref[...], kga, kgb)

    def ret_pairs(zq, zk):
        for p in range(RET_HEADS // 2):
            lo = p * RET_PAIR
            k1, k2 = _rot(zk[:, lo:lo + LANES], zk[:, lo + LANES:lo + RET_PAIR], ck_ref[...],
                          sk_ref[...])
            rk_out[:, lo:lo + LANES] = k1.astype(bf16)
            rk_out[:, lo + LANES:lo + RET_PAIR] = k2.astype(bf16)
            q1, q2 = _rot(zq[:, lo:lo + LANES], zq[:, lo + LANES:lo + RET_PAIR], c2_ref[...],
                          s2_ref[...])
            rq_out[:, lo:lo + LANES] = q1.astype(bf16)
            rq_out[:, lo + LANES:lo + RET_PAIR] = q2.astype(bf16)

    rv_out[...] = _dot_nt(hb_ref[...], wvg_ref[:RET_WIDTH, :]).astype(bf16)
    cast(0)
    mla_pair(0)
    rg_out[...] = _dot_nt(hb_ref[...], wvg_ref[RET_WIDTH:, :]).astype(bf16)
    cast(1)
    mla_pair(1)
    zk = _dot_nt(hb_ref[...], wqk_ref[RET_WIDTH:, :])
    cast(2)
    zq = _dot_nt(hb_ref[...], wqk_ref[:RET_WIDTH, :])
    v_out[...] = _dot(ckvn_ref[...], wkvb_ref[:, MLA_HEADS * QK_PAD:]).astype(bf16)
    ret_pairs(zq, zk)


def _pre_call(h, p, l, tabs, tm, name, cast_ws=()):
    B, R, _ = h.shape
    nj = R // tm
    row = lambda w: pl.BlockSpec((None, tm, w), lambda b, j: (b, j, 0))
    tab = pl.BlockSpec((tm, LANES), lambda b, j: (j, 0))
    cast_in, cast_out, cast_shapes = _cast_specs(cast_ws, l, B * nj, lambda b, j: b * nj + j)
    in_specs = [
        row(D_MODEL),
        _gain_spec(l, G_ATTN, D_MODEL),
        _layer_spec(l, (C_RQ, D_MODEL)),
        _layer_spec(l, (C_RV - C_RQ, D_MODEL)),
        _layer_spec(l, (N_IN_PAD - C_RV, D_MODEL)),
        _gain_spec(l, G_QA, Q_LORA),
        _layer_spec(l, (Q_LORA, MLA_HEADS * QK_PAD)),
        _gain_spec(l, G_KVA, KV_LORA),
        _layer_spec(l, (KV_LORA, KV_WIDTH)),
        _gain_spec(l, G_Q, QK_PAD),
        _gain_spec(l, G_K, QK_PAD),
    ] + [tab] * 6 + cast_in
    widths = (MLA_HEADS * QK_PAD, MLA_HEADS * QK_PAD, MLA_WIDTH, RET_WIDTH, RET_WIDTH, RET_WIDTH,
              RET_WIDTH)
    return pl.pallas_call(
        functools.partial(_pre_kernel, n_cast=len(cast_ws)), grid=(B, nj),
        in_specs=in_specs, out_specs=[row(w) for w in widths] + cast_out,
        out_shape=[jax.ShapeDtypeStruct((B, R, w), bf16) for w in widths] + cast_shapes,
        scratch_shapes=[pltpu.VMEM((tm, D_MODEL), bf16), pltpu.VMEM((tm, Q_LORA), bf16),
                        pltpu.VMEM((tm, KV_LORA), bf16), pltpu.VMEM((tm, 2 * LANES), f32)],
        name=name, compiler_params=_params(2),
    )(h, p["gains"], p["w_head"], p["w_qk"], p["w_vg"], p["gains"], p["w_qb"], p["gains"], p["w_kvb"],
      p["gains"], p["gains"], *tabs, *cast_ws)


def _rowmax(s):
    return jnp.max(s, axis=-1, keepdims=True)


def _rowsum(p):
    return jnp.sum(p, axis=-1, keepdims=True)


def _attn_finish(acc, l, og):
    o = acc * (1.0 / l)
    return (o * _rms(o, 1.0 / V_HEAD) * og).astype(bf16)


def _meta_valid():
    return lax.broadcasted_iota(jnp.int32, (1, BLOCK), 1) >= PAD_ROWS


def _attn_kernel(q_ref, k_ref, v_ref, km_ref, vm_ref, og_ref, *rest, n_cast):
    cast_in, o_ref, cast_out, vt_ref = rest[:n_cast], rest[n_cast], rest[n_cast + 1:-1], rest[-1]
    for hh in range(ATT_HPS):
        vv = slice(hh * V_HEAD, (hh + 1) * V_HEAD)
        vt_ref[vv, 0:BLOCK] = vm_ref[:, vv].T
        vt_ref[vv, BLOCK:] = v_ref[:, vv].T
    key_valid = lax.broadcasted_iota(jnp.int32, (BLOCK, 1), 0) >= PAD_ROWS
    tri_t = (lax.broadcasted_iota(jnp.int32, (ATT_TQ, ATT_TQ), 0)
             <= lax.broadcasted_iota(jnp.int32, (ATT_TQ, ATT_TQ), 1))

    def scores(item):
        hh, i = item
        qk = slice(hh * QK_PAD, (hh + 1) * QK_PAD)
        lo, hi = i * ATT_TQ, (i + 1) * ATT_TQ
        q = q_ref[lo:hi, qk]
        s_m = jnp.where(key_valid, _dot_nt(km_ref[:, qk], q), NEG_INF)
        s_x = _dot_nt(k_ref[0:hi, qk], q)
        parts = [s_m] + ([s_x[:lo]] if lo else []) + [jnp.where(tri_t, s_x[lo:], NEG_INF)]
        return jnp.concatenate(parts, axis=0)

    items = [(hh, i) for i in ATT_ORDER for hh in range(ATT_HPS)]
    ahead = [scores(it) for it in items[:ATT_AHEAD]]
    for n, (hh, i) in enumerate(items):
        if n + ATT_AHEAD < len(items):
            ahead.append(scores(items[n + ATT_AHEAD]))
        if n % 2 and n // 2 < n_cast:
            cast_out[n // 2][...] = cast_in[n // 2][...].astype(bf16)
        s = ahead.pop(0)
        vv = slice(hh * V_HEAD, (hh + 1) * V_HEAD)
        lo, hi = i * ATT_TQ, (i + 1) * ATT_TQ
        p = jnp.exp2(s - jnp.max(s, axis=0, keepdims=True))
        l = jnp.sum(p, axis=0, keepdims=True)
        o_t = _dot(vt_ref[vv, 0:BLOCK + hi], p.astype(bf16)) * (1.0 / l)
        r = lax.rsqrt(jnp.sum(o_t * o_t, axis=0, keepdims=True) * (1.0 / V_HEAD) + EPS)
        o_ref[lo:hi, vv] = ((o_t * r).T * og_ref[:, vv]).astype(bf16)


def _attn_meta_kernel(q_ref, k_ref, v_ref, og_ref, o_ref):
    mask = (lax.broadcasted_iota(jnp.int32, (BLOCK, BLOCK), 0)
            >= lax.broadcasted_iota(jnp.int32, (BLOCK, BLOCK), 1)) & _meta_valid()
    s = jnp.where(mask, _dot_nt(q_ref[...], k_ref[...]), NEG_INF)
    p = jnp.exp2(s - _rowmax(s))
    o_ref[...] = _attn_finish(_dot(p.astype(bf16), v_ref[...]), _rowsum(p), og_ref[...])


def _attn_call(q, k, v, km, vm, out_g, l, cast_ws=()):
    B = q.shape[0]
    n = ATT_HPS
    nh = MLA_HEADS // n
    assert 2 * len(cast_ws) <= n * (SEQ // ATT_TQ)
    qk_spec = pl.BlockSpec((None, SEQ, n * QK_PAD), lambda b, h: (b, 0, h))
    v_spec = pl.BlockSpec((None, SEQ, n * V_HEAD), lambda b, h: (b, 0, h))
    cast_in, cast_out, cast_shapes = _cast_specs(cast_ws, l, B * nh, lambda b, h: b * nh + h)
    return pl.pallas_call(
        functools.partial(_attn_kernel, n_cast=len(cast_ws)), grid=(B, nh),
        in_specs=[qk_spec, qk_spec, v_spec,
                  pl.BlockSpec((None, BLOCK, n * QK_PAD), lambda b, h: (0, 0, h)),
                  pl.BlockSpec((None, BLOCK, n * V_HEAD), lambda b, h: (0, 0, h)),
                  pl.BlockSpec((None, 1, n * V_HEAD), lambda b, h: (l, 0, G_OUT // (n * V_HEAD) + h))]
        + cast_in,
        out_specs=[v_spec] + cast_out,
        out_shape=[jax.ShapeDtypeStruct((B, SEQ, MLA_WIDTH), bf16)] + cast_shapes,
        scratch_shapes=[pltpu.VMEM((n * V_HEAD, BLOCK + SEQ), bf16)],
        name="attn", compiler_params=_params(2),
    )(q, k, v, km, vm, out_g, *cast_ws)


def _attn_meta_call(qm, km, vm, out_g, l):
    qk_spec = pl.BlockSpec((None, BLOCK, QK_PAD), lambda h: (0, 0, h))
    v_spec = pl.BlockSpec((None, BLOCK, V_HEAD), lambda h: (0, 0, h))
    return pl.pallas_call(
        _attn_meta_kernel, grid=(MLA_HEADS,),
        in_specs=[qk_spec, qk_spec, v_spec,
                  pl.BlockSpec((None, 1, V_HEAD), lambda h: (l, 0, G_OUT // V_HEAD + h))],
        out_specs=v_spec,
        out_shape=jax.ShapeDtypeStruct((1, BLOCK, MLA_WIDTH), bf16),
        name="attn_meta", compiler_params=_params(1),
    )(qm, km, vm, out_g)


def _ret_tables(lg_ref, C):
    lg = lg_ref[0:1, :]
    lgc = jnp.concatenate([lg] * (C // LANES), axis=1)
    ri = lax.broadcasted_iota(jnp.int32, (C, C), 0)
    ci = lax.broadcasted_iota(jnp.int32, (C, C), 1)
    diff = (ri - ci).astype(f32)
    decay = jnp.where(diff >= 0, jnp.exp(jnp.maximum(diff, 0.0) * lgc), 0.0)
    idx = lax.broadcasted_iota(jnp.int32, (C, RET_HEAD), 0).astype(f32)
    xi = jnp.exp((idx + 1.0) * lg)
    return lg, decay, idx, xi


def _ret_key_state(k, v, idx, lg):
    n = k.shape[0]
    zeta = jnp.exp((n - 1.0 - idx[:n]) * lg)
    vz = (v.astype(f32) * zeta).astype(bf16)
    return _dot_tn(vz, k)


def _ret_mix(q, e, k, v, decay, state_t, xi):
    lane = lax.broadcasted_iota(jnp.int32, (1, RET_PAIR), 1) % LANES
    q = jnp.where((lane < HALF) if e == 0 else (lane >= HALF), q, jnp.zeros_like(q))
    n = q.shape[0]
    o = _dot((_dot_nt(q, k) * decay[:n, :n]).astype(bf16), v)
    if state_t is not None:
        o = o + _dot_nt(q, state_t.astype(bf16)) * xi[:n]
    return o


def _ret_emit(o, g, ng, nb):
    mu = jnp.mean(o, axis=-1, keepdims=True)
    d = o - mu
    var = jnp.mean(d * d, axis=-1, keepdims=True)
    on = d * lax.rsqrt(var + EPS) * ng + nb
    g = g.astype(f32)
    return (g * _sigmoid(g) * on).astype(bf16)


def _ret_kernel(q_ref, k_ref, v_ref, g_ref, km_ref, vm_ref, lg_ref, ng_ref, nb_ref, o_ref):
    C = RET_CHUNK
    n_chunks = SEQ // C
    chunks = [slice(c * C, (c + 1) * C) for c in range(n_chunks)]
    heads = []
    for h in range(2 * RET_PPS):
        lg, decay, idx, xi = _ret_tables(lg_ref.at[h], C)
        hv = slice(h * RET_HEAD, (h + 1) * RET_HEAD)
        pr = slice((h // 2) * RET_PAIR, (h // 2 + 1) * RET_PAIR)
        heads.append(dict(
            lg=lg, decay=decay, idx=idx, xi=xi, hv=hv, pr=pr, e=h % 2,
            chunk_decay=jnp.exp(float(C) * jnp.concatenate([lg, lg], axis=1)),
            state_t=_ret_key_state(km_ref[:, pr], vm_ref[:, hv], idx, lg), o_prev=None))
    for c, rows in enumerate(chunks):
        for hd in heads:
            hv, pr = hd["hv"], hd["pr"]
            o = _ret_mix(q_ref[rows, pr], hd["e"], k_ref[rows, pr], v_ref[rows, hv], hd["decay"],
                         hd["state_t"], hd["xi"])
            if c + 1 < n_chunks:
                hd["state_t"] = hd["state_t"] * hd["chunk_decay"] + _ret_key_state(
                    k_ref[rows, pr], v_ref[rows, hv], hd["idx"], hd["lg"])
            if hd["o_prev"] is not None:
                o_ref[chunks[c - 1], hv] = _ret_emit(hd["o_prev"], g_ref[chunks[c - 1], hv],
                                                     ng_ref[:, hv], nb_ref[:, hv])
            hd["o_prev"] = o
    for hd in heads:
        hv = hd["hv"]
        o_ref[chunks[-1], hv] = _ret_emit(hd["o_prev"], g_ref[chunks[-1], hv], ng_ref[:, hv],
                                          nb_ref[:, hv])


def _ret_meta_kernel(q_ref, k_ref, v_ref, g_ref, lg_ref, ng_ref, nb_ref, o_ref):
    for e in range(2):
        hv = slice(e * RET_HEAD, (e + 1) * RET_HEAD)
        _, decay, _, _ = _ret_tables(lg_ref.at[e], BLOCK)
        o = _ret_mix(q_ref[...], e, k_ref[...], v_ref[:, hv], decay, None, None)
        o_ref[:, hv] = _ret_emit(o, g_ref[:, hv], ng_ref[:, hv], nb_ref[:, hv])


def _ret_call(rq, rk, rv, rg, rkm, rvm, log_g, norm_g, norm_b, l):
    B = rq.shape[0]
    n = RET_PPS
    w = 2 * n * RET_HEAD
    pair = pl.BlockSpec((None, SEQ, n * RET_PAIR), lambda b, p: (b, 0, p))
    spec = pl.BlockSpec((None, SEQ, w), lambda b, p: (b, 0, p))
    vec_g = pl.BlockSpec((None, 1, w), lambda b, p: (l, 0, G_RETG // w + p))
    vec_b = pl.BlockSpec((None, 1, w), lambda b, p: (l, 0, G_RETB // w + p))
    return pl.pallas_call(
        _ret_kernel, grid=(B, RET_HEADS // (2 * n)),
        in_specs=[pair, pair, spec, spec,
                  pl.BlockSpec((None, BLOCK, n * RET_PAIR), lambda b, p: (0, 0, p)),
                  pl.BlockSpec((None, BLOCK, w), lambda b, p: (0, 0, p)),
                  pl.BlockSpec((2 * n, 8, LANES), lambda b, p: (p, 0, 0)), vec_g, vec_b],
        out_specs=spec,
        out_shape=jax.ShapeDtypeStruct((B, SEQ, RET_WIDTH), bf16),
        name="ret", compiler_params=_params(2),
    )(rq, rk, rv, rg, rkm, rvm, log_g, norm_g, norm_b)


def _ret_meta_call(rqm, rkm, rvm, rgm, log_g, norm_g, norm_b, l):
    spec = pl.BlockSpec((None, BLOCK, 2 * RET_HEAD), lambda p: (0, 0, p))
    w = 2 * RET_HEAD
    vec_g = pl.BlockSpec((None, 1, w), lambda p: (l, 0, G_RETG // w + p))
    vec_b = pl.BlockSpec((None, 1, w), lambda p: (l, 0, G_RETB // w + p))
    pair = pl.BlockSpec((None, BLOCK, RET_PAIR), lambda p: (0, 0, p))
    return pl.pallas_call(
        _ret_meta_kernel, grid=(RET_HEADS // 2,),
        in_specs=[pair, pair, spec, spec, pl.BlockSpec((2, 8, LANES), lambda p: (p, 0, 0)),
                  vec_g, vec_b],
        out_specs=spec,
        out_shape=jax.ShapeDtypeStruct((1, BLOCK, RET_WIDTH), bf16),
        name="ret_meta", compiler_params=_params(1),
    )(rqm, rkm, rvm, rgm, log_g, norm_g, norm_b)


def _post_kernel(h_ref, ym_ref, yr_ref, wo_ref, fg_ref, wgu_ref, wd_ref, o_ref, act_ref):
    h1 = (h_ref[...] + _dot(ym_ref[...], wo_ref[0:MLA_WIDTH, :])
          + _dot(yr_ref[...], wo_ref[MLA_WIDTH:MLA_WIDTH + RET_WIDTH, :]))
    hf = (h1 * _rms(h1, 1.0 / D_MODEL) * fg_ref[...]).astype(bf16)
    for c in range(D_FF // FF_TILE):
        lo = c * FF_TILE
        gate = _dot(hf, wgu_ref[:, lo:lo + FF_TILE])
        up = _dot(hf, wgu_ref[:, D_FF + lo:D_FF + lo + FF_TILE])
        act_ref[:, lo:lo + FF_TILE] = (gate * _sigmoid(gate) * up).astype(bf16)
    o_ref[...] = h1 + _dot(act_ref[...], wd_ref[...])


def _post_call(h, ym, yr, p, l, w_bf, tm, name):
    B, R, _ = h.shape
    row = lambda w: pl.BlockSpec((None, tm, w), lambda b, j: (b, j, 0))
    whole = lambda w: pl.BlockSpec(w.shape, lambda *_: (0, 0), pipeline_mode=pl.Buffered(1))
    w_out, w_gu, w_down = w_bf
    assert w_out.shape == (MLA_WIDTH + RET_WIDTH, D_MODEL) and w_gu.shape == (D_MODEL, 2 * D_FF)
    assert w_down.shape == (D_FF, D_MODEL)
    return pl.pallas_call(
        _post_kernel, grid=(B, R // tm),
        in_specs=[row(D_MODEL), row(MLA_WIDTH), row(RET_WIDTH), whole(w_out),
                  _gain_spec(l, G_FFN, D_MODEL), whole(w_gu), whole(w_down)],
        out_specs=row(D_MODEL),
        out_shape=jax.ShapeDtypeStruct((B, R, D_MODEL), f32),
        scratch_shapes=[pltpu.VMEM((tm, D_FF), bf16)],
        name=name, compiler_params=_params(2),
    )(h, ym, yr, w_out, p["gains"], w_gu, w_down)


W_IN_COLS = 2 * Q_LORA + QK_ROPE + 4 * RET_WIDTH
O_KPE = Q_LORA + KV_LORA
O_RQ = O_KPE + QK_ROPE
O_RK = O_RQ + RET_WIDTH
O_RV = O_RK + RET_WIDTH
WPREP_COLS = 256


def _wprep_kernel(w_ref, head_ref, qk_ref, vg_ref):
    def put(dst_ref, dst, src, n):
        dst_ref[dst:dst + n, :] = w_ref[src:src + n, :].astype(bf16)

    q = QK_ROPE // 2
    head_ref[...] = jnp.zeros(head_ref.shape, bf16)
    put(head_ref, 0, 0, O_KPE)
    put(head_ref, C_KPE + HALF, O_KPE, q)
    put(head_ref, C_KPE + LANES + HALF, O_KPE + q, q)
    for s, src0 in enumerate((O_RQ, O_RK)):
        for p in range(RET_HEADS // 2):
            src, dst = src0 + p * RET_PAIR, s * RET_WIDTH + p * RET_PAIR
            put(qk_ref, dst, src, HALF)
            put(qk_ref, dst + HALF, src + 2 * HALF, HALF)
            put(qk_ref, dst + 2 * HALF, src + HALF, HALF)
            put(qk_ref, dst + 3 * HALF, src + 3 * HALF, HALF)
    put(vg_ref, 0, O_RV, 2 * RET_WIDTH)


def _wprep_call(w_in):
    depth = w_in.shape[0]
    wt = jnp.swapaxes(w_in, 1, 2)
    cb = WPREP_COLS
    out = lambda n: pl.BlockSpec((None, n, cb), lambda l, c: (l, 0, c))
    rows = (C_RQ, C_RV - C_RQ, N_IN_PAD - C_RV)
    return pl.pallas_call(
        _wprep_kernel, grid=(depth, D_MODEL // cb),
        in_specs=[out(W_IN_COLS)], out_specs=[out(n) for n in rows],
        out_shape=[jax.ShapeDtypeStruct((depth, n, D_MODEL), bf16) for n in rows],
        name="wprep", compiler_params=_params(2),
    )(wt)


def _rope_tables(pos, valid):
    pos = pos.astype(np.float32)
    n = pos.shape[0]

    def cs(dim):
        inv = np.float32(ROPE_BASE) ** (-np.arange(0, dim, 2, dtype=np.float32) / np.float32(dim))
        ang = pos[:, None] * inv[None, :].astype(np.float32)
        return np.cos(ang).astype(np.float32), np.sin(ang).astype(np.float32)

    cm, sm = cs(QK_ROPE)
    z32 = np.zeros_like(cm)
    ca = np.concatenate([np.ones((n, HALF), np.float32), cm, z32], axis=1)
    sa = np.concatenate([np.zeros((n, HALF), np.float32), sm, z32], axis=1)
    cr, sr = cs(RET_HEAD)
    c2 = np.concatenate([cr, cr], axis=1)
    s2 = np.concatenate([sr, sr], axis=1)
    kscale = valid.astype(np.float32)[:, None] * np.float32(RET_HEAD ** -0.5)
    return tuple(jnp.asarray(t) for t in (ca, sa, c2, s2, c2 * kscale, s2 * kscale))


def _mla_head_layout(a):
    q = QK_ROPE // 2
    z = jnp.zeros(a.shape[:-1] + (q,), a.dtype)
    return jnp.concatenate([a[..., :HALF], a[..., QK_NOPE:QK_NOPE + q], z,
                            a[..., HALF:QK_NOPE], a[..., QK_NOPE + q:], z], axis=-1)


def _prep_params(attn_norm_g, w_in, q_a_norm_g, w_q_b, kv_a_norm_g, w_kv_b, q_norm_g, k_norm_g,
                 mla_out_norm_g, ret_norm_g, ret_norm_b, ffn_norm_g):
    depth = w_in.shape[0]
    w_head, w_qk, w_vg = _wprep_call(w_in)
    wq = _mla_head_layout(w_q_b.astype(bf16).reshape(depth, Q_LORA, MLA_HEADS, QK_HEAD))
    wkv = w_kv_b.astype(bf16).reshape(depth, KV_LORA, MLA_HEADS, QK_NOPE + V_HEAD)
    z64 = jnp.zeros((depth, KV_LORA, MLA_HEADS, HALF), bf16)
    wk = jnp.concatenate([wkv[..., :HALF], z64, wkv[..., HALF:QK_NOPE], z64], axis=-1)
    wkv = jnp.concatenate([wk.reshape(depth, KV_LORA, MLA_HEADS * QK_PAD),
                           wkv[..., QK_NOPE:].reshape(depth, KV_LORA, MLA_WIDTH)], axis=-1)
    scale = QK_HEAD ** -0.5 * LOG2_E
    gains = jnp.concatenate(
        [attn_norm_g, ffn_norm_g, mla_out_norm_g, ret_norm_g, ret_norm_b, q_a_norm_g, kv_a_norm_g,
         _mla_head_layout(q_norm_g * scale), _mla_head_layout(k_norm_g)], axis=-1)[:, None, :]
    assert gains.shape[-1] == G_TOTAL
    return {
        "gains": gains,
        "w_head": w_head,
        "w_qk": w_qk,
        "w_vg": w_vg,
        "w_qb": wq.reshape(depth, Q_LORA, MLA_HEADS * QK_PAD),
        "w_kvb": wkv,
    }


def kernel(x, meta_tokens, attn_norm_g, w_in, q_a_norm_g, w_q_b, kv_a_norm_g, w_kv_b, q_norm_g,
           k_norm_g, mla_out_norm_g, ret_norm_g, ret_norm_b, w_out, ffn_norm_g, w_gate_up, w_down):
    depth = w_in.shape[0]
    hx = x
    hm = jnp.concatenate([jnp.zeros((PAD_ROWS, D_MODEL), x.dtype), meta_tokens.astype(x.dtype)])[None]
    r = np.arange(BLOCK)
    tabs_x = _rope_tables(np.arange(SEQ) + N_META, np.ones(SEQ))
    tabs_m = _rope_tables(np.maximum(r - PAD_ROWS, 0), r >= PAD_ROWS)
    gamma = np.float32(1.0) - np.float32(2.0) ** (np.float32(-5.0) - np.arange(RET_HEADS, dtype=np.float32))
    log_g = jnp.asarray(np.broadcast_to(np.log(gamma)[:, None, None], (RET_HEADS, 8, LANES)))
    p = _prep_params(attn_norm_g, w_in, q_a_norm_g, w_q_b, kv_a_norm_g, w_kv_b, q_norm_g, k_norm_g,
                     mla_out_norm_g, ret_norm_g, ret_norm_b, ffn_norm_g)
    for l in range(depth):
        q, k, v, rq, rk, rv, rg = _pre_call(hx, p, l, tabs_x, PRE_TM, "pre")
        qm, km, vm, rqm, rkm, rvm, rgm = _pre_call(hm, p, l, tabs_m, BLOCK, "pre_meta")
        y_mla, *w_bf = _attn_call(q, k, v, km, vm, p["gains"], l, (w_out, w_gate_up, w_down))
        y_ret = _ret_call(rq, rk, rv, rg, rkm, rvm, log_g, p["gains"], p["gains"], l)
        hx = _post_call(hx, y_mla, y_ret, p, l, w_bf, ROW_TM, "post")
        if l + 1 < depth:
            ym_mla = _attn_meta_call(qm, km, vm, p["gains"], l)
            ym_ret = _ret_meta_call(rqm, rkm, rvm, rgm, log_g, p["gains"], p["gains"], l)
            hm = _post_call(hm, ym_mla, ym_ret, p, l, w_bf, BLOCK, "post_meta")
    return hx
```

```python
import functools

import numpy as np

import jax
import jax.numpy as jnp
from jax import lax
from jax.experimental import pallas as pl
from jax.experimental.pallas import tpu as pltpu

D_MODEL = 1024
SEQ = 2048
N_META = 16
BLOCK = 128
MLA_HEADS = 4
Q_LORA = 256
KV_LORA = 256
QK_NOPE = 128
QK_ROPE = 64
QK_HEAD = QK_NOPE + QK_ROPE
V_HEAD = 128
MLA_WIDTH = MLA_HEADS * V_HEAD
RET_HEADS = 4
RET_HEAD = 128
RET_WIDTH = RET_HEADS * RET_HEAD
D_FF = 2816
ROPE_BASE = 10000.0
EPS = 1e-6
NEG_INF = -1e30

LANES = 128
HALF = LANES // 2
PAD_ROWS = BLOCK - N_META
QK_PAD = 2 * LANES
KV_WIDTH = MLA_HEADS * QK_PAD + MLA_WIDTH
RET_PAIR = 2 * RET_HEAD

C_CQ = 0
C_CKV = C_CQ + Q_LORA
C_KPE = C_CKV + KV_LORA
C_RQ = C_KPE + 2 * LANES
C_RK = C_RQ + RET_WIDTH
C_RV = C_RK + RET_WIDTH
C_RG = C_RV + RET_WIDTH
N_IN_PAD = C_RG + RET_WIDTH

ROW_TM = 1024
PRE_TM = 1024
ATT_TQ = 512
ATT_ORDER = (0, 1, 2, 3)
ATT_HPS = 2
ATT_AHEAD = 2
RET_CHUNK = 256
RET_PPS = 2
FF_TILE = 256
assert D_FF % FF_TILE == 0
LOG2_E = 1.4426950408889634
VMEM_LIMIT = 56 * 1024 * 1024

f32 = jnp.float32
bf16 = jnp.bfloat16


def _dot(a, b):
    return jnp.dot(a, b, preferred_element_type=f32)


def _dot_nt(a, b):
    return lax.dot_general(a, b, (((1,), (1,)), ((), ())), preferred_element_type=f32)


def _dot_tn(a, b):
    return lax.dot_general(a, b, (((0,), (0,)), ((), ())), preferred_element_type=f32)


def _rms(x, inv_n):
    return lax.rsqrt(jnp.sum(x * x, axis=-1, keepdims=True) * inv_n + EPS)


def _sigmoid(x):
    return 1.0 / (1.0 + jnp.exp(-x))


def _rot(a, b, c, s):
    return a * c - b * s, b * c + a * s


def _params(n_grid_axes):
    return pltpu.CompilerParams(dimension_semantics=("parallel",) * n_grid_axes,
                                vmem_limit_bytes=VMEM_LIMIT)


G_ATTN = 0
G_FFN = G_ATTN + D_MODEL
G_OUT = G_FFN + D_MODEL
G_RETG = G_OUT + MLA_WIDTH
G_RETB = G_RETG + RET_WIDTH
G_QA = G_RETB + RET_WIDTH
G_KVA = G_QA + Q_LORA
G_Q = G_KVA + KV_LORA
G_K = G_Q + QK_PAD
G_TOTAL = G_K + QK_PAD


def _gain_spec(l, off, width):
    assert off % width == 0
    return pl.BlockSpec((None, 1, width), lambda *_: (l, 0, off // width),
                        pipeline_mode=pl.Buffered(1))


def _layer_spec(l, shape):
    nd = len(shape)
    return pl.BlockSpec((None,) + shape, lambda *_: (l,) + (0,) * nd, pipeline_mode=pl.Buffered(1))


def _cast_specs(cast_ws, l, steps, step_of):
    slabs = []
    for w in cast_ws:
        assert w.shape[1] % (steps * 16) == 0, w.shape
        slabs.append((w.shape[1] // steps, w.shape[2]))
    return ([pl.BlockSpec((None,) + s, lambda *g: (l, step_of(*g), 0)) for s in slabs],
            [pl.BlockSpec(s, lambda *g: (step_of(*g), 0)) for s in slabs],
            [jax.ShapeDtypeStruct(w.shape[1:], bf16) for w in cast_ws])


def _pre_kernel(h_ref, g_ref, whead_ref, wqk_ref, wvg_ref, qag_ref, wqb_ref, kvag_ref, wkvb_ref,
                qg_ref, kg_ref, ca_ref, sa_ref, c2_ref, s2_ref, ck_ref, sk_ref,
                q_out, k_out, v_out, rq_out, rk_out, rv_out, rg_out, hb_ref, cqn_ref, ckvn_ref,
                kpe_ref):
    qga, qgb = qg_ref[:, :LANES], qg_ref[:, LANES:]
    kga, kgb = kg_ref[:, :LANES], kg_ref[:, LANES:]
    G = 2 * LANES

    x = h_ref[...]
    hb_ref[...] = (x * _rms(x, 1.0 / D_MODEL) * g_ref[...]).astype(bf16)

    def proj(w_ref, lo):
        return _dot_nt(hb_ref[...], w_ref[lo:lo + G, :])

    cq = proj(whead_ref, C_CQ)
    cqn_ref[...] = (cq * _rms(cq, 1.0 / Q_LORA) * qag_ref[...]).astype(bf16)
    ckv = proj(whead_ref, C_CKV)
    ckvn_ref[...] = (ckv * _rms(ckv, 1.0 / KV_LORA) * kvag_ref[...]).astype(bf16)
    kpe_ref[...] = proj(whead_ref, C_KPE)

    def norm_rope(ab, ga, gb):
        a, b = ab[:, :LANES], ab[:, LANES:]
        r = lax.rsqrt(jnp.sum(a * a + b * b, axis=-1, keepdims=True) * (1.0 / QK_HEAD) + EPS)
        ao, bo = _rot(a * ga, b * gb, ca_ref[...], sa_ref[...])
        return (ao * r).astype(bf16), (bo * r).astype(bf16)

    def mla_pair(hp):
        lo2 = hp * 2 * QK_PAD
        q2h = _dot(cqn_ref[...], wqb_ref[:, lo2:lo2 + 2 * QK_PAD])
        k2h = _dot(ckvn_ref[...], wkvb_ref[:, lo2:lo2 + 2 * QK_PAD])
        for e in range(2):
            lo = lo2 + e * QK_PAD
            q_out[:, lo:lo + LANES], q_out[:, lo + LANES:lo + QK_PAD] = norm_rope(
                q2h[:, e * QK_PAD:(e + 1) * QK_PAD], qga, qgb)
            k_out[:, lo:lo + LANES], k_out[:, lo + LANES:lo + QK_PAD] = norm_rope(
                k2h[:, e * QK_PAD:(e + 1) * QK_PAD] + kpe_ref[...], kga, kgb)

    def ret_pairs(zq, zk):
        for p in range(RET_HEADS // 2):
            lo = p * RET_PAIR
            k1, k2 = _rot(zk[:, lo:lo + LANES], zk[:, lo + LANES:lo + RET_PAIR], ck_ref[...],
                          sk_ref[...])
            rk_out[:, lo:lo + LANES] = k1.astype(bf16)
            rk_out[:, lo + LANES:lo + RET_PAIR] = k2.astype(bf16)
            q1, q2 = _rot(zq[:, lo:lo + LANES], zq[:, lo + LANES:lo + RET_PAIR], c2_ref[...],
                          s2_ref[...])
            rq_out[:, lo:lo + LANES] = q1.astype(bf16)
            rq_out[:, lo + LANES:lo + RET_PAIR] = q2.astype(bf16)

    rv_out[...] = _dot_nt(hb_ref[...], wvg_ref[:RET_WIDTH, :]).astype(bf16)
    mla_pair(0)
    rg_out[...] = _dot_nt(hb_ref[...], wvg_ref[RET_WIDTH:, :]).astype(bf16)
    mla_pair(1)
    zk = _dot_nt(hb_ref[...], wqk_ref[RET_WIDTH:, :])
    zq = _dot_nt(hb_ref[...], wqk_ref[:RET_WIDTH, :])
    v_out[...] = _dot(ckvn_ref[...], wkvb_ref[:, MLA_HEADS * QK_PAD:]).astype(bf16)
    ret_pairs(zq, zk)


def _pre_call(h, p, l, tabs, tm, name):
    B, R, _ = h.shape
    row = lambda w: pl.BlockSpec((None, tm, w), lambda b, j: (b, j, 0))
    tab = pl.BlockSpec((tm, LANES), lambda b, j: (j, 0))
    in_specs = [
        row(D_MODEL),
        _gain_spec(l, G_ATTN, D_MODEL),
        _layer_spec(l, (C_RQ, D_MODEL)),
        _layer_spec(l, (C_RV - C_RQ, D_MODEL)),
        _layer_spec(l, (N_IN_PAD - C_RV, D_MODEL)),
        _gain_spec(l, G_QA, Q_LORA),
        _layer_spec(l, (Q_LORA, MLA_HEADS * QK_PAD)),
        _gain_spec(l, G_KVA, KV_LORA),
        _layer_spec(l, (KV_LORA, KV_WIDTH)),
        _gain_spec(l, G_Q, QK_PAD),
        _gain_spec(l, G_K, QK_PAD),
    ] + [tab] * 6
    widths = (MLA_HEADS * QK_PAD, MLA_HEADS * QK_PAD, MLA_WIDTH, RET_WIDTH, RET_WIDTH, RET_WIDTH,
              RET_WIDTH)
    return pl.pallas_call(
        _pre_kernel, grid=(B, R // tm),
        in_specs=in_specs, out_specs=[row(w) for w in widths],
        out_shape=[jax.ShapeDtypeStruct((B, R, w), bf16) for w in widths],
        scratch_shapes=[pltpu.VMEM((tm, D_MODEL), bf16), pltpu.VMEM((tm, Q_LORA), bf16),
                        pltpu.VMEM((tm, KV_LORA), bf16), pltpu.VMEM((tm, 2 * LANES), f32)],
        name=name, compiler_params=_params(2),
    )(h, p["gains"], p["w_head"], p["w_qk"], p["w_vg"], p["gains"], p["w_qb"], p["gains"], p["w_kvb"],
      p["gains"], p["gains"], *tabs)


def _rowmax(s):
    return jnp.max(s, axis=-1, keepdims=True)


def _rowsum(p):
    return jnp.sum(p, axis=-1, keepdims=True)


def _attn_finish(acc, l, og):
    o = acc * (1.0 / l)
    return (o * _rms(o, 1.0 / V_HEAD) * og).astype(bf16)


def _meta_valid():
    return lax.broadcasted_iota(jnp.int32, (1, BLOCK), 1) >= PAD_ROWS


def _attn_kernel(q_ref, k_ref, v_ref, km_ref, vm_ref, og_ref, *rest, n_cast):
    cast_in, o_ref, cast_out, vt_ref = rest[:n_cast], rest[n_cast], rest[n_cast + 1:-1], rest[-1]
    for hh in range(ATT_HPS):
        vv = slice(hh * V_HEAD, (hh + 1) * V_HEAD)
        vt_ref[vv, 0:BLOCK] = vm_ref[:, vv].T
        vt_ref[vv, BLOCK:] = v_ref[:, vv].T
    key_valid = lax.broadcasted_iota(jnp.int32, (BLOCK, 1), 0) >= PAD_ROWS
    tri_t = (lax.broadcasted_iota(jnp.int32, (ATT_TQ, ATT_TQ), 0)
             <= lax.broadcasted_iota(jnp.int32, (ATT_TQ, ATT_TQ), 1))

    def scores(item):
        hh, i = item
        qk = slice(hh * QK_PAD, (hh + 1) * QK_PAD)
        lo, hi = i * ATT_TQ, (i + 1) * ATT_TQ
        q = q_ref[lo:hi, qk]
        s_m = jnp.where(key_valid, _dot_nt(km_ref[:, qk], q), NEG_INF)
        s_x = _dot_nt(k_ref[0:hi, qk], q)
        parts = [s_m] + ([s_x[:lo]] if lo else []) + [jnp.where(tri_t, s_x[lo:], NEG_INF)]
        return jnp.concatenate(parts, axis=0)

    items = [(hh, i) for i in ATT_ORDER for hh in range(ATT_HPS)]
    ahead = [scores(it) for it in items[:ATT_AHEAD]]
    for n, (hh, i) in enumerate(items):
        if n + ATT_AHEAD < len(items):
            ahead.append(scores(items[n + ATT_AHEAD]))
        if n % 2 and n // 2 < n_cast:
            cast_out[n // 2][...] = cast_in[n // 2][...].astype(bf16)
        s = ahead.pop(0)
        vv = slice(hh * V_HEAD, (hh + 1) * V_HEAD)
        lo, hi = i * ATT_TQ, (i + 1) * ATT_TQ
        p = jnp.exp2(s - jnp.max(s, axis=0, keepdims=True))
        l = jnp.sum(p, axis=0, keepdims=True)
        o_t = _dot(vt_ref[vv, 0:BLOCK + hi], p.astype(bf16)) * (1.0 / l)
        r = lax.rsqrt(jnp.sum(o_t * o_t, axis=0, keepdims=True) * (1.0 / V_HEAD) + EPS)
        o_ref[lo:hi, vv] = ((o_t * r).T * og_ref[:, vv]).astype(bf16)


def _attn_meta_kernel(q_ref, k_ref, v_ref, og_ref, o_ref):
    mask = (lax.broadcasted_iota(jnp.int32, (BLOCK, BLOCK), 0)
            >= lax.broadcasted_iota(jnp.int32, (BLOCK, BLOCK), 1)) & _meta_valid()
    s = jnp.where(mask, _dot_nt(q_ref[...], k_ref[...]), NEG_INF)
    p = jnp.exp2(s - _rowmax(s))
    o_ref[...] = _attn_finish(_dot(p.astype(bf16), v_ref[...]), _rowsum(p), og_ref[...])


def _attn_call(q, k, v, km, vm, out_g, l, cast_ws=()):
    B = q.shape[0]
    n = ATT_HPS
    nh = MLA_HEADS // n
    assert 2 * len(cast_ws) <= n * (SEQ // ATT_TQ)
    qk_spec = pl.BlockSpec((None, SEQ, n * QK_PAD), lambda b, h: (b, 0, h))
    v_spec = pl.BlockSpec((None, SEQ, n * V_HEAD), lambda b, h: (b, 0, h))
    cast_in, cast_out, cast_shapes = _cast_specs(cast_ws, l, B * nh, lambda b, h: b * nh + h)
    return pl.pallas_call(
        functools.partial(_attn_kernel, n_cast=len(cast_ws)), grid=(B, nh),
        in_specs=[qk_spec, qk_spec, v_spec,
                  pl.BlockSpec((None, BLOCK, n * QK_PAD), lambda b, h: (0, 0, h)),
                  pl.BlockSpec((None, BLOCK, n * V_HEAD), lambda b, h: (0, 0, h)),
                  pl.BlockSpec((None, 1, n * V_HEAD), lambda b, h: (l, 0, G_OUT // (n * V_HEAD) + h))]
        + cast_in,
        out_specs=[v_spec] + cast_out,
        out_shape=[jax.ShapeDtypeStruct((B, SEQ, MLA_WIDTH), bf16)] + cast_shapes,
        scratch_shapes=[pltpu.VMEM((n * V_HEAD, BLOCK + SEQ), bf16)],
        name="attn", compiler_params=_params(2),
    )(q, k, v, km, vm, out_g, *cast_ws)


def _attn_meta_call(qm, km, vm, out_g, l):
    qk_spec = pl.BlockSpec((None, BLOCK, QK_PAD), lambda h: (0, 0, h))
    v_spec = pl.BlockSpec((None, BLOCK, V_HEAD), lambda h: (0, 0, h))
    return pl.pallas_call(
        _attn_meta_kernel, grid=(MLA_HEADS,),
        in_specs=[qk_spec, qk_spec, v_spec,
                  pl.BlockSpec((None, 1, V_HEAD), lambda h: (l, 0, G_OUT // V_HEAD + h))],
        out_specs=v_spec,
        out_shape=jax.ShapeDtypeStruct((1, BLOCK, MLA_WIDTH), bf16),
        name="attn_meta", compiler_params=_params(1),
    )(qm, km, vm, out_g)


def _ret_tables(lg_ref, C):
    lg = lg_ref[0:1, :]
    lgc = jnp.concatenate([lg] * (C // LANES), axis=1)
    ri = lax.broadcasted_iota(jnp.int32, (C, C), 0)
    ci = lax.broadcasted_iota(jnp.int32, (C, C), 1)
    diff = (ri - ci).astype(f32)
    decay = jnp.where(diff >= 0, jnp.exp(jnp.maximum(diff, 0.0) * lgc), 0.0)
    idx = lax.broadcasted_iota(jnp.int32, (C, RET_HEAD), 0).astype(f32)
    xi = jnp.exp((idx + 1.0) * lg)
    return lg, decay, idx, xi


def _ret_key_state(k, v, idx, lg):
    n = k.shape[0]
    zeta = jnp.exp((n - 1.0 - idx[:n]) * lg)
    vz = (v.astype(f32) * zeta).astype(bf16)
    return _dot_tn(vz, k)


def _ret_mix(q, e, k, v, decay, state_t, xi):
    lane = lax.broadcasted_iota(jnp.int32, (1, RET_PAIR), 1) % LANES
    q = jnp.where((lane < HALF) if e == 0 else (lane >= HALF), q, jnp.zeros_like(q))
    n = q.shape[0]
    o = _dot((_dot_nt(q, k) * decay[:n, :n]).astype(bf16), v)
    if state_t is not None:
        o = o + _dot_nt(q, state_t.astype(bf16)) * xi[:n]
    return o


def _ret_emit(o, g, ng, nb):
    mu = jnp.mean(o, axis=-1, keepdims=True)
    d = o - mu
    var = jnp.mean(d * d, axis=-1, keepdims=True)
    on = d * lax.rsqrt(var + EPS) * ng + nb
    g = g.astype(f32)
    return (g * _sigmoid(g) * on).astype(bf16)


def _ret_kernel(q_ref, k_ref, v_ref, g_ref, km_ref, vm_ref, lg_ref, ng_ref, nb_ref, o_ref):
    C = RET_CHUNK
    n_chunks = SEQ // C
    chunks = [slice(c * C, (c + 1) * C) for c in range(n_chunks)]
    heads = []
    for h in range(2 * RET_PPS):
        lg, decay, idx, xi = _ret_tables(lg_ref.at[h], C)
        hv = slice(h * RET_HEAD, (h + 1) * RET_HEAD)
        pr = slice((h // 2) * RET_PAIR, (h // 2 + 1) * RET_PAIR)
        heads.append(dict(
            lg=lg, decay=decay, idx=idx, xi=xi, hv=hv, pr=pr, e=h % 2,
            chunk_decay=jnp.exp(float(C) * jnp.concatenate([lg, lg], axis=1)),
            state_t=_ret_key_state(km_ref[:, pr], vm_ref[:, hv], idx, lg), o_prev=None))
    for c, rows in enumerate(chunks):
        for hd in heads:
            hv, pr = hd["hv"], hd["pr"]
            o = _ret_mix(q_ref[rows, pr], hd["e"], k_ref[rows, pr], v_ref[rows, hv], hd["decay"],
                         hd["state_t"], hd["xi"])
            if c + 1 < n_chunks:
                hd["state_t"] = hd["state_t"] * hd["chunk_decay"] + _ret_key_state(
                    k_ref[rows, pr], v_ref[rows, hv], hd["idx"], hd["lg"])
            if hd["o_prev"] is not None:
                o_ref[chunks[c - 1], hv] = _ret_emit(hd["o_prev"], g_ref[chunks[c - 1], hv],
                                                     ng_ref[:, hv], nb_ref[:, hv])
            hd["o_prev"] = o
    for hd in heads:
        hv = hd["hv"]
        o_ref[chunks[-1], hv] = _ret_emit(hd["o_prev"], g_ref[chunks[-1], hv], ng_ref[:, hv],
                                          nb_ref[:, hv])


def _ret_meta_kernel(q_ref, k_ref, v_ref, g_ref, lg_ref, ng_ref, nb_ref, o_ref):
    for e in range(2):
        hv = slice(e * RET_HEAD, (e + 1) * RET_HEAD)
        _, decay, _, _ = _ret_tables(lg_ref.at[e], BLOCK)
        o = _ret_mix(q_ref[...], e, k_ref[...], v_ref[:, hv], decay, None, None)
        o_ref[:, hv] = _ret_emit(o, g_ref[:, hv], ng_ref[:, hv], nb_ref[:, hv])


def _ret_call(rq, rk, rv, rg, rkm, rvm, log_g, norm_g, norm_b, l):
    B = rq.shape[0]
    n = RET_PPS
    w = 2 * n * RET_HEAD
    pair = pl.BlockSpec((None, SEQ, n * RET_PAIR), lambda b, p: (b, 0, p))
    spec = pl.BlockSpec((None, SEQ, w), lambda b, p: (b, 0, p))
    vec_g = pl.BlockSpec((None, 1, w), lambda b, p: (l, 0, G_RETG // w + p))
    vec_b = pl.BlockSpec((None, 1, w), lambda b, p: (l, 0, G_RETB // w + p))
    return pl.pallas_call(
        _ret_kernel, grid=(B, RET_HEADS // (2 * n)),
        in_specs=[pair, pair, spec, spec,
                  pl.BlockSpec((None, BLOCK, n * RET_PAIR), lambda b, p: (0, 0, p)),
                  pl.BlockSpec((None, BLOCK, w), lambda b, p: (0, 0, p)),
                  pl.BlockSpec((2 * n, 8, LANES), lambda b, p: (p, 0, 0)), vec_g, vec_b],
        out_specs=spec,
        out_shape=jax.ShapeDtypeStruct((B, SEQ, RET_WIDTH), bf16),
        name="ret", compiler_params=_params(2),
    )(rq, rk, rv, rg, rkm, rvm, log_g, norm_g, norm_b)


def _ret_meta_call(rqm, rkm, rvm, rgm, log_g, norm_g, norm_b, l):
    spec = pl.BlockSpec((None, BLOCK, 2 * RET_HEAD), lambda p: (0, 0, p))
    w = 2 * RET_HEAD
    vec_g = pl.BlockSpec((None, 1, w), lambda p: (l, 0, G_RETG // w + p))
    vec_b = pl.BlockSpec((None, 1, w), lambda p: (l, 0, G_RETB // w + p))
    pair = pl.BlockSpec((None, BLOCK, RET_PAIR), lambda p: (0, 0, p))
    return pl.pallas_call(
        _ret_meta_kernel, grid=(RET_HEADS // 2,),
        in_specs=[pair, pair, spec, spec, pl.BlockSpec((2, 8, LANES), lambda p: (p, 0, 0)),
                  vec_g, vec_b],
        out_specs=spec,
        out_shape=jax.ShapeDtypeStruct((1, BLOCK, RET_WIDTH), bf16),
        name="ret_meta", compiler_params=_params(1),
    )(rqm, rkm, rvm, rgm, log_g, norm_g, norm_b)


def _post_kernel(h_ref, ym_ref, yr_ref, wo_ref, fg_ref, wgu_ref, wd_ref, o_ref, act_ref):
    h1 = (h_ref[...] + _dot(ym_ref[...], wo_ref[0:MLA_WIDTH, :])
          + _dot(yr_ref[...], wo_ref[MLA_WIDTH:MLA_WIDTH + RET_WIDTH, :]))
    hf = (h1 * _rms(h1, 1.0 / D_MODEL) * fg_ref[...]).astype(bf16)
    for c in range(D_FF // FF_TILE):
        lo = c * FF_TILE
        gate = _dot(hf, wgu_ref[:, lo:lo + FF_TILE])
        up = _dot(hf, wgu_ref[:, D_FF + lo:D_FF + lo + FF_TILE])
        act_ref[:, lo:lo + FF_TILE] = (gate * _sigmoid(gate) * up).astype(bf16)
    o_ref[...] = h1 + _dot(act_ref[...], wd_ref[...])


def _post_call(h, ym, yr, p, l, w_bf, tm, name):
    B, R, _ = h.shape
    row = lambda w: pl.BlockSpec((None, tm, w), lambda b, j: (b, j, 0))
    whole = lambda w: pl.BlockSpec(w.shape, lambda *_: (0, 0), pipeline_mode=pl.Buffered(1))
    w_out, w_gu, w_down = w_bf
    assert w_out.shape == (MLA_WIDTH + RET_WIDTH, D_MODEL) and w_gu.shape == (D_MODEL, 2 * D_FF)
    assert w_down.shape == (D_FF, D_MODEL)
    return pl.pallas_call(
        _post_kernel, grid=(B, R // tm),
        in_specs=[row(D_MODEL), row(MLA_WIDTH), row(RET_WIDTH), whole(w_out),
                  _gain_spec(l, G_FFN, D_MODEL), whole(w_gu), whole(w_down)],
        out_specs=row(D_MODEL),
        out_shape=jax.ShapeDtypeStruct((B, R, D_MODEL), f32),
        scratch_shapes=[pltpu.VMEM((tm, D_FF), bf16)],
        name=name, compiler_params=_params(2),
    )(h, ym, yr, w_out, p["gains"], w_gu, w_down)


W_IN_COLS = 2 * Q_LORA + QK_ROPE + 4 * RET_WIDTH
O_KPE = Q_LORA + KV_LORA
O_RQ = O_KPE + QK_ROPE
O_RK = O_RQ + RET_WIDTH
O_RV = O_RK + RET_WIDTH
WPREP_COLS = 256


def _wprep_kernel(w_ref, head_ref, qk_ref, vg_ref):
    def put(dst_ref, dst, src, n):
        dst_ref[dst:dst + n, :] = w_ref[src:src + n, :].astype(bf16)

    q = QK_ROPE // 2
    head_ref[...] = jnp.zeros(head_ref.shape, bf16)
    put(head_ref, 0, 0, O_KPE)
    put(head_ref, C_KPE + HALF, O_KPE, q)
    put(head_ref, C_KPE + LANES + HALF, O_KPE + q, q)
    for s, src0 in enumerate((O_RQ, O_RK)):
        for p in range(RET_HEADS // 2):
            src, dst = src0 + p * RET_PAIR, s * RET_WIDTH + p * RET_PAIR
            put(qk_ref, dst, src, HALF)
            put(qk_ref, dst + HALF, src + 2 * HALF, HALF)
            put(qk_ref, dst + 2 * HALF, src + HALF, HALF)
            put(qk_ref, dst + 3 * HALF, src + 3 * HALF, HALF)
    put(vg_ref, 0, O_RV, 2 * RET_WIDTH)


def _wprep_call(w_in):
    depth = w_in.shape[0]
    wt = jnp.swapaxes(w_in, 1, 2)
    cb = WPREP_COLS
    out = lambda n: pl.BlockSpec((None, n, cb), lambda l, c: (l, 0, c))
    rows = (C_RQ, C_RV - C_RQ, N_IN_PAD - C_RV)
    return pl.pallas_call(
        _wprep_kernel, grid=(depth, D_MODEL // cb),
        in_specs=[out(W_IN_COLS)], out_specs=[out(n) for n in rows],
        out_shape=[jax.ShapeDtypeStruct((depth, n, D_MODEL), bf16) for n in rows],
        name="wprep", compiler_params=_params(2),
    )(wt)


def _rope_tables(pos, valid):
    pos = pos.astype(np.float32)
    n = pos.shape[0]

    def cs(dim):
        inv = np.float32(ROPE_BASE) ** (-np.arange(0, dim, 2, dtype=np.float32) / np.float32(dim))
        ang = pos[:, None] * inv[None, :].astype(np.float32)
        return np.cos(ang).astype(np.float32), np.sin(ang).astype(np.float32)

    cm, sm = cs(QK_ROPE)
    z32 = np.zeros_like(cm)
    ca = np.concatenate([np.ones((n, HALF), np.float32), cm, z32], axis=1)
    sa = np.concatenate([np.zeros((n, HALF), np.float32), sm, z32], axis=1)
    cr, sr = cs(RET_HEAD)
    c2 = np.concatenate([cr, cr], axis=1)
    s2 = np.concatenate([sr, sr], axis=1)
    kscale = valid.astype(np.float32)[:, None] * np.float32(RET_HEAD ** -0.5)
    return tuple(jnp.asarray(t) for t in (ca, sa, c2, s2, c2 * kscale, s2 * kscale))


def _mla_head_layout(a):
    q = QK_ROPE // 2
    z = jnp.zeros(a.shape[:-1] + (q,), a.dtype)
    return jnp.concatenate([a[..., :HALF], a[..., QK_NOPE:QK_NOPE + q], z,
                            a[..., HALF:QK_NOPE], a[..., QK_NOPE + q:], z], axis=-1)


def _prep_params(attn_norm_g, w_in, q_a_norm_g, w_q_b, kv_a_norm_g, w_kv_b, q_norm_g, k_norm_g,
                 mla_out_norm_g, ret_norm_g, ret_norm_b, ffn_norm_g):
    depth = w_in.shape[0]
    w_head, w_qk, w_vg = _wprep_call(w_in)
    wq = _mla_head_layout(w_q_b.astype(bf16).reshape(depth, Q_LORA, MLA_HEADS, QK_HEAD))
    wkv = w_kv_b.astype(bf16).reshape(depth, KV_LORA, MLA_HEADS, QK_NOPE + V_HEAD)
    z64 = jnp.zeros((depth, KV_LORA, MLA_HEADS, HALF), bf16)
    wk = jnp.concatenate([wkv[..., :HALF], z64, wkv[..., HALF:QK_NOPE], z64], axis=-1)
    wkv = jnp.concatenate([wk.reshape(depth, KV_LORA, MLA_HEADS * QK_PAD),
                           wkv[..., QK_NOPE:].reshape(depth, KV_LORA, MLA_WIDTH)], axis=-1)
    scale = QK_HEAD ** -0.5 * LOG2_E
    gains = jnp.concatenate(
        [attn_norm_g, ffn_norm_g, mla_out_norm_g, ret_norm_g, ret_norm_b, q_a_norm_g, kv_a_norm_g,
         _mla_head_layout(q_norm_g * scale), _mla_head_layout(k_norm_g)], axis=-1)[:, None, :]
    assert gains.shape[-1] == G_TOTAL
    return {
        "gains": gains,
        "w_head": w_head,
        "w_qk": w_qk,
        "w_vg": w_vg,
        "w_qb": wq.reshape(depth, Q_LORA, MLA_HEADS * QK_PAD),
        "w_kvb": wkv,
    }


def kernel(x, meta_tokens, attn_norm_g, w_in, q_a_norm_g, w_q_b, kv_a_norm_g, w_kv_b, q_norm_g,
           k_norm_g, mla_out_norm_g, ret_norm_g, ret_norm_b, w_out, ffn_norm_g, w_gate_up, w_down):
    depth = w_in.shape[0]
    hx = x
    hm = jnp.concatenate([jnp.zeros((PAD_ROWS, D_MODEL), x.dtype), meta_tokens.astype(x.dtype)])[None]
    r = np.arange(BLOCK)
    tabs_x = _rope_tables(np.arange(SEQ) + N_META, np.ones(SEQ))
    tabs_m = _rope_tables(np.maximum(r - PAD_ROWS, 0), r >= PAD_ROWS)
    gamma = np.float32(1.0) - np.float32(2.0) ** (np.float32(-5.0) - np.arange(RET_HEADS, dtype=np.float32))
    log_g = jnp.asarray(np.broadcast_to(np.log(gamma)[:, None, None], (RET_HEADS, 8, LANES)))
    p = _prep_params(attn_norm_g, w_in, q_a_norm_g, w_q_b, kv_a_norm_g, w_kv_b, q_norm_g, k_norm_g,
                     mla_out_norm_g, ret_norm_g, ret_norm_b, ffn_norm_g)
    for l in range(depth):
        q, k, v, rq, rk, rv, rg = _pre_call(hx, p, l, tabs_x, PRE_TM, "pre")
        qm, km, vm, rqm, rkm, rvm, rgm = _pre_call(hm, p, l, tabs_m, BLOCK, "pre_meta")
        y_mla, *w_bf = _attn_call(q, k, v, km, vm, p["gains"], l, (w_out, w_gate_up, w_down))
        y_ret = _ret_call(rq, rk, rv, rg, rkm, rvm, log_g, p["gains"], p["gains"], l)
        hx = _post_call(hx, y_mla, y_ret, p, l, w_bf, ROW_TM, "post")
        if l + 1 < depth:
            ym_mla = _attn_meta_call(qm, km, vm, p["gains"], l)
            ym_ret = _ret_meta_call(rqm, rkm, rvm, rgm, log_g, p["gains"], p["gains"], l)
            hm = _post_call(hm, ym_mla, ym_ret, p, l, w_bf, BLOCK, "post_meta")
    return hx
```

```python
import functools

import numpy as np

import jax
import jax.numpy as jnp
from jax import lax
from jax.experimental import pallas as pl
from jax.experimental.pallas import tpu as pltpu

D_MODEL = 1024
SEQ = 2048
N_META = 16
BLOCK = 128
MLA_HEADS = 4
Q_LORA = 256
KV_LORA = 256
QK_NOPE = 128
QK_ROPE = 64
QK_HEAD = QK_NOPE + QK_ROPE
V_HEAD = 128
MLA_WIDTH = MLA_HEADS * V_HEAD
RET_HEADS = 4
RET_HEAD = 128
RET_WIDTH = RET_HEADS * RET_HEAD
D_FF = 2816
ROPE_BASE = 10000.0
EPS = 1e-6
NEG_INF = -1e30

LANES = 128
HALF = LANES // 2
PAD_ROWS = BLOCK - N_META
QK_PAD = 2 * LANES
KV_WIDTH = MLA_HEADS * QK_PAD + MLA_WIDTH
RET_PAIR = 2 * RET_HEAD

C_CQ = 0
C_CKV = C_CQ + Q_LORA
C_KPE = C_CKV + KV_LORA
C_RQ = C_KPE + 2 * LANES
C_RK = C_RQ + RET_WIDTH
C_RV = C_RK + RET_WIDTH
C_RG = C_RV + RET_WIDTH
N_IN_PAD = C_RG + RET_WIDTH

ROW_TM = 1024
PRE_TM = 1024
ATT_TQ = 512
ATT_ORDER = (0, 1, 2, 3)
ATT_HPS = 2
ATT_AHEAD = 2
RET_CHUNK = 256
RET_PPS = 2
FF_TILE = 256
assert D_FF % FF_TILE == 0
LOG2_E = 1.4426950408889634
VMEM_LIMIT = 56 * 1024 * 1024

f32 = jnp.float32
bf16 = jnp.bfloat16


def _dot(a, b):
    return jnp.dot(a, b, preferred_element_type=f32)


def _dot_nt(a, b):
    return lax.dot_general(a, b, (((1,), (1,)), ((), ())), preferred_element_type=f32)


def _dot_tn(a, b):
    return lax.dot_general(a, b, (((0,), (0,)), ((), ())), preferred_element_type=f32)


def _rms(x, inv_n):
    return lax.rsqrt(jnp.sum(x * x, axis=-1, keepdims=True) * inv_n + EPS)


def _sigmoid(x):
    return 1.0 / (1.0 + jnp.exp(-x))


def _rot(a, b, c, s):
    return a * c - b * s, b * c + a * s


def _params(n_grid_axes):
    return pltpu.CompilerParams(dimension_semantics=("parallel",) * n_grid_axes,
                                vmem_limit_bytes=VMEM_LIMIT)


G_ATTN = 0
G_FFN = G_ATTN + D_MODEL
G_OUT = G_FFN + D_MODEL
G_RETG = G_OUT + MLA_WIDTH
G_RETB = G_RETG + RET_WIDTH
G_QA = G_RETB + RET_WIDTH
G_KVA = G_QA + Q_LORA
G_Q = G_KVA + KV_LORA
G_K = G_Q + QK_PAD
G_TOTAL = G_K + QK_PAD


def _gain_spec(l, off, width):
    assert off % width == 0
    return pl.BlockSpec((None, 1, width), lambda *_: (l, 0, off // width),
                        pipeline_mode=pl.Buffered(1))


def _layer_spec(l, shape):
    nd = len(shape)
    return pl.BlockSpec((None,) + shape, lambda *_: (l,) + (0,) * nd, pipeline_mode=pl.Buffered(1))


def _cast_specs(cast_ws, l, steps, step_of):
    slabs = []
    for w in cast_ws:
        assert w.shape[1] % (steps * 16) == 0, w.shape
        slabs.append((w.shape[1] // steps, w.shape[2]))
    return ([pl.BlockSpec((None,) + s, lambda *g: (l, step_of(*g), 0)) for s in slabs],
            [pl.BlockSpec(s, lambda *g: (step_of(*g), 0)) for s in slabs],
            [jax.ShapeDtypeStruct(w.shape[1:], bf16) for w in cast_ws])


def _pre_kernel(h_ref, g_ref, whead_ref, wqk_ref, wvg_ref, qag_ref, wqb_ref, kvag_ref, wkvb_ref,
                qg_ref, kg_ref, ca_ref, sa_ref, c2_ref, s2_ref, ck_ref, sk_ref,
                q_out, k_out, v_out, rq_out, rk_out, rv_out, rg_out, hb_ref, cqn_ref, ckvn_ref,
                kpe_ref):
    qga, qgb = qg_ref[:, :LANES], qg_ref[:, LANES:]
    kga, kgb = kg_ref[:, :LANES], kg_ref[:, LANES:]
    G = 2 * LANES

    x = h_ref[...]
    hb_ref[...] = (x * _rms(x, 1.0 / D_MODEL) * g_ref[...]).astype(bf16)

    def proj(w_ref, lo):
        return _dot_nt(hb_ref[...], w_ref[lo:lo + G, :])

    cq = proj(whead_ref, C_CQ)
    cqn_ref[...] = (cq * _rms(cq, 1.0 / Q_LORA) * qag_ref[...]).astype(bf16)
    ckv = proj(whead_ref, C_CKV)
    ckvn_ref[...] = (ckv * _rms(ckv, 1.0 / KV_LORA) * kvag_ref[...]).astype(bf16)
    kpe_ref[...] = proj(whead_ref, C_KPE)

    def norm_rope(ab, ga, gb):
        a, b = ab[:, :LANES], ab[:, LANES:]
        r = lax.rsqrt(jnp.sum(a * a + b * b, axis=-1, keepdims=True) * (1.0 / QK_HEAD) + EPS)
        ao, bo = _rot(a * ga, b * gb, ca_ref[...], sa_ref[...])
        return (ao * r).astype(bf16), (bo * r).astype(bf16)

    def mla_pair(hp):
        lo2 = hp * 2 * QK_PAD
        q2h = _dot(cqn_ref[...], wqb_ref[:, lo2:lo2 + 2 * QK_PAD])
        k2h = _dot(ckvn_ref[...], wkvb_ref[:, lo2:lo2 + 2 * QK_PAD])
        for e in range(2):
            lo = lo2 + e * QK_PAD
            q_out[:, lo:lo + LANES], q_out[:, lo + LANES:lo + QK_PAD] = norm_rope(
                q2h[:, e * QK_PAD:(e + 1) * QK_PAD], qga, qgb)
            k_out[:, lo:lo + LANES], k_out[:, lo + LANES:lo + QK_PAD] = norm_rope(
                k2h[:, e * QK_PAD:(e + 1) * QK_PAD] + kpe_ref[...], kga, kgb)

    def ret_pairs(zq, zk):
        for p in range(RET_HEADS // 2):
            lo = p * RET_PAIR
            k1, k2 = _rot(zk[:, lo:lo + LANES], zk[:, lo + LANES:lo + RET_PAIR], ck_ref[...],
                          sk_ref[...])
            rk_out[:, lo:lo + LANES] = k1.astype(bf16)
            rk_out[:, lo + LANES:lo + RET_PAIR] = k2.astype(bf16)
            q1, q2 = _rot(zq[:, lo:lo + LANES], zq[:, lo + LANES:lo + RET_PAIR], c2_ref[...],
                          s2_ref[...])
            rq_out[:, lo:lo + LANES] = q1.astype(bf16)
            rq_out[:, lo + LANES:lo + RET_PAIR] = q2.astype(bf16)

    rv_out[...] = _dot_nt(hb_ref[...], wvg_ref[:RET_WIDTH, :]).astype(bf16)
    mla_pair(0)
    rg_out[...] = _dot_nt(hb_ref[...], wvg_ref[RET_WIDTH:, :]).astype(bf16)
    mla_pair(1)
    zk = _dot_nt(hb_ref[...], wqk_ref[RET_WIDTH:, :])
    zq = _dot_nt(hb_ref[...], wqk_ref[:RET_WIDTH, :])
    v_out[...] = _dot(ckvn_ref[...], wkvb_ref[:, MLA_HEADS * QK_PAD:]).astype(bf16)
    ret_pairs(zq, zk)


def _pre_call(h, p, l, tabs, tm, name):
    B, R, _ = h.shape
    row = lambda w: pl.BlockSpec((None, tm, w), lambda b, j: (b, j, 0))
    tab = pl.BlockSpec((tm, LANES), lambda b, j: (j, 0))
    in_specs = [
        row(D_MODEL),
        _gain_spec(l, G_ATTN, D_MODEL),
        _layer_spec(l, (C_RQ, D_MODEL)),
        _layer_spec(l, (C_RV - C_RQ, D_MODEL)),
        _layer_spec(l, (N_IN_PAD - C_RV, D_MODEL)),
        _gain_spec(l, G_QA, Q_LORA),
        _layer_spec(l, (Q_LORA, MLA_HEADS * QK_PAD)),
        _gain_spec(l, G_KVA, KV_LORA),
        _layer_spec(l, (KV_LORA, KV_WIDTH)),
        _gain_spec(l, G_Q, QK_PAD),
        _gain_spec(l, G_K, QK_PAD),
    ] + [tab] * 6
    widths = (MLA_HEADS * QK_PAD, MLA_HEADS * QK_PAD, MLA_WIDTH, RET_WIDTH, RET_WIDTH, RET_WIDTH,
              RET_WIDTH)
    return pl.pallas_call(
        _pre_kernel, grid=(B, R // tm),
        in_specs=in_specs, out_specs=[row(w) for w in widths],
        out_shape=[jax.ShapeDtypeStruct((B, R, w), bf16) for w in widths],
        scratch_shapes=[pltpu.VMEM((tm, D_MODEL), bf16), pltpu.VMEM((tm, Q_LORA), bf16),
                        pltpu.VMEM((tm, KV_LORA), bf16), pltpu.VMEM((tm, 2 * LANES), f32)],
        name=name, compiler_params=_params(2),
    )(h, p["gains"], p["w_head"], p["w_qk"], p["w_vg"], p["gains"], p["w_qb"], p["gains"], p["w_kvb"],
      p["gains"], p["gains"], *tabs)


def _rowmax(s):
    return jnp.max(s, axis=-1, keepdims=True)


def _rowsum(p):
    return jnp.sum(p, axis=-1, keepdims=True)


def _attn_finish(acc, l, og):
    o = acc * (1.0 / l)
    return (o * _rms(o, 1.0 / V_HEAD) * og).astype(bf16)


def _meta_valid():
    return lax.broadcasted_iota(jnp.int32, (1, BLOCK), 1) >= PAD_ROWS


def _attn_kernel(q_ref, k_ref, v_ref, km_ref, vm_ref, og_ref, *rest, n_cast):
    cast_in, o_ref, cast_out, vt_ref = rest[:n_cast], rest[n_cast], rest[n_cast + 1:-1], rest[-1]
    for hh in range(ATT_HPS):
        vv = slice(hh * V_HEAD, (hh + 1) * V_HEAD)
        vt_ref[vv, 0:BLOCK] = vm_ref[:, vv].T
        vt_ref[vv, BLOCK:] = v_ref[:, vv].T
    key_valid = lax.broadcasted_iota(jnp.int32, (BLOCK, 1), 0) >= PAD_ROWS
    H = ATT_TQ // 2
    tri_main = (lax.broadcasted_iota(jnp.int32, (H, ATT_TQ), 0)
                <= lax.broadcasted_iota(jnp.int32, (H, ATT_TQ), 1))
    tri_tail = (lax.broadcasted_iota(jnp.int32, (H, H), 0)
                <= lax.broadcasted_iota(jnp.int32, (H, H), 1))

    def scores(item):
        hh, i = item
        qk = slice(hh * QK_PAD, (hh + 1) * QK_PAD)
        lo, hi = i * ATT_TQ, (i + 1) * ATT_TQ
        q = q_ref[lo:hi, qk]
        s_m = jnp.where(key_valid, _dot_nt(km_ref[:, qk], q), NEG_INF)
        s_x = _dot_nt(k_ref[0:lo + H, qk], q)
        parts = [s_m] + ([s_x[:lo]] if lo else []) + [jnp.where(tri_main, s_x[lo:], NEG_INF)]
        tail = jnp.where(tri_tail, _dot_nt(k_ref[lo + H:hi, qk], q[H:]), NEG_INF)
        return jnp.concatenate(parts, axis=0), tail

    items = [(hh, i) for i in ATT_ORDER for hh in range(ATT_HPS)]
    ahead = [scores(it) for it in items[:ATT_AHEAD]]
    for n, (hh, i) in enumerate(items):
        if n + ATT_AHEAD < len(items):
            ahead.append(scores(items[n + ATT_AHEAD]))
        if n % 2 and n // 2 < n_cast:
            cast_out[n // 2][...] = cast_in[n // 2][...].astype(bf16)
        s, s_tail = ahead.pop(0)
        vv = slice(hh * V_HEAD, (hh + 1) * V_HEAD)
        lo, hi = i * ATT_TQ, (i + 1) * ATT_TQ
        mid = BLOCK + lo + H
        m = jnp.max(s, axis=0, keepdims=True)
        m_hi = jnp.maximum(m[:, H:], jnp.max(s_tail, axis=0, keepdims=True))
        p = jnp.exp2(s - jnp.concatenate([m[:, :H], m_hi], axis=1))
        p_tail = jnp.exp2(s_tail - m_hi)
        l = jnp.sum(p, axis=0, keepdims=True)
        l = jnp.concatenate([l[:, :H], l[:, H:] + jnp.sum(p_tail, axis=0, keepdims=True)], axis=1)
        o_t = _dot(vt_ref[vv, 0:mid], p.astype(bf16))
        o_tail = _dot(vt_ref[vv, mid:BLOCK + hi], p_tail.astype(bf16))
        o_t = jnp.concatenate([o_t[:, :H], o_t[:, H:] + o_tail], axis=1) * (1.0 / l)
        r = lax.rsqrt(jnp.sum(o_t * o_t, axis=0, keepdims=True) * (1.0 / V_HEAD) + EPS)
        o_ref[lo:hi, vv] = ((o_t * r).T * og_ref[:, vv]).astype(bf16)


def _attn_meta_kernel(q_ref, k_ref, v_ref, og_ref, o_ref):
    mask = (lax.broadcasted_iota(jnp.int32, (BLOCK, BLOCK), 0)
            >= lax.broadcasted_iota(jnp.int32, (BLOCK, BLOCK), 1)) & _meta_valid()
    s = jnp.where(mask, _dot_nt(q_ref[...], k_ref[...]), NEG_INF)
    p = jnp.exp2(s - _rowmax(s))
    o_ref[...] = _attn_finish(_dot(p.astype(bf16), v_ref[...]), _rowsum(p), og_ref[...])


def _attn_call(q, k, v, km, vm, out_g, l, cast_ws=()):
    B = q.shape[0]
    n = ATT_HPS
    nh = MLA_HEADS // n
    assert 2 * len(cast_ws) <= n * (SEQ // ATT_TQ)
    qk_spec = pl.BlockSpec((None, SEQ, n * QK_PAD), lambda b, h: (b, 0, h))
    v_spec = pl.BlockSpec((None, SEQ, n * V_HEAD), lambda b, h: (b, 0, h))
    cast_in, cast_out, cast_shapes = _cast_specs(cast_ws, l, B * nh, lambda b, h: b * nh + h)
    return pl.pallas_call(
        functools.partial(_attn_kernel, n_cast=len(cast_ws)), grid=(B, nh),
        in_specs=[qk_spec, qk_spec, v_spec,
                  pl.BlockSpec((None, BLOCK, n * QK_PAD), lambda b, h: (0, 0, h)),
                  pl.BlockSpec((None, BLOCK, n * V_HEAD), lambda b, h: (0, 0, h)),
                  pl.BlockSpec((None, 1, n * V_HEAD), lambda b, h: (l, 0, G_OUT // (n * V_HEAD) + h))]
        + cast_in,
        out_specs=[v_spec] + cast_out,
        out_shape=[jax.ShapeDtypeStruct((B, SEQ, MLA_WIDTH), bf16)] + cast_shapes,
        scratch_shapes=[pltpu.VMEM((n * V_HEAD, BLOCK + SEQ), bf16)],
        name="attn", compiler_params=_params(2),
    )(q, k, v, km, vm, out_g, *cast_ws)


def _attn_meta_call(qm, km, vm, out_g, l):
    qk_spec = pl.BlockSpec((None, BLOCK, QK_PAD), lambda h: (0, 0, h))
    v_spec = pl.BlockSpec((None, BLOCK, V_HEAD), lambda h: (0, 0, h))
    return pl.pallas_call(
        _attn_meta_kernel, grid=(MLA_HEADS,),
        in_specs=[qk_spec, qk_spec, v_spec,
                  pl.BlockSpec((None, 1, V_HEAD), lambda h: (l, 0, G_OUT // V_HEAD + h))],
        out_specs=v_spec,
        out_shape=jax.ShapeDtypeStruct((1, BLOCK, MLA_WIDTH), bf16),
        name="attn_meta", compiler_params=_params(1),
    )(qm, km, vm, out_g)


def _ret_tables(lg_ref, C):
    lg = lg_ref[0:1, :]
    lgc = jnp.concatenate([lg] * (C // LANES), axis=1)
    ri = lax.broadcasted_iota(jnp.int32, (C, C), 0)
    ci = lax.broadcasted_iota(jnp.int32, (C, C), 1)
    diff = (ri - ci).astype(f32)
    decay = jnp.where(diff >= 0, jnp.exp(jnp.maximum(diff, 0.0) * lgc), 0.0)
    idx = lax.broadcasted_iota(jnp.int32, (C, RET_HEAD), 0).astype(f32)
    xi = jnp.exp((idx + 1.0) * lg)
    return lg, decay, idx, xi


def _ret_key_state(k, v, idx, lg):
    n = k.shape[0]
    zeta = jnp.exp((n - 1.0 - idx[:n]) * lg)
    vz = (v.astype(f32) * zeta).astype(bf16)
    return _dot_tn(vz, k)


def _ret_mix(q, e, k, v, decay, state_t, xi):
    lane = lax.broadcasted_iota(jnp.int32, (1, RET_PAIR), 1) % LANES
    q = jnp.where((lane < HALF) if e == 0 else (lane >= HALF), q, jnp.zeros_like(q))
    n = q.shape[0]
    o = _dot((_dot_nt(q, k) * decay[:n, :n]).astype(bf16), v)
    if state_t is not None:
        o = o + _dot_nt(q, state_t.astype(bf16)) * xi[:n]
    return o


def _ret_emit(o, g, ng, nb):
    mu = jnp.mean(o, axis=-1, keepdims=True)
    d = o - mu
    var = jnp.mean(d * d, axis=-1, keepdims=True)
    on = d * lax.rsqrt(var + EPS) * ng + nb
    g = g.astype(f32)
    return (g * _sigmoid(g) * on).astype(bf16)


def _ret_kernel(q_ref, k_ref, v_ref, g_ref, km_ref, vm_ref, lg_ref, ng_ref, nb_ref, o_ref):
    C = RET_CHUNK
    n_chunks = SEQ // C
    chunks = [slice(c * C, (c + 1) * C) for c in range(n_chunks)]
    heads = []
    for h in range(2 * RET_PPS):
        lg, decay, idx, xi = _ret_tables(lg_ref.at[h], C)
        hv = slice(h * RET_HEAD, (h + 1) * RET_HEAD)
        pr = slice((h // 2) * RET_PAIR, (h // 2 + 1) * RET_PAIR)
        heads.append(dict(
            lg=lg, decay=decay, idx=idx, xi=xi, hv=hv, pr=pr, e=h % 2,
            chunk_decay=jnp.exp(float(C) * jnp.concatenate([lg, lg], axis=1)),
            state_t=_ret_key_state(km_ref[:, pr], vm_ref[:, hv], idx, lg), o_prev=None))
    for c, rows in enumerate(chunks):
        for hd in heads:
            hv, pr = hd["hv"], hd["pr"]
            o = _ret_mix(q_ref[rows, pr], hd["e"], k_ref[rows, pr], v_ref[rows, hv], hd["decay"],
                         hd["state_t"], hd["xi"])
            if c + 1 < n_chunks:
                hd["state_t"] = hd["state_t"] * hd["chunk_decay"] + _ret_key_state(
                    k_ref[rows, pr], v_ref[rows, hv], hd["idx"], hd["lg"])
            if hd["o_prev"] is not None:
                o_ref[chunks[c - 1], hv] = _ret_emit(hd["o_prev"], g_ref[chunks[c - 1], hv],
                                                     ng_ref[:, hv], nb_ref[:, hv])
            hd["o_prev"] = o
    for hd in heads:
        hv = hd["hv"]
        o_ref[chunks[-1], hv] = _ret_emit(hd["o_prev"], g_ref[chunks[-1], hv], ng_ref[:, hv],
                                          nb_ref[:, hv])


def _ret_meta_kernel(q_ref, k_ref, v_ref, g_ref, lg_ref, ng_ref, nb_ref, o_ref):
    for e in range(2):
        hv = slice(e * RET_HEAD, (e + 1) * RET_HEAD)
        _, decay, _, _ = _ret_tables(lg_ref.at[e], BLOCK)
        o = _ret_mix(q_ref[...], e, k_ref[...], v_ref[:, hv], decay, None, None)
        o_ref[:, hv] = _ret_emit(o, g_ref[:, hv], ng_ref[:, hv], nb_ref[:, hv])


def _ret_call(rq, rk, rv, rg, rkm, rvm, log_g, norm_g, norm_b, l):
    B = rq.shape[0]
    n = RET_PPS
    w = 2 * n * RET_HEAD
    pair = pl.BlockSpec((None, SEQ, n * RET_PAIR), lambda b, p: (b, 0, p))
    spec = pl.BlockSpec((None, SEQ, w), lambda b, p: (b, 0, p))
    vec_g = pl.BlockSpec((None, 1, w), lambda b, p: (l, 0, G_RETG // w + p))
    vec_b = pl.BlockSpec((None, 1, w), lambda b, p: (l, 0, G_RETB // w + p))
    return pl.pallas_call(
        _ret_kernel, grid=(B, RET_HEADS // (2 * n)),
        in_specs=[pair, pair, spec, spec,
                  pl.BlockSpec((None, BLOCK, n * RET_PAIR), lambda b, p: (0, 0, p)),
                  pl.BlockSpec((None, BLOCK, w), lambda b, p: (0, 0, p)),
                  pl.BlockSpec((2 * n, 8, LANES), lambda b, p: (p, 0, 0)), vec_g, vec_b],
        out_specs=spec,
        out_shape=jax.ShapeDtypeStruct((B, SEQ, RET_WIDTH), bf16),
        name="ret", compiler_params=_params(2),
    )(rq, rk, rv, rg, rkm, rvm, log_g, norm_g, norm_b)


def _ret_meta_call(rqm, rkm, rvm, rgm, log_g, norm_g, norm_b, l):
    spec = pl.BlockSpec((None, BLOCK, 2 * RET_HEAD), lambda p: (0, 0, p))
    w = 2 * RET_HEAD
    vec_g = pl.BlockSpec((None, 1, w), lambda p: (l, 0, G_RETG // w + p))
    vec_b = pl.BlockSpec((None, 1, w), lambda p: (l, 0, G_RETB // w + p))
    pair = pl.BlockSpec((None, BLOCK, RET_PAIR), lambda p: (0, 0, p))
    return pl.pallas_call(
        _ret_meta_kernel, grid=(RET_HEADS // 2,),
        in_specs=[pair, pair, spec, spec, pl.BlockSpec((2, 8, LANES), lambda p: (p, 0, 0)),
                  vec_g, vec_b],
        out_specs=spec,
        out_shape=jax.ShapeDtypeStruct((1, BLOCK, RET_WIDTH), bf16),
        name="ret_meta", compiler_params=_params(1),
    )(rqm, rkm, rvm, rgm, log_g, norm_g, norm_b)


def _post_kernel(h_ref, ym_ref, yr_ref, wo_ref, fg_ref, wgu_ref, wd_ref, o_ref, act_ref):
    h1 = (h_ref[...] + _dot(ym_ref[...], wo_ref[0:MLA_WIDTH, :])
          + _dot(yr_ref[...], wo_ref[MLA_WIDTH:MLA_WIDTH + RET_WIDTH, :]))
    hf = (h1 * _rms(h1, 1.0 / D_MODEL) * fg_ref[...]).astype(bf16)
    for c in range(D_FF // FF_TILE):
        lo = c * FF_TILE
        gate = _dot(hf, wgu_ref[:, lo:lo + FF_TILE])
        up = _dot(hf, wgu_ref[:, D_FF + lo:D_FF + lo + FF_TILE])
        act_ref[:, lo:lo + FF_TILE] = (gate * _sigmoid(gate) * up).astype(bf16)
    o_ref[...] = h1 + _dot(act_ref[...], wd_ref[...])


def _post_call(h, ym, yr, p, l, w_bf, tm, name):
    B, R, _ = h.shape
    row = lambda w: pl.BlockSpec((None, tm, w), lambda b, j: (b, j, 0))
    whole = lambda w: pl.BlockSpec(w.shape, lambda *_: (0, 0), pipeline_mode=pl.Buffered(1))
    w_out, w_gu, w_down = w_bf
    assert w_out.shape == (MLA_WIDTH + RET_WIDTH, D_MODEL) and w_gu.shape == (D_MODEL, 2 * D_FF)
    assert w_down.shape == (D_FF, D_MODEL)
    return pl.pallas_call(
        _post_kernel, grid=(B, R // tm),
        in_specs=[row(D_MODEL), row(MLA_WIDTH), row(RET_WIDTH), whole(w_out),
                  _gain_spec(l, G_FFN, D_MODEL), whole(w_gu), whole(w_down)],
        out_specs=row(D_MODEL),
        out_shape=jax.ShapeDtypeStruct((B, R, D_MODEL), f32),
        scratch_shapes=[pltpu.VMEM((tm, D_FF), bf16)],
        name=name, compiler_params=_params(2),
    )(h, ym, yr, w_out, p["gains"], w_gu, w_down)


W_IN_COLS = 2 * Q_LORA + QK_ROPE + 4 * RET_WIDTH
O_KPE = Q_LORA + KV_LORA
O_RQ = O_KPE + QK_ROPE
O_RK = O_RQ + RET_WIDTH
O_RV = O_RK + RET_WIDTH
WPREP_COLS = 256


def _wprep_kernel(w_ref, head_ref, qk_ref, vg_ref):
    def put(dst_ref, dst, src, n):
        dst_ref[dst:dst + n, :] = w_ref[src:src + n, :].astype(bf16)

    q = QK_ROPE // 2
    head_ref[...] = jnp.zeros(head_ref.shape, bf16)
    put(head_ref, 0, 0, O_KPE)
    put(head_ref, C_KPE + HALF, O_KPE, q)
    put(head_ref, C_KPE + LANES + HALF, O_KPE + q, q)
    for s, src0 in enumerate((O_RQ, O_RK)):
        for p in range(RET_HEADS // 2):
            src, dst = src0 + p * RET_PAIR, s * RET_WIDTH + p * RET_PAIR
            put(qk_ref, dst, src, HALF)
            put(qk_ref, dst + HALF, src + 2 * HALF, HALF)
            put(qk_ref, dst + 2 * HALF, src + HALF, HALF)
            put(qk_ref, dst + 3 * HALF, src + 3 * HALF, HALF)
    put(vg_ref, 0, O_RV, 2 * RET_WIDTH)


def _wprep_call(w_in):
    depth = w_in.shape[0]
    wt = jnp.swapaxes(w_in, 1, 2)
    cb = WPREP_COLS
    out = lambda n: pl.BlockSpec((None, n, cb), lambda l, c: (l, 0, c))
    rows = (C_RQ, C_RV - C_RQ, N_IN_PAD - C_RV)
    return pl.pallas_call(
        _wprep_kernel, grid=(depth, D_MODEL // cb),
        in_specs=[out(W_IN_COLS)], out_specs=[out(n) for n in rows],
        out_shape=[jax.ShapeDtypeStruct((depth, n, D_MODEL), bf16) for n in rows],
        name="wprep", compiler_params=_params(2),
    )(wt)


def _rope_tables(pos, valid):
    pos = pos.astype(np.float32)
    n = pos.shape[0]

    def cs(dim):
        inv = np.float32(ROPE_BASE) ** (-np.arange(0, dim, 2, dtype=np.float32) / np.float32(dim))
        ang = pos[:, None] * inv[None, :].astype(np.float32)
        return np.cos(ang).astype(np.float32), np.sin(ang).astype(np.float32)

    cm, sm = cs(QK_ROPE)
    z32 = np.zeros_like(cm)
    ca = np.concatenate([np.ones((n, HALF), np.float32), cm, z32], axis=1)
    sa = np.concatenate([np.zeros((n, HALF), np.float32), sm, z32], axis=1)
    cr, sr = cs(RET_HEAD)
    c2 = np.concatenate([cr, cr], axis=1)
    s2 = np.concatenate([sr, sr], axis=1)
    kscale = valid.astype(np.float32)[:, None] * np.float32(RET_HEAD ** -0.5)
    return tuple(jnp.asarray(t) for t in (ca, sa, c2, s2, c2 * kscale, s2 * kscale))


def _mla_head_layout(a):
    q = QK_ROPE // 2
    z = jnp.zeros(a.shape[:-1] + (q,), a.dtype)
    return jnp.concatenate([a[..., :HALF], a[..., QK_NOPE:QK_NOPE + q], z,
                            a[..., HALF:QK_NOPE], a[..., QK_NOPE + q:], z], axis=-1)


def _prep_params(attn_norm_g, w_in, q_a_norm_g, w_q_b, kv_a_norm_g, w_kv_b, q_norm_g, k_norm_g,
                 mla_out_norm_g, ret_norm_g, ret_norm_b, ffn_norm_g):
    depth = w_in.shape[0]
    w_head, w_qk, w_vg = _wprep_call(w_in)
    wq = _mla_head_layout(w_q_b.astype(bf16).reshape(depth, Q_LORA, MLA_HEADS, QK_HEAD))
    wkv = w_kv_b.astype(bf16).reshape(depth, KV_LORA, MLA_HEADS, QK_NOPE + V_HEAD)
    z64 = jnp.zeros((depth, KV_LORA, MLA_HEADS, HALF), bf16)
    wk = jnp.concatenate([wkv[..., :HALF], z64, wkv[..., HALF:QK_NOPE], z64], axis=-1)
    wkv = jnp.concatenate([wk.reshape(depth, KV_LORA, MLA_HEADS * QK_PAD),
                           wkv[..., QK_NOPE:].reshape(depth, KV_LORA, MLA_WIDTH)], axis=-1)
    scale = QK_HEAD ** -0.5 * LOG2_E
    gains = jnp.concatenate(
        [attn_norm_g, ffn_norm_g, mla_out_norm_g, ret_norm_g, ret_norm_b, q_a_norm_g, kv_a_norm_g,
         _mla_head_layout(q_norm_g * scale), _mla_head_layout(k_norm_g)], axis=-1)[:, None, :]
    assert gains.shape[-1] == G_TOTAL
    return {
        "gains": gains,
        "w_head": w_head,
        "w_qk": w_qk,
        "w_vg": w_vg,
        "w_qb": wq.reshape(depth, Q_LORA, MLA_HEADS * QK_PAD),
        "w_kvb": wkv,
    }


def kernel(x, meta_tokens, attn_norm_g, w_in, q_a_norm_g, w_q_b, kv_a_norm_g, w_kv_b, q_norm_g,
           k_norm_g, mla_out_norm_g, ret_norm_g, ret_norm_b, w_out, ffn_norm_g, w_gate_up, w_down):
    depth = w_in.shape[0]
    hx = x
    hm = jnp.concatenate([jnp.zeros((PAD_ROWS, D_MODEL), x.dtype), meta_tokens.astype(x.dtype)])[None]
    r = np.arange(BLOCK)
    tabs_x = _rope_tables(np.arange(SEQ) + N_META, np.ones(SEQ))
    tabs_m = _rope_tables(np.maximum(r - PAD_ROWS, 0), r >= PAD_ROWS)
    gamma = np.float32(1.0) - np.float32(2.0) ** (np.float32(-5.0) - np.arange(RET_HEADS, dtype=np.float32))
    log_g = jnp.asarray(np.broadcast_to(np.log(gamma)[:, None, None], (RET_HEADS, 8, LANES)))
    p = _prep_params(attn_norm_g, w_in, q_a_norm_g, w_q_b, kv_a_norm_g, w_kv_b, q_norm_g, k_norm_g,
                     mla_out_norm_g, ret_norm_g, ret_norm_b, ffn_norm_g)
    for l in range(depth):
        q, k, v, rq, rk, rv, rg = _pre_call(hx, p, l, tabs_x, PRE_TM, "pre")
        qm, km, vm, rqm, rkm, rvm, rgm = _pre_call(hm, p, l, tabs_m, BLOCK, "pre_meta")
        y_mla, *w_bf = _attn_call(q, k, v, km, vm, p["gains"], l, (w_out, w_gate_up, w_down))
        y_ret = _ret_call(rq, rk, rv, rg, rkm, rvm, log_g, p["gains"], p["gains"], l)
        hx = _post_call(hx, y_mla, y_ret, p, l, w_bf, ROW_TM, "post")
        if l + 1 < depth:
            ym_mla = _attn_meta_call(qm, km, vm, p["gains"], l)
            ym_ret = _ret_meta_call(rqm, rkm, rvm, rgm, log_g, p["gains"], p["gains"], l)
            hm = _post_call(hm, ym_mla, ym_ret, p, l, w_bf, BLOCK, "post_meta")
    return hx
```

```python
import functools

import numpy as np

import jax
import jax.numpy as jnp
from jax import lax
from jax.experimental import pallas as pl
from jax.experimental.pallas import tpu as pltpu

D_MODEL = 1024
SEQ = 2048
N_META = 16
BLOCK = 128
MLA_HEADS = 4
Q_LORA = 256
KV_LORA = 256
QK_NOPE = 128
QK_ROPE = 64
QK_HEAD = QK_NOPE + QK_ROPE
V_HEAD = 128
MLA_WIDTH = MLA_HEADS * V_HEAD
RET_HEADS = 4
RET_HEAD = 128
RET_WIDTH = RET_HEADS * RET_HEAD
D_FF = 2816
ROPE_BASE = 10000.0
EPS = 1e-6
NEG_INF = -1e30

LANES = 128
HALF = LANES // 2
PAD_ROWS = BLOCK - N_META
QK_PAD = 2 * LANES
KV_WIDTH = MLA_HEADS * QK_PAD + MLA_WIDTH
RET_PAIR = 2 * RET_HEAD

C_CQ = 0
C_CKV = C_CQ + Q_LORA
C_KPE = C_CKV + KV_LORA
C_RQ = C_KPE + 2 * LANES
C_RK = C_RQ + RET_WIDTH
C_RV = C_RK + RET_WIDTH
C_RG = C_RV + RET_WIDTH
N_IN_PAD = C_RG + RET_WIDTH

ROW_TM = 1024
PRE_TM = 1024
ATT_TQ = 512
ATT_ORDER = (0, 1, 2, 3)
ATT_HPS = 2
ATT_AHEAD = 2
RET_CHUNK = 256
RET_PPS = 2
FF_TILE = 256
assert D_FF % FF_TILE == 0
LOG2_E = 1.4426950408889634
VMEM_LIMIT = 56 * 1024 * 1024

f32 = jnp.float32
bf16 = jnp.bfloat16


def _dot(a, b):
    return jnp.dot(a, b, preferred_element_type=f32)


def _dot_nt(a, b):
    return lax.dot_general(a, b, (((1,), (1,)), ((), ())), preferred_element_type=f32)


def _dot_tn(a, b):
    return lax.dot_general(a, b, (((0,), (0,)), ((), ())), preferred_element_type=f32)


def _rms(x, inv_n):
    return lax.rsqrt(jnp.sum(x * x, axis=-1, keepdims=True) * inv_n + EPS)


def _sigmoid(x):
    return 1.0 / (1.0 + jnp.exp(-x))


def _rot(a, b, c, s):
    return a * c - b * s, b * c + a * s


def _params(n_grid_axes):
    return pltpu.CompilerParams(dimension_semantics=("parallel",) * n_grid_axes,
                                vmem_limit_bytes=VMEM_LIMIT)


G_ATTN = 0
G_FFN = G_ATTN + D_MODEL
G_OUT = G_FFN + D_MODEL
G_RETG = G_OUT + MLA_WIDTH
G_RETB = G_RETG + RET_WIDTH
G_QA = G_RETB + RET_WIDTH
G_KVA = G_QA + Q_LORA
G_Q = G_KVA + KV_LORA
G_K = G_Q + QK_PAD
G_TOTAL = G_K + QK_PAD


def _gain_spec(l, off, width):
    assert off % width == 0
    return pl.BlockSpec((None, 1, width), lambda *_: (l, 0, off // width),
                        pipeline_mode=pl.Buffered(1))


def _layer_spec(l, shape):
    nd = len(shape)
    return pl.BlockSpec((None,) + shape, lambda *_: (l,) + (0,) * nd, pipeline_mode=pl.Buffered(1))


def _cast_specs(cast_ws, l, steps, step_of):
    slabs = []
    for w in cast_ws:
        assert w.shape[1] % (steps * 16) == 0, w.shape
        slabs.append((w.shape[1] // steps, w.shape[2]))
    return ([pl.BlockSpec((None,) + s, lambda *g: (l, step_of(*g), 0)) for s in slabs],
            [pl.BlockSpec(s, lambda *g: (step_of(*g), 0)) for s in slabs],
            [jax.ShapeDtypeStruct(w.shape[1:], bf16) for w in cast_ws])


def _pre_kernel(h_ref, g_ref, whead_ref, wqk_ref, wvg_ref, qag_ref, wqb_ref, kvag_ref, wkvb_ref,
                qg_ref, kg_ref, ca_ref, sa_ref, c2_ref, s2_ref, ck_ref, sk_ref,
                q_out, k_out, v_out, rq_out, rk_out, rv_out, rg_out, hb_ref, cqn_ref, ckvn_ref,
                kpe_ref):
    qga, qgb = qg_ref[:, :LANES], qg_ref[:, LANES:]
    kga, kgb = kg_ref[:, :LANES], kg_ref[:, LANES:]
    G = 2 * LANES

    x = h_ref[...]
    hb_ref[...] = (x * _rms(x, 1.0 / D_MODEL) * g_ref[...]).astype(bf16)

    def proj(w_ref, lo):
        return _dot_nt(hb_ref[...], w_ref[lo:lo + G, :])

    cq = proj(whead_ref, C_CQ)
    cqn_ref[...] = (cq * _rms(cq, 1.0 / Q_LORA) * qag_ref[...]).astype(bf16)
    ckv = proj(whead_ref, C_CKV)
    ckvn_ref[...] = (ckv * _rms(ckv, 1.0 / KV_LORA) * kvag_ref[...]).astype(bf16)
    kpe_ref[...] = proj(whead_ref, C_KPE)

    def norm_rope(ab, ga, gb):
        a, b = ab[:, :LANES], ab[:, LANES:]
        r = lax.rsqrt(jnp.sum(a * a + b * b, axis=-1, keepdims=True) * (1.0 / QK_HEAD) + EPS)
        ao, bo = _rot(a * ga, b * gb, ca_ref[...], sa_ref[...])
        return (ao * r).astype(bf16), (bo * r).astype(bf16)

    def mla_pair(hp):
        lo2 = hp * 2 * QK_PAD
        q2h = _dot(cqn_ref[...], wqb_ref[:, lo2:lo2 + 2 * QK_PAD])
        k2h = _dot(ckvn_ref[...], wkvb_ref[:, lo2:lo2 + 2 * QK_PAD])
        for e in range(2):
            lo = lo2 + e * QK_PAD
            q_out[:, lo:lo + LANES], q_out[:, lo + LANES:lo + QK_PAD] = norm_rope(
                q2h[:, e * QK_PAD:(e + 1) * QK_PAD], qga, qgb)
            k_out[:, lo:lo + LANES], k_out[:, lo + LANES:lo + QK_PAD] = norm_rope(
                k2h[:, e * QK_PAD:(e + 1) * QK_PAD] + kpe_ref[...], kga, kgb)

    def ret_pairs(zq, zk):
        for p in range(RET_HEADS // 2):
            lo = p * RET_PAIR
            k1, k2 = _rot(zk[:, lo:lo + LANES], zk[:, lo + LANES:lo + RET_PAIR], ck_ref[...],
                          sk_ref[...])
            rk_out[:, lo:lo + LANES] = k1.astype(bf16)
            rk_out[:, lo + LANES:lo + RET_PAIR] = k2.astype(bf16)
            q1, q2 = _rot(zq[:, lo:lo + LANES], zq[:, lo + LANES:lo + RET_PAIR], c2_ref[...],
                          s2_ref[...])
            rq_out[:, lo:lo + LANES] = q1.astype(bf16)
            rq_out[:, lo + LANES:lo + RET_PAIR] = q2.astype(bf16)

    rv_out[...] = _dot_nt(hb_ref[...], wvg_ref[:RET_WIDTH, :]).astype(bf16)
    mla_pair(0)
    rg_out[...] = _dot_nt(hb_ref[...], wvg_ref[RET_WIDTH:, :]).astype(bf16)
    mla_pair(1)
    zk = _dot_nt(hb_ref[...], wqk_ref[RET_WIDTH:, :])
    zq = _dot_nt(hb_ref[...], wqk_ref[:RET_WIDTH, :])
    v_out[...] = _dot(ckvn_ref[...], wkvb_ref[:, MLA_HEADS * QK_PAD:]).astype(bf16)
    ret_pairs(zq, zk)


def _pre_call(h, p, l, tabs, tm, name):
    B, R, _ = h.shape
    row = lambda w: pl.BlockSpec((None, tm, w), lambda b, j: (b, j, 0))
    tab = pl.BlockSpec((tm, LANES), lambda b, j: (j, 0))
    in_specs = [
        row(D_MODEL),
        _gain_spec(l, G_ATTN, D_MODEL),
        _layer_spec(l, (C_RQ, D_MODEL)),
        _layer_spec(l, (C_RV - C_RQ, D_MODEL)),
        _layer_spec(l, (N_IN_PAD - C_RV, D_MODEL)),
        _gain_spec(l, G_QA, Q_LORA),
        _layer_spec(l, (Q_LORA, MLA_HEADS * QK_PAD)),
        _gain_spec(l, G_KVA, KV_LORA),
        _layer_spec(l, (KV_LORA, KV_WIDTH)),
        _gain_spec(l, G_Q, QK_PAD),
        _gain_spec(l, G_K, QK_PAD),
    ] + [tab] * 6
    widths = (MLA_HEADS * QK_PAD, MLA_HEADS * QK_PAD, MLA_WIDTH, RET_WIDTH, RET_WIDTH, RET_WIDTH,
              RET_WIDTH)
    return pl.pallas_call(
        _pre_kernel, grid=(B, R // tm),
        in_specs=in_specs, out_specs=[row(w) for w in widths],
        out_shape=[jax.ShapeDtypeStruct((B, R, w), bf16) for w in widths],
        scratch_shapes=[pltpu.VMEM((tm, D_MODEL), bf16), pltpu.VMEM((tm, Q_LORA), bf16),
                        pltpu.VMEM((tm, KV_LORA), bf16), pltpu.VMEM((tm, 2 * LANES), f32)],
        name=name, compiler_params=_params(2),
    )(h, p["gains"], p["w_head"], p["w_qk"], p["w_vg"], p["gains"], p["w_qb"], p["gains"], p["w_kvb"],
      p["gains"], p["gains"], *tabs)


def _rowmax(s):
    return jnp.max(s, axis=-1, keepdims=True)


def _rowsum(p):
    return jnp.sum(p, axis=-1, keepdims=True)


def _attn_finish(acc, l, og):
    o = acc * (1.0 / l)
    return (o * _rms(o, 1.0 / V_HEAD) * og).astype(bf16)


def _meta_valid():
    return lax.broadcasted_iota(jnp.int32, (1, BLOCK), 1) >= PAD_ROWS


def _attn_kernel(q_ref, k_ref, v_ref, km_ref, vm_ref, og_ref, *rest, n_cast):
    cast_in, o_ref, cast_out, vt_ref = rest[:n_cast], rest[n_cast], rest[n_cast + 1:-1], rest[-1]
    for hh in range(ATT_HPS):
        vv = slice(hh * V_HEAD, (hh + 1) * V_HEAD)
        vt_ref[vv, 0:BLOCK] = vm_ref[:, vv].T
        vt_ref[vv, BLOCK:] = v_ref[:, vv].T
    key_valid = lax.broadcasted_iota(jnp.int32, (BLOCK, 1), 0) >= PAD_ROWS
    H = ATT_TQ // 2
    tri_main = (lax.broadcasted_iota(jnp.int32, (H, ATT_TQ), 0)
                <= lax.broadcasted_iota(jnp.int32, (H, ATT_TQ), 1))
    tri_tail = (lax.broadcasted_iota(jnp.int32, (H, H), 0)
                <= lax.broadcasted_iota(jnp.int32, (H, H), 1))

    def scores(item):
        hh, i = item
        qk = slice(hh * QK_PAD, (hh + 1) * QK_PAD)
        lo, hi = i * ATT_TQ, (i + 1) * ATT_TQ
        q = q_ref[lo:hi, qk]
        s_m = jnp.where(key_valid, _dot_nt(km_ref[:, qk], q), NEG_INF)
        s_x = _dot_nt(k_ref[0:lo + H, qk], q)
        parts = [s_m] + ([s_x[:lo]] if lo else []) + [jnp.where(tri_main, s_x[lo:], NEG_INF)]
        tail = jnp.where(tri_tail, _dot_nt(k_ref[lo + H:hi, qk], q[H:]), NEG_INF)
        return jnp.concatenate(parts, axis=0), tail

    items = [(hh, i) for i in ATT_ORDER for hh in range(ATT_HPS)]
    ahead = [scores(it) for it in items[:ATT_AHEAD]]
    for n, (hh, i) in enumerate(items):
        if n + ATT_AHEAD < len(items):
            ahead.append(scores(items[n + ATT_AHEAD]))
        if n % 2 and n // 2 < n_cast:
            cast_out[n // 2][...] = cast_in[n // 2][...].astype(bf16)
        s, s_tail = ahead.pop(0)
        vv = slice(hh * V_HEAD, (hh + 1) * V_HEAD)
        lo, hi = i * ATT_TQ, (i + 1) * ATT_TQ
        mid = BLOCK + lo + H
        m = jnp.max(s, axis=0, keepdims=True)
        m_hi = jnp.maximum(m[:, H:], jnp.max(s_tail, axis=0, keepdims=True))
        p = jnp.exp2(s - jnp.concatenate([m[:, :H], m_hi], axis=1))
        p_tail = jnp.exp2(s_tail - m_hi)
        l = jnp.sum(p, axis=0, keepdims=True)
        l = jnp.concatenate([l[:, :H], l[:, H:] + jnp.sum(p_tail, axis=0, keepdims=True)], axis=1)
        o_t = _dot(vt_ref[vv, 0:mid], p.astype(bf16))
        o_tail = _dot(vt_ref[vv, mid:BLOCK + hi], p_tail.astype(bf16))
        o_t = jnp.concatenate([o_t[:, :H], o_t[:, H:] + o_tail], axis=1) * (1.0 / l)
        r = lax.rsqrt(jnp.sum(o_t * o_t, axis=0, keepdims=True) * (1.0 / V_HEAD) + EPS)
        o_ref[lo:hi, vv] = ((o_t * r).T * og_ref[:, vv]).astype(bf16)


def _attn_meta_kernel(q_ref, k_ref, v_ref, og_ref, o_ref):
    mask = (lax.broadcasted_iota(jnp.int32, (BLOCK, BLOCK), 0)
            >= lax.broadcasted_iota(jnp.int32, (BLOCK, BLOCK), 1)) & _meta_valid()
    s = jnp.where(mask, _dot_nt(q_ref[...], k_ref[...]), NEG_INF)
    p = jnp.exp2(s - _rowmax(s))
    o_ref[...] = _attn_finish(_dot(p.astype(bf16), v_ref[...]), _rowsum(p), og_ref[...])


def _attn_call(q, k, v, km, vm, out_g, l, cast_ws=()):
    B = q.shape[0]
    n = ATT_HPS
    nh = MLA_HEADS // n
    assert 2 * len(cast_ws) <= n * (SEQ // ATT_TQ)
    qk_spec = pl.BlockSpec((None, SEQ, n * QK_PAD), lambda b, h: (b, 0, h))
    v_spec = pl.BlockSpec((None, SEQ, n * V_HEAD), lambda b, h: (b, 0, h))
    cast_in, cast_out, cast_shapes = _cast_specs(cast_ws, l, B * nh, lambda b, h: b * nh + h)
    return pl.pallas_call(
        functools.partial(_attn_kernel, n_cast=len(cast_ws)), grid=(B, nh),
        in_specs=[qk_spec, qk_spec, v_spec,
                  pl.BlockSpec((None, BLOCK, n * QK_PAD), lambda b, h: (0, 0, h)),
                  pl.BlockSpec((None, BLOCK, n * V_HEAD), lambda b, h: (0, 0, h)),
                  pl.BlockSpec((None, 1, n * V_HEAD), lambda b, h: (l, 0, G_OUT // (n * V_HEAD) + h))]
        + cast_in,
        out_specs=[v_spec] + cast_out,
        out_shape=[jax.ShapeDtypeStruct((B, SEQ, MLA_WIDTH), bf16)] + cast_shapes,
        scratch_shapes=[pltpu.VMEM((n * V_HEAD, BLOCK + SEQ), bf16)],
        name="attn", compiler_params=_params(2),
    )(q, k, v, km, vm, out_g, *cast_ws)


def _attn_meta_call(qm, km, vm, out_g, l):
    qk_spec = pl.BlockSpec((None, BLOCK, QK_PAD), lambda h: (0, 0, h))
    v_spec = pl.BlockSpec((None, BLOCK, V_HEAD), lambda h: (0, 0, h))
    return pl.pallas_call(
        _attn_meta_kernel, grid=(MLA_HEADS,),
        in_specs=[qk_spec, qk_spec, v_spec,
                  pl.BlockSpec((None, 1, V_HEAD), lambda h: (l, 0, G_OUT // V_HEAD + h))],
        out_specs=v_spec,
        out_shape=jax.ShapeDtypeStruct((1, BLOCK, MLA_WIDTH), bf16),
        name="attn_meta", compiler_params=_params(1),
    )(qm, km, vm, out_g)


def _ret_tables(lg_ref, C):
    lg = lg_ref[0:1, :]
    lgc = jnp.concatenate([lg] * (C // LANES), axis=1)
    ri = lax.broadcasted_iota(jnp.int32, (C, C), 0)
    ci = lax.broadcasted_iota(jnp.int32, (C, C), 1)
    diff = (ri - ci).astype(f32)
    decay = jnp.where(diff >= 0, jnp.exp(jnp.maximum(diff, 0.0) * lgc), 0.0)
    idx = lax.broadcasted_iota(jnp.int32, (C, RET_HEAD), 0).astype(f32)
    xi = jnp.exp((idx + 1.0) * lg)
    return lg, decay, idx, xi


def _ret_key_state(k, v, idx, lg):
    n = k.shape[0]
    zeta = jnp.exp((n - 1.0 - idx[:n]) * lg)
    vz = (v.astype(f32) * zeta).astype(bf16)
    return _dot_tn(vz, k)


def _ret_mix(q, e, k, v, decay, state_t, xi):
    lane = lax.broadcasted_iota(jnp.int32, (1, RET_PAIR), 1) % LANES
    q = jnp.where((lane < HALF) if e == 0 else (lane >= HALF), q, jnp.zeros_like(q))
    n = q.shape[0]
    o = _dot((_dot_nt(q, k) * decay[:n, :n]).astype(bf16), v)
    if state_t is not None:
        o = o + _dot_nt(q, state_t.astype(bf16)) * xi[:n]
    return o


def _ret_emit(o, g, ng, nb):
    mu = jnp.mean(o, axis=-1, keepdims=True)
    d = o - mu
    var = jnp.mean(d * d, axis=-1, keepdims=True)
    on = d * lax.rsqrt(var + EPS) * ng + nb
    g = g.astype(f32)
    return (g * _sigmoid(g) * on).astype(bf16)


def _ret_kernel(q_ref, k_ref, v_ref, g_ref, km_ref, vm_ref, lg_ref, ng_ref, nb_ref, o_ref):
    C = RET_CHUNK
    n_chunks = SEQ // C
    chunks = [slice(c * C, (c + 1) * C) for c in range(n_chunks)]
    heads = []
    for h in range(2 * RET_PPS):
        lg, decay, idx, xi = _ret_tables(lg_ref.at[h], C)
        hv = slice(h * RET_HEAD, (h + 1) * RET_HEAD)
        pr = slice((h // 2) * RET_PAIR, (h // 2 + 1) * RET_PAIR)
        heads.append(dict(
            lg=lg, decay=decay, idx=idx, xi=xi, hv=hv, pr=pr, e=h % 2,
            chunk_decay=jnp.exp(float(C) * jnp.concatenate([lg, lg], axis=1)),
            state_t=_ret_key_state(km_ref[:, pr], vm_ref[:, hv], idx, lg), o_prev=None))
    for c, rows in enumerate(chunks):
        for hd in heads:
            hv, pr = hd["hv"], hd["pr"]
            o = _ret_mix(q_ref[rows, pr], hd["e"], k_ref[rows, pr], v_ref[rows, hv], hd["decay"],
                         hd["state_t"], hd["xi"])
            if c + 1 < n_chunks:
                hd["state_t"] = hd["state_t"] * hd["chunk_decay"] + _ret_key_state(
                    k_ref[rows, pr], v_ref[rows, hv], hd["idx"], hd["lg"])
            if hd["o_prev"] is not None:
                o_ref[chunks[c - 1], hv] = _ret_emit(hd["o_prev"], g_ref[chunks[c - 1], hv],
                                                     ng_ref[:, hv], nb_ref[:, hv])
            hd["o_prev"] = o
    for hd in heads:
        hv = hd["hv"]
        o_ref[chunks[-1], hv] = _ret_emit(hd["o_prev"], g_ref[chunks[-1], hv], ng_ref[:, hv],
                                          nb_ref[:, hv])


def _ret_meta_kernel(q_ref, k_ref, v_ref, g_ref, lg_ref, ng_ref, nb_ref, o_ref):
    for e in range(2):
        hv = slice(e * RET_HEAD, (e + 1) * RET_HEAD)
        _, decay, _, _ = _ret_tables(lg_ref.at[e], BLOCK)
        o = _ret_mix(q_ref[...], e, k_ref[...], v_ref[:, hv], decay, None, None)
        o_ref[:, hv] = _ret_emit(o, g_ref[:, hv], ng_ref[:, hv], nb_ref[:, hv])


def _ret_call(rq, rk, rv, rg, rkm, rvm, log_g, norm_g, norm_b, l):
    B = rq.shape[0]
    n = RET_PPS
    w = 2 * n * RET_HEAD
    pair = pl.BlockSpec((None, SEQ, n * RET_PAIR), lambda b, p: (b, 0, p))
    spec = pl.BlockSpec((None, SEQ, w), lambda b, p: (b, 0, p))
    vec_g = pl.BlockSpec((None, 1, w), lambda b, p: (l, 0, G_RETG // w + p))
    vec_b = pl.BlockSpec((None, 1, w), lambda b, p: (l, 0, G_RETB // w + p))
    return pl.pallas_call(
        _ret_kernel, grid=(B, RET_HEADS // (2 * n)),
        in_specs=[pair, pair, spec, spec,
                  pl.BlockSpec((None, BLOCK, n * RET_PAIR), lambda b, p: (0, 0, p)),
                  pl.BlockSpec((None, BLOCK, w), lambda b, p: (0, 0, p)),
                  pl.BlockSpec((2 * n, 8, LANES), lambda b, p: (p, 0, 0)), vec_g, vec_b],
        out_specs=spec,
        out_shape=jax.ShapeDtypeStruct((B, SEQ, RET_WIDTH), bf16),
        name="ret", compiler_params=_params(2),
    )(rq, rk, rv, rg, rkm, rvm, log_g, norm_g, norm_b)


def _ret_meta_call(rqm, rkm, rvm, rgm, log_g, norm_g, norm_b, l):
    spec = pl.BlockSpec((None, BLOCK, 2 * RET_HEAD), lambda p: (0, 0, p))
    w = 2 * RET_HEAD
    vec_g = pl.BlockSpec((None, 1, w), lambda p: (l, 0, G_RETG // w + p))
    vec_b = pl.BlockSpec((None, 1, w), lambda p: (l, 0, G_RETB // w + p))
    pair = pl.BlockSpec((None, BLOCK, RET_PAIR), lambda p: (0, 0, p))
    return pl.pallas_call(
        _ret_meta_kernel, grid=(RET_HEADS // 2,),
        in_specs=[pair, pair, spec, spec, pl.BlockSpec((2, 8, LANES), lambda p: (p, 0, 0)),
                  vec_g, vec_b],
        out_specs=spec,
        out_shape=jax.ShapeDtypeStruct((1, BLOCK, RET_WIDTH), bf16),
        name="ret_meta", compiler_params=_params(1),
    )(rqm, rkm, rvm, rgm, log_g, norm_g, norm_b)


def _post_kernel(h_ref, ym_ref, yr_ref, wo_ref, fg_ref, wgu_ref, wd_ref, o_ref, act_ref):
    h1 = (h_ref[...] + _dot(ym_ref[...], wo_ref[0:MLA_WIDTH, :])
          + _dot(yr_ref[...], wo_ref[MLA_WIDTH:MLA_WIDTH + RET_WIDTH, :]))
    hf = (h1 * _rms(h1, 1.0 / D_MODEL) * fg_ref[...]).astype(bf16)
    for c in range(D_FF // FF_TILE):
        lo = c * FF_TILE
        gate = _dot(hf, wgu_ref[:, lo:lo + FF_TILE])
        up = _dot(hf, wgu_ref[:, D_FF + lo:D_FF + lo + FF_TILE])
        act_ref[:, lo:lo + FF_TILE] = (gate * _sigmoid(gate) * up).astype(bf16)
    o_ref[...] = h1 + _dot(act_ref[...], wd_ref[...])


def _post_call(h, ym, yr, p, l, w_bf, tm, name):
    B, R, _ = h.shape
    row = lambda w: pl.BlockSpec((None, tm, w), lambda b, j: (b, j, 0))
    whole = lambda w: pl.BlockSpec(w.shape, lambda *_: (0, 0), pipeline_mode=pl.Buffered(1))
    w_out, w_gu, w_down = w_bf
    assert w_out.shape == (MLA_WIDTH + RET_WIDTH, D_MODEL) and w_gu.shape == (D_MODEL, 2 * D_FF)
    assert w_down.shape == (D_FF, D_MODEL)
    return pl.pallas_call(
        _post_kernel, grid=(B, R // tm),
        in_specs=[row(D_MODEL), row(MLA_WIDTH), row(RET_WIDTH), whole(w_out),
                  _gain_spec(l, G_FFN, D_MODEL), whole(w_gu), whole(w_down)],
        out_specs=row(D_MODEL),
        out_shape=jax.ShapeDtypeStruct((B, R, D_MODEL), f32),
        scratch_shapes=[pltpu.VMEM((tm, D_FF), bf16)],
        name=name, compiler_params=_params(2),
    )(h, ym, yr, w_out, p["gains"], w_gu, w_down)


W_IN_COLS = 2 * Q_LORA + QK_ROPE + 4 * RET_WIDTH
O_KPE = Q_LORA + KV_LORA
O_RQ = O_KPE + QK_ROPE
O_RK = O_RQ + RET_WIDTH
O_RV = O_RK + RET_WIDTH
WPREP_COLS = 512


def _wprep_kernel(w_ref, head_ref, qk_ref, vg_ref):
    def put(dst_ref, dst, src, n):
        dst_ref[dst:dst + n, :] = w_ref[src:src + n, :].astype(bf16)

    q = QK_ROPE // 2
    head_ref[...] = jnp.zeros(head_ref.shape, bf16)
    put(head_ref, 0, 0, O_KPE)
    put(head_ref, C_KPE + HALF, O_KPE, q)
    put(head_ref, C_KPE + LANES + HALF, O_KPE + q, q)
    for s, src0 in enumerate((O_RQ, O_RK)):
        for p in range(RET_HEADS // 2):
            src, dst = src0 + p * RET_PAIR, s * RET_WIDTH + p * RET_PAIR
            put(qk_ref, dst, src, HALF)
            put(qk_ref, dst + HALF, src + 2 * HALF, HALF)
            put(qk_ref, dst + 2 * HALF, src + HALF, HALF)
            put(qk_ref, dst + 3 * HALF, src + 3 * HALF, HALF)
    put(vg_ref, 0, O_RV, 2 * RET_WIDTH)


def _wprep_call(w_in):
    depth = w_in.shape[0]
    wt = jnp.swapaxes(w_in, 1, 2)
    cb = WPREP_COLS
    out = lambda n: pl.BlockSpec((None, n, cb), lambda l, c: (l, 0, c))
    rows = (C_RQ, C_RV - C_RQ, N_IN_PAD - C_RV)
    return pl.pallas_call(
        _wprep_kernel, grid=(depth, D_MODEL // cb),
        in_specs=[out(W_IN_COLS)], out_specs=[out(n) for n in rows],
        out_shape=[jax.ShapeDtypeStruct((depth, n, D_MODEL), bf16) for n in rows],
        name="wprep", compiler_params=_params(2),
    )(wt)


def _rope_tables(pos, valid):
    pos = pos.astype(np.float32)
    n = pos.shape[0]

    def cs(dim):
        inv = np.float32(ROPE_BASE) ** (-np.arange(0, dim, 2, dtype=np.float32) / np.float32(dim))
        ang = pos[:, None] * inv[None, :].astype(np.float32)
        return np.cos(ang).astype(np.float32), np.sin(ang).astype(np.float32)

    cm, sm = cs(QK_ROPE)
    z32 = np.zeros_like(cm)
    ca = np.concatenate([np.ones((n, HALF), np.float32), cm, z32], axis=1)
    sa = np.concatenate([np.zeros((n, HALF), np.float32), sm, z32], axis=1)
    cr, sr = cs(RET_HEAD)
    c2 = np.concatenate([cr, cr], axis=1)
    s2 = np.concatenate([sr, sr], axis=1)
    kscale = valid.astype(np.float32)[:, None] * np.float32(RET_HEAD ** -0.5)
    return tuple(jnp.asarray(t) for t in (ca, sa, c2, s2, c2 * kscale, s2 * kscale))


def _mla_head_layout(a):
    q = QK_ROPE // 2
    z = jnp.zeros(a.shape[:-1] + (q,), a.dtype)
    return jnp.concatenate([a[..., :HALF], a[..., QK_NOPE:QK_NOPE + q], z,
                            a[..., HALF:QK_NOPE], a[..., QK_NOPE + q:], z], axis=-1)


def _prep_params(attn_norm_g, w_in, q_a_norm_g, w_q_b, kv_a_norm_g, w_kv_b, q_norm_g, k_norm_g,
                 mla_out_norm_g, ret_norm_g, ret_norm_b, ffn_norm_g):
    depth = w_in.shape[0]
    w_head, w_qk, w_vg = _wprep_call(w_in)
    wq = _mla_head_layout(w_q_b.astype(bf16).reshape(depth, Q_LORA, MLA_HEADS, QK_HEAD))
    wkv = w_kv_b.astype(bf16).reshape(depth, KV_LORA, MLA_HEADS, QK_NOPE + V_HEAD)
    z64 = jnp.zeros((depth, KV_LORA, MLA_HEADS, HALF), bf16)
    wk = jnp.concatenate([wkv[..., :HALF], z64, wkv[..., HALF:QK_NOPE], z64], axis=-1)
    wkv = jnp.concatenate([wk.reshape(depth, KV_LORA, MLA_HEADS * QK_PAD),
                           wkv[..., QK_NOPE:].reshape(depth, KV_LORA, MLA_WIDTH)], axis=-1)
    scale = QK_HEAD ** -0.5 * LOG2_E
    gains = jnp.concatenate(
        [attn_norm_g, ffn_norm_g, mla_out_norm_g, ret_norm_g, ret_norm_b, q_a_norm_g, kv_a_norm_g,
         _mla_head_layout(q_norm_g * scale), _mla_head_layout(k_norm_g)], axis=-1)[:, None, :]
    assert gains.shape[-1] == G_TOTAL
    return {
        "gains": gains,
        "w_head": w_head,
        "w_qk": w_qk,
        "w_vg": w_vg,
        "w_qb": wq.reshape(depth, Q_LORA, MLA_HEADS * QK_PAD),
        "w_kvb": wkv,
    }


def kernel(x, meta_tokens, attn_norm_g, w_in, q_a_norm_g, w_q_b, kv_a_norm_g, w_kv_b, q_norm_g,
           k_norm_g, mla_out_norm_g, ret_norm_g, ret_norm_b, w_out, ffn_norm_g, w_gate_up, w_down):
    depth = w_in.shape[0]
    hx = x
    hm = jnp.concatenate([jnp.zeros((PAD_ROWS, D_MODEL), x.dtype), meta_tokens.astype(x.dtype)])[None]
    r = np.arange(BLOCK)
    tabs_x = _rope_tables(np.arange(SEQ) + N_META, np.ones(SEQ))
    tabs_m = _rope_tables(np.maximum(r - PAD_ROWS, 0), r >= PAD_ROWS)
    gamma = np.float32(1.0) - np.float32(2.0) ** (np.float32(-5.0) - np.arange(RET_HEADS, dtype=np.float32))
    log_g = jnp.asarray(np.broadcast_to(np.log(gamma)[:, None, None], (RET_HEADS, 8, LANES)))
    p = _prep_params(attn_norm_g, w_in, q_a_norm_g, w_q_b, kv_a_norm_g, w_kv_b, q_norm_g, k_norm_g,
                     mla_out_norm_g, ret_norm_g, ret_norm_b, ffn_norm_g)
    for l in range(depth):
        q, k, v, rq, rk, rv, rg = _pre_call(hx, p, l, tabs_x, PRE_TM, "pre")
        qm, km, vm, rqm, rkm, rvm, rgm = _pre_call(hm, p, l, tabs_m, BLOCK, "pre_meta")
        y_mla, *w_bf = _attn_call(q, k, v, km, vm, p["gains"], l, (w_out, w_gate_up, w_down))
        y_ret = _ret_call(rq, rk, rv, rg, rkm, rvm, log_g, p["gains"], p["gains"], l)
        hx = _post_call(hx, y_mla, y_ret, p, l, w_bf, ROW_TM, "post")
        if l + 1 < depth:
            ym_mla = _attn_meta_call(qm, km, vm, p["gains"], l)
            ym_ret = _ret_meta_call(rqm, rkm, rvm, rgm, log_g, p["gains"], p["gains"], l)
            hm = _post_call(hm, ym_mla, ym_ret, p, l, w_bf, BLOCK, "post_meta")
    return hx
```

```python
import functools
import itertools

import numpy as np

import jax
import jax.numpy as jnp
from jax import lax
from jax.experimental import pallas as pl
from jax.experimental.pallas import tpu as pltpu

D_MODEL = 1024
SEQ = 2048
N_META = 16
BLOCK = 128
MLA_HEADS = 4
Q_LORA = 256
KV_LORA = 256
QK_NOPE = 128
QK_ROPE = 64
QK_HEAD = QK_NOPE + QK_ROPE
V_HEAD = 128
MLA_WIDTH = MLA_HEADS * V_HEAD
RET_HEADS = 4
RET_HEAD = 128
RET_WIDTH = RET_HEADS * RET_HEAD
D_FF = 2816
ROPE_BASE = 10000.0
EPS = 1e-6
NEG_INF = -1e30

LANES = 128
HALF = LANES // 2
PAD_ROWS = BLOCK - N_META
QK_PAD = 2 * LANES
KV_WIDTH = MLA_HEADS * QK_PAD + MLA_WIDTH
RET_PAIR = 2 * RET_HEAD

C_CQ = 0
C_CKV = C_CQ + Q_LORA
C_KPE = C_CKV + KV_LORA
C_RQ = C_KPE + 2 * LANES
C_RK = C_RQ + RET_WIDTH
C_RV = C_RK + RET_WIDTH
C_RG = C_RV + RET_WIDTH
N_IN_PAD = C_RG + RET_WIDTH

ROW_TM = 1024
PRE_TM = 1024
ATT_TQ = 512
ATT_ORDER = (0, 1, 2, 3)
ATT_HPS = 2
ATT_AHEAD = 2
RET_CHUNK = 256
FF_TILE = 256
assert D_FF % FF_TILE == 0
LOG2_E = 1.4426950408889634
VMEM_LIMIT = 56 * 1024 * 1024

f32 = jnp.float32
bf16 = jnp.bfloat16


def _dot(a, b):
    return jnp.dot(a, b, preferred_element_type=f32)


def _dot_nt(a, b):
    return lax.dot_general(a, b, (((1,), (1,)), ((), ())), preferred_element_type=f32)


def _dot_tn(a, b):
    return lax.dot_general(a, b, (((0,), (0,)), ((), ())), preferred_element_type=f32)


def _rms(x, inv_n):
    return lax.rsqrt(jnp.sum(x * x, axis=-1, keepdims=True) * inv_n + EPS)


def _sigmoid(x):
    return 1.0 / (1.0 + jnp.exp(-x))


def _rot(a, b, c, s):
    return a * c - b * s, b * c + a * s


def _params(n_grid_axes):
    return pltpu.CompilerParams(dimension_semantics=("parallel",) * n_grid_axes,
                                vmem_limit_bytes=VMEM_LIMIT)


G_ATTN = 0
G_FFN = G_ATTN + D_MODEL
G_OUT = G_FFN + D_MODEL
G_RETG = G_OUT + MLA_WIDTH
G_RETB = G_RETG + RET_WIDTH
G_QA = G_RETB + RET_WIDTH
G_KVA = G_QA + Q_LORA
G_Q = G_KVA + KV_LORA
G_K = G_Q + QK_PAD
G_TOTAL = G_K + QK_PAD


def _gain_spec(l, off, width):
    assert off % width == 0
    return pl.BlockSpec((None, 1, width), lambda *_: (l, 0, off // width),
                        pipeline_mode=pl.Buffered(1))


def _layer_spec(l, shape):
    nd = len(shape)
    return pl.BlockSpec((None,) + shape, lambda *_: (l,) + (0,) * nd, pipeline_mode=pl.Buffered(1))


def _cast_specs(cast_ws, l, steps, step_of):
    slabs = []
    for w in cast_ws:
        assert w.shape[1] % (steps * 16) == 0, w.shape
        slabs.append((w.shape[1] // steps, w.shape[2]))
    return ([pl.BlockSpec((None,) + s, lambda *g: (l, step_of(*g), 0)) for s in slabs],
            [pl.BlockSpec(s, lambda *g: (step_of(*g), 0)) for s in slabs],
            [jax.ShapeDtypeStruct(w.shape[1:], bf16) for w in cast_ws])


def _pre_kernel(h_ref, g_ref, whead_ref, wqk_ref, wvg_ref, qag_ref, wqb_ref, kvag_ref, wkvb_ref,
                qg_ref, kg_ref, ca_ref, sa_ref, c2_ref, s2_ref, ck_ref, sk_ref,
                q_out, k_out, v_out, rq_out, rk_out, rv_out, rg_out, hb_ref, cqn_ref, ckvn_ref,
                kpe_ref):
    qga, qgb = qg_ref[:, :LANES], qg_ref[:, LANES:]
    kga, kgb = kg_ref[:, :LANES], kg_ref[:, LANES:]
    G = 2 * LANES

    x = h_ref[...]
    hb_ref[...] = (x * _rms(x, 1.0 / D_MODEL) * g_ref[...]).astype(bf16)

    def proj(w_ref, lo):
        return _dot_nt(hb_ref[...], w_ref[lo:lo + G, :])

    cq = proj(whead_ref, C_CQ)
    cqn_ref[...] = (cq * _rms(cq, 1.0 / Q_LORA) * qag_ref[...]).astype(bf16)
    ckv = proj(whead_ref, C_CKV)
    ckvn_ref[...] = (ckv * _rms(ckv, 1.0 / KV_LORA) * kvag_ref[...]).astype(bf16)
    kpe_ref[...] = proj(whead_ref, C_KPE)

    def norm_rope(ab, ga, gb):
        a, b = ab[:, :LANES], ab[:, LANES:]
        r = lax.rsqrt(jnp.sum(a * a + b * b, axis=-1, keepdims=True) * (1.0 / QK_HEAD) + EPS)
        ao, bo = _rot(a * ga, b * gb, ca_ref[...], sa_ref[...])
        return (ao * r).astype(bf16), (bo * r).astype(bf16)

    def mla_pair(hp):
        lo2 = hp * 2 * QK_PAD
        q2h = _dot(cqn_ref[...], wqb_ref[:, lo2:lo2 + 2 * QK_PAD])
        k2h = _dot(ckvn_ref[...], wkvb_ref[:, lo2:lo2 + 2 * QK_PAD])
        for e in range(2):
            lo = lo2 + e * QK_PAD
            q_out[:, lo:lo + LANES], q_out[:, lo + LANES:lo + QK_PAD] = norm_rope(
                q2h[:, e * QK_PAD:(e + 1) * QK_PAD], qga, qgb)
            k_out[:, lo:lo + LANES], k_out[:, lo + LANES:lo + QK_PAD] = norm_rope(
                k2h[:, e * QK_PAD:(e + 1) * QK_PAD] + kpe_ref[...], kga, kgb)

    def ret_pairs(zq, zk):
        for p in range(RET_HEADS // 2):
            lo = p * RET_PAIR
            k1, k2 = _rot(zk[:, lo:lo + LANES], zk[:, lo + LANES:lo + RET_PAIR], ck_ref[...],
                          sk_ref[...])
            rk_out[:, lo:lo + LANES] = k1.astype(bf16)
            rk_out[:, lo + LANES:lo + RET_PAIR] = k2.astype(bf16)
            q1, q2 = _rot(zq[:, lo:lo + LANES], zq[:, lo + LANES:lo + RET_PAIR], c2_ref[...],
                          s2_ref[...])
            rq_out[:, lo:lo + LANES] = q1.astype(bf16)
            rq_out[:, lo + LANES:lo + RET_PAIR] = q2.astype(bf16)

    rv_out[...] = _dot_nt(hb_ref[...], wvg_ref[:RET_WIDTH, :]).astype(bf16)
    mla_pair(0)
    rg_out[...] = _dot_nt(hb_ref[...], wvg_ref[RET_WIDTH:, :]).astype(bf16)
    mla_pair(1)
    zk = _dot_nt(hb_ref[...], wqk_ref[RET_WIDTH:, :])
    zq = _dot_nt(hb_ref[...], wqk_ref[:RET_WIDTH, :])
    v_out[...] = _dot(ckvn_ref[...], wkvb_ref[:, MLA_HEADS * QK_PAD:]).astype(bf16)
    ret_pairs(zq, zk)


def _pre_call(h, p, l, tabs, tm, name):
    B, R, _ = h.shape
    row = lambda w: pl.BlockSpec((None, tm, w), lambda b, j: (b, j, 0))
    tab = pl.BlockSpec((tm, LANES), lambda b, j: (j, 0))
    in_specs = [
        row(D_MODEL),
        _gain_spec(l, G_ATTN, D_MODEL),
        _layer_spec(l, (C_RQ, D_MODEL)),
        _layer_spec(l, (C_RV - C_RQ, D_MODEL)),
        _layer_spec(l, (N_IN_PAD - C_RV, D_MODEL)),
        _gain_spec(l, G_QA, Q_LORA),
        _layer_spec(l, (Q_LORA, MLA_HEADS * QK_PAD)),
        _gain_spec(l, G_KVA, KV_LORA),
        _layer_spec(l, (KV_LORA, KV_WIDTH)),
        _gain_spec(l, G_Q, QK_PAD),
        _gain_spec(l, G_K, QK_PAD),
    ] + [tab] * 6
    widths = (MLA_HEADS * QK_PAD, MLA_HEADS * QK_PAD, MLA_WIDTH, RET_WIDTH, RET_WIDTH, RET_WIDTH,
              RET_WIDTH)
    return pl.pallas_call(
        _pre_kernel, grid=(B, R // tm),
        in_specs=in_specs, out_specs=[row(w) for w in widths],
        out_shape=[jax.ShapeDtypeStruct((B, R, w), bf16) for w in widths],
        scratch_shapes=[pltpu.VMEM((tm, D_MODEL), bf16), pltpu.VMEM((tm, Q_LORA), bf16),
                        pltpu.VMEM((tm, KV_LORA), bf16), pltpu.VMEM((tm, 2 * LANES), f32)],
        name=name, compiler_params=_params(2),
    )(h, p["gains"], p["w_head"], p["w_qk"], p["w_vg"], p["gains"], p["w_qb"], p["gains"], p["w_kvb"],
      p["gains"], p["gains"], *tabs)


def _rowmax(s):
    return jnp.max(s, axis=-1, keepdims=True)


def _rowsum(p):
    return jnp.sum(p, axis=-1, keepdims=True)


def _attn_finish(acc, l, og):
    o = acc * (1.0 / l)
    return (o * _rms(o, 1.0 / V_HEAD) * og).astype(bf16)


def _meta_valid():
    return lax.broadcasted_iota(jnp.int32, (1, BLOCK), 1) >= PAD_ROWS


def _attn_steps(q_ref, k_ref, v_ref, km_ref, vm_ref, og_ref, *rest, n_cast):
    cast_in, o_ref, cast_out, vt_ref = rest[:n_cast], rest[n_cast], rest[n_cast + 1:-1], rest[-1]
    for hh in range(ATT_HPS):
        vv = slice(hh * V_HEAD, (hh + 1) * V_HEAD)
        vt_ref[vv, 0:BLOCK] = vm_ref[:, vv].T
        vt_ref[vv, BLOCK:] = v_ref[:, vv].T
    key_valid = lax.broadcasted_iota(jnp.int32, (BLOCK, 1), 0) >= PAD_ROWS
    H = ATT_TQ // 2
    tri_main = (lax.broadcasted_iota(jnp.int32, (H, ATT_TQ), 0)
                <= lax.broadcasted_iota(jnp.int32, (H, ATT_TQ), 1))
    tri_tail = (lax.broadcasted_iota(jnp.int32, (H, H), 0)
                <= lax.broadcasted_iota(jnp.int32, (H, H), 1))

    def scores(item):
        hh, i = item
        qk = slice(hh * QK_PAD, (hh + 1) * QK_PAD)
        lo, hi = i * ATT_TQ, (i + 1) * ATT_TQ
        q = q_ref[lo:hi, qk]
        s_m = jnp.where(key_valid, _dot_nt(km_ref[:, qk], q), NEG_INF)
        s_x = _dot_nt(k_ref[0:lo + H, qk], q)
        parts = [s_m] + ([s_x[:lo]] if lo else []) + [jnp.where(tri_main, s_x[lo:], NEG_INF)]
        tail = jnp.where(tri_tail, _dot_nt(k_ref[lo + H:hi, qk], q[H:]), NEG_INF)
        return jnp.concatenate(parts, axis=0), tail

    items = [(hh, i) for i in ATT_ORDER for hh in range(ATT_HPS)]
    ahead = [scores(it) for it in items[:ATT_AHEAD]]
    for n, (hh, i) in enumerate(items):
        if n + ATT_AHEAD < len(items):
            ahead.append(scores(items[n + ATT_AHEAD]))
        if n % 2 and n // 2 < n_cast:
            cast_out[n // 2][...] = cast_in[n // 2][...].astype(bf16)
        s, s_tail = ahead.pop(0)
        vv = slice(hh * V_HEAD, (hh + 1) * V_HEAD)
        lo, hi = i * ATT_TQ, (i + 1) * ATT_TQ
        mid = BLOCK + lo + H
        m = jnp.max(s, axis=0, keepdims=True)
        m_hi = jnp.maximum(m[:, H:], jnp.max(s_tail, axis=0, keepdims=True))
        p = jnp.exp2(s - jnp.concatenate([m[:, :H], m_hi], axis=1))
        p_tail = jnp.exp2(s_tail - m_hi)
        l = jnp.sum(p, axis=0, keepdims=True)
        l = jnp.concatenate([l[:, :H], l[:, H:] + jnp.sum(p_tail, axis=0, keepdims=True)], axis=1)
        o_t = _dot(vt_ref[vv, 0:mid], p.astype(bf16))
        o_tail = _dot(vt_ref[vv, mid:BLOCK + hi], p_tail.astype(bf16))
        o_t = jnp.concatenate([o_t[:, :H], o_t[:, H:] + o_tail], axis=1) * (1.0 / l)
        r = lax.rsqrt(jnp.sum(o_t * o_t, axis=0, keepdims=True) * (1.0 / V_HEAD) + EPS)
        o_ref[lo:hi, vv] = ((o_t * r).T * og_ref[:, vv]).astype(bf16)
        yield


def _attn_meta_kernel(q_ref, k_ref, v_ref, og_ref, o_ref):
    mask = (lax.broadcasted_iota(jnp.int32, (BLOCK, BLOCK), 0)
            >= lax.broadcasted_iota(jnp.int32, (BLOCK, BLOCK), 1)) & _meta_valid()
    s = jnp.where(mask, _dot_nt(q_ref[...], k_ref[...]), NEG_INF)
    p = jnp.exp2(s - _rowmax(s))
    o_ref[...] = _attn_finish(_dot(p.astype(bf16), v_ref[...]), _rowsum(p), og_ref[...])


def _attn_ret_call(attn_in, ret_in, log_g, gains, l, cast_ws):
    B = attn_in[0].shape[0]
    n = ATT_HPS
    nh = MLA_HEADS // n
    assert nh == RET_HEADS // 2 and SEQ // RET_CHUNK <= n * (SEQ // ATT_TQ)
    assert 2 * len(cast_ws) <= n * (SEQ // ATT_TQ)
    qk_spec = pl.BlockSpec((None, SEQ, n * QK_PAD), lambda b, h: (b, 0, h))
    v_spec = pl.BlockSpec((None, SEQ, n * V_HEAD), lambda b, h: (b, 0, h))
    cast_in, cast_out, cast_shapes = _cast_specs(cast_ws, l, B * nh, lambda b, h: b * nh + h)
    w = 2 * RET_HEAD
    pair = pl.BlockSpec((None, SEQ, RET_PAIR), lambda b, p: (b, 0, p))
    spec = pl.BlockSpec((None, SEQ, w), lambda b, p: (b, 0, p))
    ret_specs = [pair, pair, spec, spec,
                 pl.BlockSpec((None, BLOCK, RET_PAIR), lambda b, p: (0, 0, p)),
                 pl.BlockSpec((None, BLOCK, w), lambda b, p: (0, 0, p)),
                 pl.BlockSpec((2, 8, LANES), lambda b, p: (p, 0, 0)),
                 pl.BlockSpec((None, 1, w), lambda b, p: (l, 0, G_RETG // w + p)),
                 pl.BlockSpec((None, 1, w), lambda b, p: (l, 0, G_RETB // w + p))]
    return pl.pallas_call(
        functools.partial(_attn_ret_kernel, n_cast=len(cast_ws)), grid=(B, nh),
        in_specs=[qk_spec, qk_spec, v_spec,
                  pl.BlockSpec((None, BLOCK, n * QK_PAD), lambda b, h: (0, 0, h)),
                  pl.BlockSpec((None, BLOCK, n * V_HEAD), lambda b, h: (0, 0, h)),
                  pl.BlockSpec((None, 1, n * V_HEAD), lambda b, h: (l, 0, G_OUT // (n * V_HEAD) + h))]
        + cast_in + ret_specs,
        out_specs=[v_spec] + cast_out + [spec],
        out_shape=[jax.ShapeDtypeStruct((B, SEQ, MLA_WIDTH), bf16)] + cast_shapes
        + [jax.ShapeDtypeStruct((B, SEQ, RET_WIDTH), bf16)],
        scratch_shapes=[pltpu.VMEM((n * V_HEAD, BLOCK + SEQ), bf16)],
        name="attn_ret", compiler_params=_params(2),
    )(*attn_in, gains, *cast_ws, *ret_in, log_g, gains, gains)


def _attn_meta_call(qm, km, vm, out_g, l):
    qk_spec = pl.BlockSpec((None, BLOCK, QK_PAD), lambda h: (0, 0, h))
    v_spec = pl.BlockSpec((None, BLOCK, V_HEAD), lambda h: (0, 0, h))
    return pl.pallas_call(
        _attn_meta_kernel, grid=(MLA_HEADS,),
        in_specs=[qk_spec, qk_spec, v_spec,
                  pl.BlockSpec((None, 1, V_HEAD), lambda h: (l, 0, G_OUT // V_HEAD + h))],
        out_specs=v_spec,
        out_shape=jax.ShapeDtypeStruct((1, BLOCK, MLA_WIDTH), bf16),
        name="attn_meta", compiler_params=_params(1),
    )(qm, km, vm, out_g)


def _ret_tables(lg_ref, C):
    lg = lg_ref[0:1, :]
    lgc = jnp.concatenate([lg] * (C // LANES), axis=1)
    ri = lax.broadcasted_iota(jnp.int32, (C, C), 0)
    ci = lax.broadcasted_iota(jnp.int32, (C, C), 1)
    diff = (ri - ci).astype(f32)
    decay = jnp.where(diff >= 0, jnp.exp(jnp.maximum(diff, 0.0) * lgc), 0.0)
    idx = lax.broadcasted_iota(jnp.int32, (C, RET_HEAD), 0).astype(f32)
    xi = jnp.exp((idx + 1.0) * lg)
    return lg, decay, idx, xi


def _ret_key_state(k, v, idx, lg):
    n = k.shape[0]
    zeta = jnp.exp((n - 1.0 - idx[:n]) * lg)
    vz = (v.astype(f32) * zeta).astype(bf16)
    return _dot_tn(vz, k)


def _ret_mix(q, e, k, v, decay, state_t, xi):
    lane = lax.broadcasted_iota(jnp.int32, (1, RET_PAIR), 1) % LANES
    q = jnp.where((lane < HALF) if e == 0 else (lane >= HALF), q, jnp.zeros_like(q))
    n = q.shape[0]
    o = _dot((_dot_nt(q, k) * decay[:n, :n]).astype(bf16), v)
    if state_t is not None:
        o = o + _dot_nt(q, state_t.astype(bf16)) * xi[:n]
    return o


def _ret_emit(o, g, ng, nb):
    mu = jnp.mean(o, axis=-1, keepdims=True)
    d = o - mu
    var = jnp.mean(d * d, axis=-1, keepdims=True)
    on = d * lax.rsqrt(var + EPS) * ng + nb
    g = g.astype(f32)
    return (g * _sigmoid(g) * on).astype(bf16)


def _ret_steps(q_ref, k_ref, v_ref, g_ref, km_ref, vm_ref, lg_ref, ng_ref, nb_ref, o_ref, pps):
    C = RET_CHUNK
    n_chunks = SEQ // C
    chunks = [slice(c * C, (c + 1) * C) for c in range(n_chunks)]
    heads = []
    for h in range(2 * pps):
        lg, decay, idx, xi = _ret_tables(lg_ref.at[h], C)
        hv = slice(h * RET_HEAD, (h + 1) * RET_HEAD)
        pr = slice((h // 2) * RET_PAIR, (h // 2 + 1) * RET_PAIR)
        heads.append(dict(
            lg=lg, decay=decay, idx=idx, xi=xi, hv=hv, pr=pr, e=h % 2,
            chunk_decay=jnp.exp(float(C) * jnp.concatenate([lg, lg], axis=1)),
            state_t=_ret_key_state(km_ref[:, pr], vm_ref[:, hv], idx, lg), o_prev=None))
    for c, rows in enumerate(chunks):
        for hd in heads:
            hv, pr = hd["hv"], hd["pr"]
            o = _ret_mix(q_ref[rows, pr], hd["e"], k_ref[rows, pr], v_ref[rows, hv], hd["decay"],
                         hd["state_t"], hd["xi"])
            if c + 1 < n_chunks:
                hd["state_t"] = hd["state_t"] * hd["chunk_decay"] + _ret_key_state(
                    k_ref[rows, pr], v_ref[rows, hv], hd["idx"], hd["lg"])
            if hd["o_prev"] is not None:
                o_ref[chunks[c - 1], hv] = _ret_emit(hd["o_prev"], g_ref[chunks[c - 1], hv],
                                                     ng_ref[:, hv], nb_ref[:, hv])
            hd["o_prev"] = o
        yield
    for hd in heads:
        hv = hd["hv"]
        o_ref[chunks[-1], hv] = _ret_emit(hd["o_prev"], g_ref[chunks[-1], hv], ng_ref[:, hv],
                                          nb_ref[:, hv])


def _attn_ret_kernel(*refs, n_cast):
    n_a = 6 + n_cast
    a_in, r_in, outs = refs[:n_a], refs[n_a:n_a + 9], refs[n_a + 9:]
    attn = _attn_steps(*a_in, *outs[:1 + n_cast], outs[-1], n_cast=n_cast)
    ret = _ret_steps(*r_in, outs[1 + n_cast], pps=1)
    for _ in itertools.zip_longest(attn, ret):
        pass


def _ret_meta_kernel(q_ref, k_ref, v_ref, g_ref, lg_ref, ng_ref, nb_ref, o_ref):
    for e in range(2):
        hv = slice(e * RET_HEAD, (e + 1) * RET_HEAD)
        _, decay, _, _ = _ret_tables(lg_ref.at[e], BLOCK)
        o = _ret_mix(q_ref[...], e, k_ref[...], v_ref[:, hv], decay, None, None)
        o_ref[:, hv] = _ret_emit(o, g_ref[:, hv], ng_ref[:, hv], nb_ref[:, hv])


def _ret_meta_call(rqm, rkm, rvm, rgm, log_g, norm_g, norm_b, l):
    spec = pl.BlockSpec((None, BLOCK, 2 * RET_HEAD), lambda p: (0, 0, p))
    w = 2 * RET_HEAD
    vec_g = pl.BlockSpec((None, 1, w), lambda p: (l, 0, G_RETG // w + p))
    vec_b = pl.BlockSpec((None, 1, w), lambda p: (l, 0, G_RETB // w + p))
    pair = pl.BlockSpec((None, BLOCK, RET_PAIR), lambda p: (0, 0, p))
    return pl.pallas_call(
        _ret_meta_kernel, grid=(RET_HEADS // 2,),
        in_specs=[pair, pair, spec, spec, pl.BlockSpec((2, 8, LANES), lambda p: (p, 0, 0)),
                  vec_g, vec_b],
        out_specs=spec,
        out_shape=jax.ShapeDtypeStruct((1, BLOCK, RET_WIDTH), bf16),
        name="ret_meta", compiler_params=_params(1),
    )(rqm, rkm, rvm, rgm, log_g, norm_g, norm_b)


def _post_kernel(h_ref, ym_ref, yr_ref, wo_ref, fg_ref, wgu_ref, wd_ref, o_ref, act_ref):
    h1 = (h_ref[...] + _dot(ym_ref[...], wo_ref[0:MLA_WIDTH, :])
          + _dot(yr_ref[...], wo_ref[MLA_WIDTH:MLA_WIDTH + RET_WIDTH, :]))
    hf = (h1 * _rms(h1, 1.0 / D_MODEL) * fg_ref[...]).astype(bf16)
    for c in range(D_FF // FF_TILE):
        lo = c * FF_TILE
        gate = _dot(hf, wgu_ref[:, lo:lo + FF_TILE])
        up = _dot(hf, wgu_ref[:, D_FF + lo:D_FF + lo + FF_TILE])
        act_ref[:, lo:lo + FF_TILE] = (gate * _sigmoid(gate) * up).astype(bf16)
    o_ref[...] = h1 + _dot(act_ref[...], wd_ref[...])


def _post_call(h, ym, yr, p, l, w_bf, tm, name):
    B, R, _ = h.shape
    row = lambda w: pl.BlockSpec((None, tm, w), lambda b, j: (b, j, 0))
    whole = lambda w: pl.BlockSpec(w.shape, lambda *_: (0, 0), pipeline_mode=pl.Buffered(1))
    w_out, w_gu, w_down = w_bf
    assert w_out.shape == (MLA_WIDTH + RET_WIDTH, D_MODEL) and w_gu.shape == (D_MODEL, 2 * D_FF)
    assert w_down.shape == (D_FF, D_MODEL)
    return pl.pallas_call(
        _post_kernel, grid=(B, R // tm),
        in_specs=[row(D_MODEL), row(MLA_WIDTH), row(RET_WIDTH), whole(w_out),
                  _gain_spec(l, G_FFN, D_MODEL), whole(w_gu), whole(w_down)],
        out_specs=row(D_MODEL),
        out_shape=jax.ShapeDtypeStruct((B, R, D_MODEL), f32),
        scratch_shapes=[pltpu.VMEM((tm, D_FF), bf16)],
        name=name, compiler_params=_params(2),
    )(h, ym, yr, w_out, p["gains"], w_gu, w_down)


W_IN_COLS = 2 * Q_LORA + QK_ROPE + 4 * RET_WIDTH
O_KPE = Q_LORA + KV_LORA
O_RQ = O_KPE + QK_ROPE
O_RK = O_RQ + RET_WIDTH
O_RV = O_RK + RET_WIDTH
WPREP_COLS = 512


def _wprep_kernel(w_ref, head_ref, qk_ref, vg_ref):
    def put(dst_ref, dst, src, n):
        dst_ref[dst:dst + n, :] = w_ref[src:src + n, :].astype(bf16)

    q = QK_ROPE // 2
    head_ref[...] = jnp.zeros(head_ref.shape, bf16)
    put(head_ref, 0, 0, O_KPE)
    put(head_ref, C_KPE + HALF, O_KPE, q)
    put(head_ref, C_KPE + LANES + HALF, O_KPE + q, q)
    for s, src0 in enumerate((O_RQ, O_RK)):
        for p in range(RET_HEADS // 2):
            src, dst = src0 + p * RET_PAIR, s * RET_WIDTH + p * RET_PAIR
            put(qk_ref, dst, src, HALF)
            put(qk_ref, dst + HALF, src + 2 * HALF, HALF)
            put(qk_ref, dst + 2 * HALF, src + HALF, HALF)
            put(qk_ref, dst + 3 * HALF, src + 3 * HALF, HALF)
    put(vg_ref, 0, O_RV, 2 * RET_WIDTH)


def _wprep_call(w_in):
    depth = w_in.shape[0]
    wt = jnp.swapaxes(w_in, 1, 2)
    cb = WPREP_COLS
    out = lambda n: pl.BlockSpec((None, n, cb), lambda l, c: (l, 0, c))
    rows = (C_RQ, C_RV - C_RQ, N_IN_PAD - C_RV)
    return pl.pallas_call(
        _wprep_kernel, grid=(depth, D_MODEL // cb),
        in_specs=[out(W_IN_COLS)], out_specs=[out(n) for n in rows],
        out_shape=[jax.ShapeDtypeStruct((depth, n, D_MODEL), bf16) for n in rows],
        name="wprep", compiler_params=_params(2),
    )(wt)


def _rope_tables(pos, valid):
    pos = pos.astype(np.float32)
    n = pos.shape[0]

    def cs(dim):
        inv = np.float32(ROPE_BASE) ** (-np.arange(0, dim, 2, dtype=np.float32) / np.float32(dim))
        ang = pos[:, None] * inv[None, :].astype(np.float32)
        return np.cos(ang).astype(np.float32), np.sin(ang).astype(np.float32)

    cm, sm = cs(QK_ROPE)
    z32 = np.zeros_like(cm)
    ca = np.concatenate([np.ones((n, HALF), np.float32), cm, z32], axis=1)
    sa = np.concatenate([np.zeros((n, HALF), np.float32), sm, z32], axis=1)
    cr, sr = cs(RET_HEAD)
    c2 = np.concatenate([cr, cr], axis=1)
    s2 = np.concatenate([sr, sr], axis=1)
    kscale = valid.astype(np.float32)[:, None] * np.float32(RET_HEAD ** -0.5)
    return tuple(jnp.asarray(t) for t in (ca, sa, c2, s2, c2 * kscale, s2 * kscale))


def _mla_head_layout(a):
    q = QK_ROPE // 2
    z = jnp.zeros(a.shape[:-1] + (q,), a.dtype)
    return jnp.concatenate([a[..., :HALF], a[..., QK_NOPE:QK_NOPE + q], z,
                            a[..., HALF:QK_NOPE], a[..., QK_NOPE + q:], z], axis=-1)


def _prep_params(attn_norm_g, w_in, q_a_norm_g, w_q_b, kv_a_norm_g, w_kv_b, q_norm_g, k_norm_g,
                 mla_out_norm_g, ret_norm_g, ret_norm_b, ffn_norm_g):
    depth = w_in.shape[0]
    w_head, w_qk, w_vg = _wprep_call(w_in)
    wq = _mla_head_layout(w_q_b.astype(bf16).reshape(depth, Q_LORA, MLA_HEADS, QK_HEAD))
    wkv = w_kv_b.astype(bf16).reshape(depth, KV_LORA, MLA_HEADS, QK_NOPE + V_HEAD)
    z64 = jnp.zeros((depth, KV_LORA, MLA_HEADS, HALF), bf16)
    wk = jnp.concatenate([wkv[..., :HALF], z64, wkv[..., HALF:QK_NOPE], z64], axis=-1)
    wkv = jnp.concatenate([wk.reshape(depth, KV_LORA, MLA_HEADS * QK_PAD),
                           wkv[..., QK_NOPE:].reshape(depth, KV_LORA, MLA_WIDTH)], axis=-1)
    scale = QK_HEAD ** -0.5 * LOG2_E
    gains = jnp.concatenate(
        [attn_norm_g, ffn_norm_g, mla_out_norm_g, ret_norm_g, ret_norm_b, q_a_norm_g, kv_a_norm_g,
         _mla_head_layout(q_norm_g * scale), _mla_head_layout(k_norm_g)], axis=-1)[:, None, :]
    assert gains.shape[-1] == G_TOTAL
    return {
        "gains": gains,
        "w_head": w_head,
        "w_qk": w_qk,
        "w_vg": w_vg,
        "w_qb": wq.reshape(depth, Q_LORA, MLA_HEADS * QK_PAD),
        "w_kvb": wkv,
    }


def kernel(x, meta_tokens, attn_norm_g, w_in, q_a_norm_g, w_q_b, kv_a_norm_g, w_kv_b, q_norm_g,
           k_norm_g, mla_out_norm_g, ret_norm_g, ret_norm_b, w_out, ffn_norm_g, w_gate_up, w_down):
    depth = w_in.shape[0]
    hx = x
    hm = jnp.concatenate([jnp.zeros((PAD_ROWS, D_MODEL), x.dtype), meta_tokens.astype(x.dtype)])[None]
    r = np.arange(BLOCK)
    tabs_x = _rope_tables(np.arange(SEQ) + N_META, np.ones(SEQ))
    tabs_m = _rope_tables(np.maximum(r - PAD_ROWS, 0), r >= PAD_ROWS)
    gamma = np.float32(1.0) - np.float32(2.0) ** (np.float32(-5.0) - np.arange(RET_HEADS, dtype=np.float32))
    log_g = jnp.asarray(np.broadcast_to(np.log(gamma)[:, None, None], (RET_HEADS, 8, LANES)))
    p = _prep_params(attn_norm_g, w_in, q_a_norm_g, w_q_b, kv_a_norm_g, w_kv_b, q_norm_g, k_norm_g,
                     mla_out_norm_g, ret_norm_g, ret_norm_b, ffn_norm_g)
    for l in range(depth):
        q, k, v, rq, rk, rv, rg = _pre_call(hx, p, l, tabs_x, PRE_TM, "pre")
        qm, km, vm, rqm, rkm, rvm, rgm = _pre_call(hm, p, l, tabs_m, BLOCK, "pre_meta")
        y_mla, *w_bf, y_ret = _attn_ret_call((q, k, v, km, vm), (rq, rk, rv, rg, rkm, rvm), log_g,
                                             p["gains"], l, (w_out, w_gate_up, w_down))
        hx = _post_call(hx, y_mla, y_ret, p, l, w_bf, ROW_TM, "post")
        if l + 1 < depth:
            ym_mla = _attn_meta_call(qm, km, vm, p["gains"], l)
            ym_ret = _ret_meta_call(rqm, rkm, rvm, rgm, log_g, p["gains"], p["gains"], l)
            hm = _post_call(hm, ym_mla, ym_ret, p, l, w_bf, BLOCK, "post_meta")
    return hx
```
